```python
import math
import jax, jax.numpy as jnp
from jax import lax
import numpy as np

D_MODEL = 1024
BATCH = 32
SEQ = 256
DEPTH = 2
DEC_BATCH = 2
DEC_SEQ = 4096
PAST_LEN = 256

GRID_W = 64
EPS = 1e-6
N_HEADS = 8
N_KV_HEADS = 2
HEAD_DIM = 64
ATT_WIDTH = N_HEADS * HEAD_DIM
KV_WIDTH = N_KV_HEADS * HEAD_DIM
ROPE_THETA = 10000.0
Q_BLOCK = 128
N_FGROUPS = 8
FGROUP_DIM = 64
FOURIER_WIDTH = N_FGROUPS * FGROUP_DIM
EVEN_IN = ATT_WIDTH + 2 * KV_WIDTH + FOURIER_WIDTH
EVEN_MIX = ATT_WIDTH + FOURIER_WIDTH
SSM_INNER = 2 * D_MODEL
SSM_HEAD_DIM = 64
SSM_HEADS = SSM_INNER // SSM_HEAD_DIM
SSM_GROUPS = 4
SSM_STATE = 128
SSM_CONV = 3
SSM_CHUNK = 128
SSM_CONV_CH = SSM_INNER + 2 * SSM_GROUPS * SSM_STATE
ODD_IN = SSM_INNER + SSM_CONV_CH + 2 * SSM_HEADS
DT_MIN = 1e-3
DT_MAX = 1e-1
N_EXPERTS = 16
N_EXPERT_GROUPS = 4
EXPERTS_PER_GROUP = N_EXPERTS // N_EXPERT_GROUPS
TOP_K = 2
EXPERT_FF = 512
N_ATT = (DEPTH + 1) // 2
N_SSM = DEPTH // 2

kernel_name = 'hybrid_dit_attn_fourier_ssd_moe_step'


def rms_norm(x, g):
    xf = x.astype(jnp.float32)
    y = xf * lax.rsqrt(jnp.mean(xf * xf, axis=-1, keepdims=True) + EPS)
    return (y * g.astype(jnp.float32)).astype(x.dtype)


def ada_mod(cond, w_ada, b_ada):
    m = jax.nn.silu(cond) @ w_ada + b_ada
    return m.reshape(cond.shape[0], 1, 6, D_MODEL)


def grid_rope(n_tok):
    rows = n_tok // GRID_W
    t = jnp.arange(rows * GRID_W)
    row = (t // GRID_W).astype(jnp.float32)
    col = (t % GRID_W).astype(jnp.float32)
    axis_dim = HEAD_DIM // 2
    freqs = ROPE_THETA ** (-jnp.arange(0, axis_dim, 2, dtype=jnp.float32) / axis_dim)
    ang = jnp.concatenate([row[:, None] * freqs, col[:, None] * freqs], axis=-1)
    return jnp.cos(ang), jnp.sin(ang)


def apply_rope(x, cos, sin):
    xf = x.astype(jnp.float32).reshape(*x.shape[:-1], HEAD_DIM // 2, 2)
    x1, x2 = xf[..., 0], xf[..., 1]
    c = cos[None, :, None, :]
    s = sin[None, :, None, :]
    out = jnp.stack([x1 * c - x2 * s, x1 * s + x2 * c], axis=-1).reshape(x.shape)
    return out.astype(x.dtype)


def block_attention(q, k, v):
    b, L = q.shape[0], q.shape[1]
    nb = L // Q_BLOCK
    G = N_HEADS // N_KV_HEADS
    qb = q.reshape(b, nb, Q_BLOCK, N_KV_HEADS, G, HEAD_DIM).transpose(1, 0, 2, 3, 4, 5)
    scale = HEAD_DIM ** -0.5

    def one_block(qblk):
        s = jnp.einsum('bqkgd,bskd->bkgqs', qblk, k, preferred_element_type=jnp.float32) * scale
        p = jax.nn.softmax(s, axis=-1).astype(v.dtype)
        return jnp.einsum('bkgqs,bskd->bqkgd', p, v)

    o = lax.map(one_block, qb)
    return o.transpose(1, 0, 2, 3, 4, 5).reshape(b, L, ATT_WIDTH)


def fourier_mix(f, w_fourier):
    b, L, _ = f.shape
    fg = f.astype(jnp.float32).reshape(b, L, N_FGROUPS, FGROUP_DIM)
    mixed = jnp.fft.fft2(fg, axes=(1, 3), norm='ortho').real.astype(f.dtype)
    return jnp.einsum('blgc,gcd->blgd', mixed, w_fourier).reshape(b, L, FOURIER_WIDTH)


def att_fourier_mixer(h, w_in, q_gain, k_gain, w_fourier, w_out, rope, ctx_kv):
    b, L, _ = h.shape
    p = h @ w_in
    q, k, v, f = jnp.split(p, [ATT_WIDTH, ATT_WIDTH + KV_WIDTH, ATT_WIDTH + 2 * KV_WIDTH], axis=-1)
    q = rms_norm(q.reshape(b, L, N_HEADS, HEAD_DIM), q_gain)
    k = rms_norm(k.reshape(b, L, N_KV_HEADS, HEAD_DIM), k_gain)
    v = v.reshape(b, L, N_KV_HEADS, HEAD_DIM)
    if ctx_kv is None:
        att = block_attention(q, k, v)
    else:
        cos, sin = rope
        q_r = apply_rope(q, cos, sin)
        k_r = apply_rope(k, cos, sin)
        k_all = jnp.concatenate([k_r, ctx_kv[0].astype(k.dtype)], axis=1)
        v_all = jnp.concatenate([v, ctx_kv[1].astype(v.dtype)], axis=1)
        att = block_attention(q_r, k_all, v_all)
    four = fourier_mix(f, w_fourier)
    out = jnp.concatenate([att, four], axis=-1) @ w_out
    return out, k, v


def depthwise_conv(x, w, bias):
    y = lax.conv_general_dilated(x, w[:, None, :].astype(x.dtype), window_strides=(1,),
                                 padding=[(SSM_CONV // 2, SSM_CONV // 2)],
                                 dimension_numbers=('NWC', 'WIO', 'NWC'),
                                 feature_group_count=x.shape[-1])
    return y + bias


def segsum(a):
    T = a.shape[-1]
    ae = jnp.broadcast_to(a[..., :, None], a.shape + (T,))
    ae = jnp.where(jnp.tril(jnp.ones((T, T), bool), -1), ae, 0.0)
    cs = jnp.cumsum(ae, axis=-2)
    return jnp.where(jnp.tril(jnp.ones((T, T), bool)), cs, -jnp.inf)


def ssd_scan(x, dt, A, Bm, Cm, h0):
    b, L, H, P = x.shape
    G, HG, N, Q = SSM_GROUPS, SSM_HEADS // SSM_GROUPS, SSM_STATE, SSM_CHUNK
    nc = L // Q
    f32 = jnp.float32
    X = (x.astype(f32) * dt[..., None]).reshape(b, nc, Q, G, HG, P)
    a = (dt * A).reshape(b, nc, Q, G, HG).transpose(0, 3, 4, 1, 2)
    Bc = Bm.astype(f32).reshape(b, nc, Q, G, N)
    Cc = Cm.astype(f32).reshape(b, nc, Q, G, N)
    a_cs = jnp.cumsum(a, axis=-1)
    cb = jnp.einsum('bclgn,bcsgn->bgcls', Cc, Bc)
    m = cb[:, :, None] * jnp.exp(segsum(a))
    y_diag = jnp.einsum('bghcls,bcsghp->bclghp', m, X)
    decay_states = jnp.exp(a_cs[..., -1:] - a_cs)
    states = jnp.einsum('bclgn,bghcl,bclghp->bcghpn', Bc, decay_states, X)
    init = h0.astype(f32).reshape(b, 1, G, HG, P, N)
    states = jnp.concatenate([init, states], axis=1)
    chunk_decay = jnp.exp(segsum(jnp.pad(a_cs[..., -1], ((0, 0), (0, 0), (0, 0), (1, 0)))))
    new_states = jnp.einsum('bghzc,bcghpn->bzghpn', chunk_decay, states)
    y_off = jnp.einsum('bclgn,bcghpn,bghcl->bclghp', Cc, new_states[:, :-1], jnp.exp(a_cs))
    y = (y_diag + y_off).reshape(b, L, H, P)
    return y, new_states[:, -1].reshape(b, H, P, N)


def ssd_mixer(h, w_in, conv_w, conv_b, dt_bias_f, dt_bias_b, a_log_f, a_log_b, d_skip, norm_g, w_out,
              h0_f, h0_b):
    b, L, _ = h.shape
    p = h @ w_in
    z, xbc, dt_raw = jnp.split(p, [SSM_INNER, SSM_INNER + SSM_CONV_CH], axis=-1)
    xbc = jax.nn.silu(depthwise_conv(xbc, conv_w, conv_b))
    xs, Bm, Cm = jnp.split(xbc, [SSM_INNER, SSM_INNER + SSM_GROUPS * SSM_STATE], axis=-1)
    xs = xs.reshape(b, L, SSM_HEADS, SSM_HEAD_DIM)
    Bm = Bm.reshape(b, L, SSM_GROUPS, SSM_STATE)
    Cm = Cm.reshape(b, L, SSM_GROUPS, SSM_STATE)
    dtf_raw, dtb_raw = jnp.split(dt_raw.astype(jnp.float32), 2, axis=-1)
    dt_f = jax.nn.softplus(dtf_raw + dt_bias_f.astype(jnp.float32))
    dt_b = jax.nn.softplus(dtb_raw + dt_bias_b.astype(jnp.float32))
    A_f = -jnp.exp(a_log_f.astype(jnp.float32))
    A_b = -jnp.exp(a_log_b.astype(jnp.float32))
    y_f, hf_f = ssd_scan(xs, dt_f, A_f, Bm, Cm, h0_f)
    flip = lambda t: jnp.flip(t, axis=1)
    y_b, hf_b = ssd_scan(flip(xs), flip(dt_b), A_b, flip(Bm), flip(Cm), h0_b)
    y = y_f + flip(y_b) + d_skip.astype(jnp.float32)[:, None] * xs.astype(jnp.float32)
    y = y.reshape(b, L, SSM_INNER).astype(h.dtype)
    y = rms_norm(y * jax.nn.silu(z), norm_g)
    return y @ w_out, hf_f, hf_b


def moe(h, w_router, router_bias, w_gate, w_up, w_down):
    b, L, _ = h.shape
    t = h.reshape(b * L, D_MODEL)
    scores = jax.nn.sigmoid((t @ w_router).astype(jnp.float32))
    choice = scores + router_bias.astype(jnp.float32)
    grouped = choice.reshape(-1, N_EXPERT_GROUPS, EXPERTS_PER_GROUP)
    group_score = jnp.sum(lax.top_k(grouped, TOP_K)[0], axis=-1)
    best_group = jnp.argmax(group_score, axis=-1)
    in_group = (jnp.arange(N_EXPERTS) // EXPERTS_PER_GROUP)[None, :] == best_group[:, None]
    _, idx = lax.top_k(jnp.where(in_group, choice, -jnp.inf), TOP_K)
    w = jnp.take_along_axis(scores, idx, axis=-1)
    w = w / jnp.sum(w, axis=-1, keepdims=True)
    gates = jnp.sum(jax.nn.one_hot(idx, N_EXPERTS, dtype=jnp.float32) * w[..., None], axis=1).astype(h.dtype)
    a = jnp.einsum('td,edf->tef', t, w_gate)
    u = jnp.einsum('td,edf->tef', t, w_up)
    hid = jax.nn.silu(a) * u * gates[..., None]
    return jnp.einsum('tef,efd->td', hid, w_down).reshape(b, L, D_MODEL)


def setup_inputs(seed: int = 0) -> dict:
    key = jax.random.key(seed)
    ks = iter(jax.random.split(key, 48))
    f32 = jnp.float32

    def nrm(shape, s=1.0):
        return jax.random.normal(next(ks), shape, f32) * s

    def dt_bias_init(shape):
        u = jax.random.uniform(next(ks), shape, f32)
        dt = jnp.exp(u * (math.log(DT_MAX) - math.log(DT_MIN)) + math.log(DT_MIN))
        return dt + jnp.log(-jnp.expm1(-dt))

    def a_log_init(shape):
        return jnp.log(jax.random.uniform(next(ks), shape, f32, minval=1.0, maxval=16.0))

    return {
        'x_prompt': nrm((BATCH, SEQ, D_MODEL)),
        'x_sample': nrm((DEC_BATCH, DEC_SEQ, D_MODEL)),
        'cache_k': nrm((DEC_BATCH, N_ATT, PAST_LEN, N_KV_HEADS, HEAD_DIM)),
        'cache_v': nrm((DEC_BATCH, N_ATT, PAST_LEN, N_KV_HEADS, HEAD_DIM)),
        'state_fwd': nrm((DEC_BATCH, N_SSM, SSM_HEADS, SSM_HEAD_DIM, SSM_STATE), 0.5),
        'state_bwd': nrm((DEC_BATCH, N_SSM, SSM_HEADS, SSM_HEAD_DIM, SSM_STATE), 0.5),
        'c': nrm((DEC_BATCH, D_MODEL)),
        'c_ctx': nrm((D_MODEL,)),
        'w_ada': nrm((DEPTH, D_MODEL, 6 * D_MODEL), 0.3 * D_MODEL ** -0.5),
        'b_ada': nrm((DEPTH, 6 * D_MODEL), 0.02),
        'norm_mix': 1.0 + nrm((DEPTH, D_MODEL), 0.02),
        'norm_ffn': 1.0 + nrm((DEPTH, D_MODEL), 0.02),
        'w_in_att': nrm((N_ATT, D_MODEL, EVEN_IN), D_MODEL ** -0.5),
        'q_gain': 1.0 + nrm((N_ATT, HEAD_DIM), 0.02),
        'k_gain': 1.0 + nrm((N_ATT, HEAD_DIM), 0.02),
        'w_fourier': nrm((N_ATT, N_FGROUPS, FGROUP_DIM, FGROUP_DIM), FGROUP_DIM ** -0.5),
        'w_out_att': nrm((N_ATT, EVEN_MIX, D_MODEL), EVEN_MIX ** -0.5),
        'w_in_ssm': nrm((N_SSM, D_MODEL, ODD_IN), D_MODEL ** -0.5),
        'conv_w': nrm((N_SSM, SSM_CONV, SSM_CONV_CH), SSM_CONV ** -0.5),
        'conv_b': nrm((N_SSM, SSM_CONV_CH), 0.02),
        'dt_bias_f': dt_bias_init((N_SSM, SSM_HEADS)),
        'dt_bias_b': dt_bias_init((N_SSM, SSM_HEADS)),
        'a_log_f': a_log_init((N_SSM, SSM_HEADS)),
        'a_log_b': a_log_init((N_SSM, SSM_HEADS)),
        'd_skip': 1.0 + nrm((N_SSM, SSM_HEADS), 0.02),
        'ssm_norm': 1.0 + nrm((N_SSM, SSM_INNER), 0.02),
        'w_out_ssm': nrm((N_SSM, SSM_INNER, D_MODEL), SSM_INNER ** -0.5),
        'w_router': nrm((D_MODEL, N_EXPERTS), D_MODEL ** -0.5),
        'router_bias': nrm((N_EXPERTS,), 0.01),
        'w_gate': nrm((DEPTH, N_EXPERTS, D_MODEL, EXPERT_FF), D_MODEL ** -0.5),
        'w_up': nrm((DEPTH, N_EXPERTS, D_MODEL, EXPERT_FF), D_MODEL ** -0.5),
        'w_down': nrm((DEPTH, N_EXPERTS, EXPERT_FF, D_MODEL), EXPERT_FF ** -0.5),
        'norm_final': 1.0 + nrm((D_MODEL,), 0.02),
    }


def reference(x_prompt, x_sample, cache_k, cache_v, state_fwd, state_bwd, c, c_ctx,
              w_ada, b_ada, norm_mix, norm_ffn,
              w_in_att, q_gain, k_gain, w_fourier, w_out_att,
              w_in_ssm, conv_w, conv_b, dt_bias_f, dt_bias_b, a_log_f, a_log_b, d_skip, ssm_norm, w_out_ssm,
              w_router, router_bias, w_gate, w_up, w_down, norm_final):

    def trunk(x, cond, rope, kv_cache, ssm_cache):
        b = x.shape[0]
        new_k, new_v, new_f, new_b = [], [], [], []
        for layer in range(DEPTH):
            mod = ada_mod(cond, w_ada[layer], b_ada[layer])
            sh1, sc1, g1, sh2, sc2, g2 = (mod[:, :, i] for i in range(6))
            h = rms_norm(x, norm_mix[layer]) * (1 + sc1) + sh1
            if layer % 2 == 0:
                i = layer // 2
                ctx_kv = None if kv_cache is None else (kv_cache[0][:, i], kv_cache[1][:, i])
                out, k_l, v_l = att_fourier_mixer(h, w_in_att[i], q_gain[i], k_gain[i], w_fourier[i],
                                                  w_out_att[i], rope, ctx_kv)
                new_k.append(k_l)
                new_v.append(v_l)
            else:
                j = layer // 2
                if ssm_cache is None:
                    h0_f = jnp.zeros((b, SSM_HEADS, SSM_HEAD_DIM, SSM_STATE), jnp.float32)
                    h0_b = h0_f
                else:
                    h0_f, h0_b = ssm_cache[0][:, j], ssm_cache[1][:, j]
                out, s_f, s_b = ssd_mixer(h, w_in_ssm[j], conv_w[j], conv_b[j], dt_bias_f[j], dt_bias_b[j],
                                          a_log_f[j], a_log_b[j], d_skip[j], ssm_norm[j], w_out_ssm[j],
                                          h0_f, h0_b)
                new_f.append(s_f)
                new_b.append(s_b)
            x = x + g1 * out
            h = rms_norm(x, norm_ffn[layer]) * (1 + sc2) + sh2
            x = x + g2 * moe(h, w_router, router_bias, w_gate[layer], w_up[layer], w_down[layer])
        return rms_norm(x, norm_final), new_k, new_v, new_f, new_b

    y_prompt, ks, vs, sfs, sbs = trunk(x_prompt, c_ctx[None, :], None, None, None)
    rope = grid_rope(x_sample.shape[1])
    y_sample, _, _, _, _ = trunk(x_sample, c, rope, (cache_k, cache_v), (state_fwd, state_bwd))

    new_k = jnp.stack(ks, axis=1)
    new_v = jnp.stack(vs, axis=1)
    new_state_fwd = jnp.stack(sfs, axis=1)
    new_state_bwd = jnp.stack(sbs, axis=1)
    return (y_prompt, y_sample, new_k, new_v, new_state_fwd, new_state_bwd)
```

```python
import functools
import math

import numpy as np
import jax
import jax.numpy as jnp
from jax import lax
from jax.experimental import pallas as pl
from jax.experimental.pallas import tpu as pltpu
from jax.experimental.pallas import tpu_sc as plsc

F32 = jnp.float32
BF16 = jnp.bfloat16
U32 = jnp.uint32
I32 = jnp.int32
HIGHEST = lax.Precision.HIGHEST

D_MODEL = 1024
BATCH = 32
SEQ = 256
DEPTH = 2
DEC_BATCH = 2
DEC_SEQ = 4096
PAST_LEN = 256
GRID_W = 64
EPS = 1e-6
N_HEADS = 8
N_KV_HEADS = 2
HEAD_DIM = 64
ATT_WIDTH = N_HEADS * HEAD_DIM
KV_WIDTH = N_KV_HEADS * HEAD_DIM
ROPE_THETA = 10000.0
N_FGROUPS = 8
FGROUP_DIM = 64
FOURIER_WIDTH = N_FGROUPS * FGROUP_DIM
EVEN_IN = ATT_WIDTH + 2 * KV_WIDTH + FOURIER_WIDTH
SSM_INNER = 2 * D_MODEL
SSM_HEAD_DIM = 64
SSM_HEADS = SSM_INNER // SSM_HEAD_DIM
SSM_GROUPS = 4
SSM_STATE = 128
SSM_CHUNK = 128
SSM_CONV_CH = SSM_INNER + 2 * SSM_GROUPS * SSM_STATE
ODD_IN = SSM_INNER + SSM_CONV_CH + 2 * SSM_HEADS
ODD_IN_PAD = ODD_IN + 64
N_EXPERTS = 16
EXPERTS_PER_GROUP = 4
N_EXPERT_GROUPS = 4
EXPERT_FF = 512

N_CTX = BATCH * SEQ
N_LAT = DEC_BATCH * DEC_SEQ
NT = N_CTX + N_LAT
N_SEG = 1 + DEC_BATCH
LANE = 128
VMEM_LIMIT = 56 * 1024 * 1024

TM = 512
N_CTX_TILES = N_CTX // TM
LAT_TILES_PER_SEQ = DEC_SEQ // TM

PAIR_ORDER = ((0, 1), (0, 2), (0, 3), (1, 3), (1, 2), (3, 2))
N_BUCKETS = N_EXPERT_GROUPS * len(PAIR_ORDER)
BUCKET_A = tuple(g * EXPERTS_PER_GROUP + a for g in range(N_EXPERT_GROUPS) for a, _ in PAIR_ORDER)
BUCKET_B = tuple(g * EXPERTS_PER_GROUP + b for g in range(N_EXPERT_GROUPS) for _, b in PAIR_ORDER)
BUCKET_ROWS = 32
TE = 128
N_ETILES = (NT + N_BUCKETS * (TE - 1) + TE - 1) // TE
P_MAX = N_ETILES * TE
HALF = D_MODEL // 2
ROW_WORDS = HALF + LANE
SC_CORES = 2
SC_SUBCORES = 16
SC_WORKERS = SC_CORES * SC_SUBCORES
SC_CHUNK = 64


def _cparams(sem):
    return pltpu.CompilerParams(dimension_semantics=sem, vmem_limit_bytes=VMEM_LIMIT)


def _seg_of_tile(i, tm):
    nct = N_CTX // tm
    return jnp.where(i < nct, 0, 1 + (i - nct) // (DEC_SEQ // tm))


def _mod_spec(layer, which, tm):
    return pl.BlockSpec((None, None, None, 1, D_MODEL),
                        lambda i, *_: (layer, which, _seg_of_tile(i, tm), 0, 0))


def _row_spec(width, tm=TM):
    return pl.BlockSpec((tm, width), lambda i, *_: (i, 0))


def _const_spec(shape):
    nd = len(shape)
    return pl.BlockSpec(shape, lambda *_: (0,) * nd)


def _silu(x):
    return x * jax.nn.sigmoid(x)


def _mod_norm(x, gain, scale, shift):
    y = x * lax.rsqrt(jnp.mean(x * x, axis=-1, keepdims=True) + EPS) * gain
    return y * (1.0 + scale) + shift


def _ada_kernel(cond_ref, w_ref, b_ref, o_ref):
    c = cond_ref[...]
    o_ref[...] = jnp.dot(_silu(c), w_ref[...], precision=HIGHEST, preferred_element_type=F32) + b_ref[...]


def _ada_mods(cond8, w_ada, b_ada):
    tn = 1536
    out = pl.pallas_call(
        _ada_kernel,
        grid=(DEPTH, 6 * D_MODEL // tn),
        in_specs=[pl.BlockSpec((8, D_MODEL), lambda l, n: (0, 0)),
                  pl.BlockSpec((None, D_MODEL, tn), lambda l, n: (l, 0, n)),
                  pl.BlockSpec((None, 1, tn), lambda l, n: (l, 0, n))],
        out_specs=pl.BlockSpec((None, 8, tn), lambda l, n: (l, 0, n)),
        out_shape=jax.ShapeDtypeStruct((DEPTH, 8, 6 * D_MODEL), F32),
        compiler_params=_cparams(("arbitrary", "arbitrary")),
        name="ada_mod",
    )(cond8, w_ada, b_ada.reshape(DEPTH, 1, 6 * D_MODEL))
    return out.reshape(DEPTH, 8, 6, D_MODEL)[:, :N_SEG].transpose(0, 2, 1, 3)[:, :, :, None, :]


def _rope_tables():
    t = np.arange(DEC_SEQ)
    row = (t // GRID_W).astype(np.float64)
    col = (t % GRID_W).astype(np.float64)
    axis_dim = HEAD_DIM // 2
    freqs = ROPE_THETA ** (-np.arange(0, axis_dim, 2, dtype=np.float64) / axis_dim)
    ang = np.concatenate([row[:, None] * freqs, col[:, None] * freqs], axis=-1)
    cos = np.repeat(np.cos(ang), 2, axis=1)
    sin = np.repeat(np.sin(ang), 2, axis=1)
    sign = np.where(np.arange(HEAD_DIM) % 2 == 0, -1.0, 1.0)
    cos2 = np.tile(cos, (1, 2)).astype(np.float32)
    sin2 = np.tile(sin * sign, (1, 2)).astype(np.float32)
    return jnp.asarray(cos2), jnp.asarray(sin2)


def _dft_cos_sin(n, scale):
    k = np.arange(n)
    ang = 2.0 * np.pi * ((k[:, None] * k[None, :]) % n) / n
    return np.cos(ang) * scale, np.sin(ang) * scale


def _block_diag(m, reps):
    n = m.shape[0]
    out = np.zeros((n * reps, n * reps), m.dtype)
    for r in range(reps):
        out[r * n:(r + 1) * n, r * n:(r + 1) * n] = m
    return out


def _channel_dft():
    c, s = _dft_cos_sin(FGROUP_DIM, FGROUP_DIM ** -0.5)
    return jnp.asarray(np.concatenate([_block_diag(c, N_FGROUPS), _block_diag(s, N_FGROUPS)], axis=1), BF16)


def _group_ones(width):
    return jnp.asarray(_block_diag(np.ones((HEAD_DIM, HEAD_DIM), np.float32), width // HEAD_DIM), BF16)


def _pad_heads(x):
    lane = lax.broadcasted_iota(jnp.int32, x.shape, 1)
    low = lane < HEAD_DIM
    xr = pltpu.roll(x, HEAD_DIM, 1)
    zero = jnp.zeros_like(x)
    return [jnp.where(low, x, zero), jnp.where(low, zero, xr), jnp.where(low, xr, zero), jnp.where(low, zero, x)]


def _ctx_tile(i, *_):
    return (jnp.minimum(i, N_CTX_TILES - 1), 0)


def _lat_tile(i, *_):
    return (jnp.maximum(i - N_CTX_TILES, 0), 0)


def _split_row_specs(width):
    return [pl.BlockSpec((TM, width), _ctx_tile), pl.BlockSpec((TM, width), _lat_tile)]


def _in0_kernel(xp_ref, xs_ref, nm_ref, sc_ref, sh_ref, w_ref, qg_ref, kg_ref, ones_ref, cos_ref, sin_ref, dft_ref,
                q_ref, kp_ref, vp_ref, u_ref, nk_ref, nv_ref):
    i = pl.program_id(0)
    is_lat = i >= N_CTX_TILES
    x = jnp.where(is_lat, xs_ref[...], xp_ref[...])
    h = _mod_norm(x, nm_ref[...], sc_ref[...], sh_ref[...])
    p = jnp.dot(h.astype(BF16), w_ref[...], preferred_element_type=F32)
    q = p[:, :ATT_WIDTH]
    k = p[:, ATT_WIDTH:ATT_WIDTH + KV_WIDTH]
    v = p[:, ATT_WIDTH + KV_WIDTH:ATT_WIDTH + 2 * KV_WIDTH]
    f = p[:, ATT_WIDTH + 2 * KV_WIDTH:]
    ones = ones_ref[...]
    qss = jnp.dot((q * q).astype(BF16), ones, preferred_element_type=F32)
    kss = jnp.dot((k * k).astype(BF16), ones[:KV_WIDTH, :KV_WIDTH], preferred_element_type=F32)
    qn = q * lax.rsqrt(qss * (1.0 / HEAD_DIM) + EPS) * qg_ref[...]
    kn = k * lax.rsqrt(kss * (1.0 / HEAD_DIM) + EPS) * kg_ref[...]

    cos = jnp.where(is_lat, cos_ref[...], 1.0)
    sin = jnp.where(is_lat, sin_ref[...], 0.0)
    lane = lax.broadcasted_iota(jnp.int32, (TM, LANE), 1)
    even = (lane % 2) == 0

    def rope(xc):
        swapped = jnp.where(even, pltpu.roll(xc, LANE - 1, 1), pltpu.roll(xc, 1, 1))
        return xc * cos + swapped * sin

    scale = HEAD_DIM ** -0.5
    for j in range(ATT_WIDTH // LANE):
        q_ref[:, j * LANE:(j + 1) * LANE] = (rope(qn[:, j * LANE:(j + 1) * LANE]) * scale).astype(BF16)
    for j, c in enumerate(_pad_heads(rope(kn))):
        kp_ref[:, j * LANE:(j + 1) * LANE] = c.astype(BF16)
    for j, c in enumerate(_pad_heads(v)):
        vp_ref[:, j * LANE:(j + 1) * LANE] = c.astype(BF16)
    u_ref[...] = jnp.dot(f.astype(BF16), dft_ref[...], preferred_element_type=F32).astype(BF16)

    @pl.when(jnp.logical_not(is_lat))
    def _():
        nk_ref[...] = kn
        nv_ref[...] = v


def _in0(xp, xs, mods, norm_mix0, w_in, q_gain, k_gain, cos2, sin2):
    def table_idx(i):
        return (jnp.where(i < N_CTX_TILES, 0, (i - N_CTX_TILES) % LAT_TILES_PER_SEQ), 0)

    ctx_idx = _ctx_tile
    outs = pl.pallas_call(
        _in0_kernel,
        grid=(NT // TM,),
        in_specs=_split_row_specs(D_MODEL) + [_const_spec((1, D_MODEL)), _mod_spec(0, 1, TM), _mod_spec(0, 0, TM),
                  _const_spec((D_MODEL, EVEN_IN)), _const_spec((1, ATT_WIDTH)), _const_spec((1, KV_WIDTH)),
                  _const_spec((ATT_WIDTH, ATT_WIDTH)),
                  pl.BlockSpec((TM, LANE), table_idx), pl.BlockSpec((TM, LANE), table_idx),
                  _const_spec((FOURIER_WIDTH, 2 * FOURIER_WIDTH))],
        out_specs=[_row_spec(ATT_WIDTH), _row_spec(4 * LANE), _row_spec(4 * LANE), _row_spec(2 * FOURIER_WIDTH),
                   pl.BlockSpec((TM, KV_WIDTH), ctx_idx), pl.BlockSpec((TM, KV_WIDTH), ctx_idx)],
        out_shape=[jax.ShapeDtypeStruct((NT, ATT_WIDTH), BF16), jax.ShapeDtypeStruct((NT, 4 * LANE), BF16),
                   jax.ShapeDtypeStruct((NT, 4 * LANE), BF16), jax.ShapeDtypeStruct((NT, 2 * FOURIER_WIDTH), BF16),
                   jax.ShapeDtypeStruct((N_CTX, KV_WIDTH), F32), jax.ShapeDtypeStruct((N_CTX, KV_WIDTH), F32)],
        compiler_params=_cparams(("arbitrary",)),
        name="in_proj_att",
    )(xp, xs, norm_mix0.reshape(1, D_MODEL), mods, mods, w_in.astype(BF16),
      jnp.tile(q_gain, N_HEADS).reshape(1, ATT_WIDTH), jnp.tile(k_gain, N_KV_HEADS).reshape(1, KV_WIDTH),
      _group_ones(ATT_WIDTH), cos2, sin2, _channel_dft())
    return outs


def _att_kernel(*refs, has_cache):
    if has_cache:
        q_ref, kp_ref, vp_ref, ck_ref, cv_ref, o_ref = refs
        ckp = [c.astype(BF16) for c in _pad_heads(ck_ref[...])]
        cvp = [c.astype(BF16) for c in _pad_heads(cv_ref[...])]
    else:
        q_ref, kp_ref, vp_ref, o_ref = refs
    nt_dims = (((1,), (1,)), ((), ()))
    for j in range(ATT_WIDTH // LANE):
        qj = q_ref[:, j * LANE:(j + 1) * LANE]
        g = j // 2
        acc = None
        for half in range(2):
            c = 2 * g + half
            kk = kp_ref[:, c * LANE:(c + 1) * LANE]
            vv = vp_ref[:, c * LANE:(c + 1) * LANE]
            s = lax.dot_general(qj, kk, nt_dims, preferred_element_type=F32)
            m = jnp.max(s, axis=-1, keepdims=True)
            if has_cache:
                sc = lax.dot_general(qj, ckp[c], nt_dims, preferred_element_type=F32)
                m = jnp.maximum(m, jnp.max(sc, axis=-1, keepdims=True))
            p = jnp.exp(s - m)
            d = jnp.sum(p, axis=-1, keepdims=True)
            o = jnp.dot(p.astype(BF16), vv, preferred_element_type=F32)
            if has_cache:
                pc = jnp.exp(sc - m)
                d = d + jnp.sum(pc, axis=-1, keepdims=True)
                o = o + jnp.dot(pc.astype(BF16), cvp[c], preferred_element_type=F32)
            o = o * (1.0 / d)
            acc = o if acc is None else acc + o
        o_ref[:, j * LANE:(j + 1) * LANE] = acc.astype(BF16)


def _attention(q, kp, vp, cache_k, cache_v):
    att_ctx = pl.pallas_call(
        functools.partial(_att_kernel, has_cache=False),
        grid=(BATCH,),
        in_specs=[pl.BlockSpec((SEQ, ATT_WIDTH), lambda b: (b, 0)),
                  pl.BlockSpec((SEQ, 4 * LANE), lambda b: (b, 0)),
                  pl.BlockSpec((SEQ, 4 * LANE), lambda b: (b, 0))],
        out_specs=pl.BlockSpec((SEQ, ATT_WIDTH), lambda b: (b, 0)),
        out_shape=jax.ShapeDtypeStruct((N_CTX, ATT_WIDTH), BF16),
        compiler_params=_cparams(("arbitrary",)),
        name="attention_ctx",
    )(q, kp, vp)
    tq = 256
    off = N_CTX // DEC_SEQ
    att_lat = pl.pallas_call(
        functools.partial(_att_kernel, has_cache=True),
        grid=(DEC_BATCH, DEC_SEQ // tq),
        in_specs=[pl.BlockSpec((tq, ATT_WIDTH), lambda b, i: (N_CTX // tq + b * (DEC_SEQ // tq) + i, 0)),
                  pl.BlockSpec((DEC_SEQ, 4 * LANE), lambda b, i: (off + b, 0)),
                  pl.BlockSpec((DEC_SEQ, 4 * LANE), lambda b, i: (off + b, 0)),
                  pl.BlockSpec((None, PAST_LEN, KV_WIDTH), lambda b, i: (b, 0, 0)),
                  pl.BlockSpec((None, PAST_LEN, KV_WIDTH), lambda b, i: (b, 0, 0))],
        out_specs=pl.BlockSpec((tq, ATT_WIDTH), lambda b, i: (b * (DEC_SEQ // tq) + i, 0)),
        out_shape=jax.ShapeDtypeStruct((N_LAT, ATT_WIDTH), BF16),
        compiler_params=_cparams(("arbitrary", "arbitrary")),
        name="attention_lat",
    )(q, kp, vp, cache_k.reshape(DEC_BATCH, PAST_LEN, KV_WIDTH), cache_v.reshape(DEC_BATCH, PAST_LEN, KV_WIDTH))
    return att_ctx, att_lat


def _four_ctx_kernel(u_ref, c_ref, s_ref, o_ref):
    uc = u_ref[:, :FOURIER_WIDTH]
    us = u_ref[:, FOURIER_WIDTH:]
    o_ref[...] = (jnp.dot(c_ref[...], uc, preferred_element_type=F32)
                  - jnp.dot(s_ref[...], us, preferred_element_type=F32))


FCH = 8


def _four_lat_a_kernel(u_ref, w1_ref, w2_ref, tc_ref, ts_ref, o_ref):
    w1 = w1_ref[...]
    w2 = w2_ref[...]
    for j in range(FCH):
        uc = u_ref[:, j * 2 * FOURIER_WIDTH:j * 2 * FOURIER_WIDTH + FOURIER_WIDTH]
        us = u_ref[:, j * 2 * FOURIER_WIDTH + FOURIER_WIDTH:(j + 1) * 2 * FOURIER_WIDTH]
        z = jnp.dot(w1, uc, preferred_element_type=F32) + jnp.dot(w2, us, preferred_element_type=F32)
        zr = z[:GRID_W]
        zi = z[GRID_W:]
        tc = jnp.concatenate([tc_ref[j]] * (FOURIER_WIDTH // LANE), axis=1)
        ts = jnp.concatenate([ts_ref[j]] * (FOURIER_WIDTH // LANE), axis=1)
        o_ref[j, :, :FOURIER_WIDTH] = (zr * tc - zi * ts).astype(BF16)
        o_ref[j, :, FOURIER_WIDTH:] = (zr * ts + zi * tc).astype(BF16)


def _four_lat_b_kernel(b_ref, c_ref, s_ref, o_ref):
    c = c_ref[...]
    s = s_ref[...]
    for j in range(FCH):
        br = b_ref[:, j * 2 * FOURIER_WIDTH:j * 2 * FOURIER_WIDTH + FOURIER_WIDTH]
        bi = b_ref[:, j * 2 * FOURIER_WIDTH + FOURIER_WIDTH:(j + 1) * 2 * FOURIER_WIDTH]
        o_ref[:, j, :] = (jnp.dot(c, br, preferred_element_type=F32) - jnp.dot(s, bi, preferred_element_type=F32))


def _fourier(u):
    c256, s256 = _dft_cos_sin(SEQ, SEQ ** -0.5)
    mixed_ctx = pl.pallas_call(
        _four_ctx_kernel,
        grid=(BATCH,),
        in_specs=[pl.BlockSpec((SEQ, 2 * FOURIER_WIDTH), lambda b: (b, 0)),
                  _const_spec((SEQ, SEQ)), _const_spec((SEQ, SEQ))],
        out_specs=pl.BlockSpec((SEQ, FOURIER_WIDTH), lambda b: (b, 0)),
        out_shape=jax.ShapeDtypeStruct((N_CTX, FOURIER_WIDTH), F32),
        compiler_params=_cparams(("arbitrary",)),
        name="fourier_ctx",
    )(u, jnp.asarray(c256, BF16), jnp.asarray(s256, BF16))

    g = GRID_W
    c64, s64 = _dft_cos_sin(g, g ** -0.5)
    w1 = jnp.asarray(np.concatenate([c64, s64], axis=0), BF16)
    w2 = jnp.asarray(np.concatenate([-s64, c64], axis=0), BF16)
    t2 = np.arange(g)[:, None]
    k1 = np.arange(g)[None, :]
    ang = 2.0 * np.pi * (t2 * k1) / (g * g)
    tw_c = jnp.asarray(np.broadcast_to(np.cos(ang)[:, :, None], (g, g, LANE)), F32)
    tw_s = jnp.asarray(np.broadcast_to(np.sin(ang)[:, :, None], (g, g, LANE)), F32)
    width = 2 * FOURIER_WIDTH
    u_lat = u[N_CTX:].reshape(DEC_BATCH, g, g * width)
    stage1 = pl.pallas_call(
        _four_lat_a_kernel,
        grid=(DEC_BATCH, g // FCH),
        in_specs=[pl.BlockSpec((None, g, FCH * width), lambda b, i: (b, 0, i)),
                  _const_spec((2 * g, g)), _const_spec((2 * g, g)),
                  pl.BlockSpec((FCH, g, LANE), lambda b, i: (i, 0, 0)),
                  pl.BlockSpec((FCH, g, LANE), lambda b, i: (i, 0, 0))],
        out_specs=pl.BlockSpec((None, FCH, g, width), lambda b, i: (b, i, 0, 0)),
        out_shape=jax.ShapeDtypeStruct((DEC_BATCH, g, g, width), BF16),
        compiler_params=_cparams(("arbitrary", "arbitrary")),
        name="fourier_lat_rows",
    )(u_lat, w1, w2, tw_c, tw_s)
    stage1 = stage1.reshape(DEC_BATCH, g, g * width)
    mixed_lat = pl.pallas_call(
        _four_lat_b_kernel,
        grid=(DEC_BATCH, g // FCH),
        in_specs=[pl.BlockSpec((None, g, FCH * width), lambda b, i: (b, 0, i)),
                  _const_spec((g, g)), _const_spec((g, g))],
        out_specs=pl.BlockSpec((None, g, FCH, FOURIER_WIDTH), lambda b, i: (b, 0, i, 0)),
        out_shape=jax.ShapeDtypeStruct((DEC_BATCH, g, g, FOURIER_WIDTH), F32),
        compiler_params=_cparams(("arbitrary", "arbitrary")),
        name="fourier_lat_cols",
    )(stage1, jnp.asarray(c64, BF16), jnp.asarray(s64, BF16))
    return mixed_ctx, mixed_lat.reshape(N_LAT, FOURIER_WIDTH)


def _pack_bf16_pairs(x):
    n = x.shape[1] // 2
    lo = pltpu.bitcast(x[:, :n].astype(BF16).astype(F32), U32)
    hi = pltpu.bitcast(x[:, n:].astype(BF16).astype(F32), U32)
    return (hi & jnp.uint32(0xFFFF0000)) | (lo >> 16)


def _unpack_bf16_pairs(w):
    lo = pltpu.bitcast(w << 16, F32)
    hi = pltpu.bitcast(w & jnp.uint32(0xFFFF0000), F32)
    return jnp.concatenate([lo, hi], axis=1)


def _route(h2, wr_ref, rb_ref, tri_ref, carry_ref):
    logits = jnp.dot(h2, wr_ref[...], precision=HIGHEST, preferred_element_type=F32)
    lt = logits.T
    score = [jax.nn.sigmoid(lt[e:e + 1, :]) for e in range(N_EXPERTS)]
    choice = [score[e] + rb_ref[e:e + 1, :] for e in range(N_EXPERTS)]
    best = jnp.zeros_like(score[0], dtype=jnp.int32)
    best_v = None
    for gi in range(N_EXPERT_GROUPS):
        c = choice[gi * EXPERTS_PER_GROUP:(gi + 1) * EXPERTS_PER_GROUP]
        top2 = None
        for a in range(EXPERTS_PER_GROUP):
            for b in range(a + 1, EXPERTS_PER_GROUP):
                pair = c[a] + c[b]
                top2 = pair if top2 is None else jnp.maximum(top2, pair)
        if best_v is None:
            best_v = top2
        else:
            better = top2 > best_v
            best = jnp.where(better, gi, best)
            best_v = jnp.where(better, top2, best_v)
    sel = []
    picked = []
    for e in range(N_EXPERTS):
        gi = e // EXPERTS_PER_GROUP
        rank = jnp.zeros_like(best)
        for o in range(gi * EXPERTS_PER_GROUP, (gi + 1) * EXPERTS_PER_GROUP):
            if o == e:
                continue
            ahead = (choice[o] > choice[e]) | ((choice[o] == choice[e]) & (o < e))
            rank = rank + ahead.astype(jnp.int32)
        chosen = (best == gi) & (rank < 2)
        sel.append(jnp.where(chosen, 1.0, 0.0))
        picked.append(jnp.where(chosen, score[e], 0.0))
    total = picked[0]
    for e in range(1, N_EXPERTS):
        total = total + picked[e]
    inv = 1.0 / total
    gate = [w * inv for w in picked]

    member = [sel[BUCKET_A[k]] * sel[BUCKET_B[k]] for k in range(N_BUCKETS)]
    bucket = member[1]
    wa = member[0] * gate[BUCKET_A[0]]
    wb = member[0] * gate[BUCKET_B[0]]
    for k in range(1, N_BUCKETS):
        if k > 1:
            bucket = bucket + float(k) * member[k]
        wa = wa + member[k] * gate[BUCKET_A[k]]
        wb = wb + member[k] * gate[BUCKET_B[k]]
    tm = bucket.shape[1]
    onehot = jnp.concatenate(member + [jnp.zeros((BUCKET_ROWS - N_BUCKETS, tm), F32)], axis=0)
    earlier = jnp.dot(onehot.astype(BF16), tri_ref[...], preferred_element_type=F32)
    carry = carry_ref[...]
    rank = jnp.sum(onehot * (earlier + carry[:, 0:1]), axis=0, keepdims=True)
    carry_ref[...] = carry + jnp.sum(onehot, axis=1, keepdims=True)
    return bucket.astype(I32), rank.astype(I32), wa, wb


def _residual_router(x, out, g1_ref, nf_ref, sc2_ref, sh2_ref, wr_ref, rb_ref, tri_ref,
                     x1_ref, rows_ref, meta_ref, count_ref, carry_ref):
    @pl.when(pl.program_id(0) == 0)
    def _():
        carry_ref[...] = jnp.zeros_like(carry_ref)

    x1 = x + g1_ref[...] * out
    x1_ref[...] = x1
    h2 = _mod_norm(x1, nf_ref[...], sc2_ref[...], sh2_ref[...])
    bucket, rank, wa, wb = _route(h2, wr_ref, rb_ref, tri_ref, carry_ref)
    tm = h2.shape[0]
    rows_ref[:, :HALF] = _pack_bf16_pairs(h2)
    gates_t = jnp.concatenate([wa, wb, jnp.zeros((LANE - 2, tm), F32)], axis=0)
    rows_ref[:, HALF:] = pltpu.bitcast(gates_t.T, U32)
    meta_ref[...] = jnp.concatenate([bucket, rank, jnp.zeros((6, tm), I32)], axis=0)
    count_ref[...] = carry_ref[...]


def _out0_kernel(attc_ref, attl_ref, mixc_ref, mixl_ref, wf_ref, woa_ref, wof_ref, xp_ref, xs_ref, g1_ref, nf_ref,
                 sc2_ref, sh2_ref, wr_ref, rb_ref, tri_ref, x1_ref, rows_ref, meta_ref, count_ref, carry_ref):
    is_lat = pl.program_id(0) >= N_CTX_TILES
    att = jnp.where(is_lat, attl_ref[...], attc_ref[...])
    mix = jnp.where(is_lat, mixl_ref[...], mixc_ref[...])
    four = jnp.dot(mix.astype(BF16), wf_ref[...], preferred_element_type=F32)
    out = (jnp.dot(att, woa_ref[...], preferred_element_type=F32)
           + jnp.dot(four.astype(BF16), wof_ref[...], preferred_element_type=F32))
    x = jnp.where(is_lat, xs_ref[...], xp_ref[...])
    _residual_router(x, out, g1_ref, nf_ref, sc2_ref, sh2_ref, wr_ref, rb_ref, tri_ref,
                     x1_ref, rows_ref, meta_ref, count_ref, carry_ref)


def _router_operands(w_router, router_bias):
    wr = jnp.zeros((D_MODEL, LANE), F32).at[:, :N_EXPERTS].set(w_router)
    rb = jnp.broadcast_to(router_bias.astype(F32)[:, None], (N_EXPERTS, TM))
    tri = jnp.asarray(np.triu(np.ones((TM, TM), np.float32), 1), BF16)
    return wr, rb, tri


_EPILOGUE_OUT_SPECS = [_row_spec(D_MODEL), _row_spec(ROW_WORDS), pl.BlockSpec((8, TM), lambda i: (0, i)),
                       _const_spec((BUCKET_ROWS, LANE))]
_EPILOGUE_OUT_SHAPE = [jax.ShapeDtypeStruct((NT, D_MODEL), F32), jax.ShapeDtypeStruct((NT, ROW_WORDS), U32),
                       jax.ShapeDtypeStruct((8, NT), I32), jax.ShapeDtypeStruct((BUCKET_ROWS, LANE), F32)]
_EPILOGUE_SCRATCH = [pltpu.VMEM((BUCKET_ROWS, LANE), F32)]


def _epilogue_specs(layer):
    return [_mod_spec(layer, 2, TM), _const_spec((1, D_MODEL)), _mod_spec(layer, 4, TM),
            _mod_spec(layer, 3, TM), _const_spec((D_MODEL, LANE)), _const_spec((N_EXPERTS, TM)),
            _const_spec((TM, TM))]


def _out0(att, mixed, xp, xs, mods, w_fourier, w_out, norm_ffn0, router):
    wf = jnp.zeros((FOURIER_WIDTH, FOURIER_WIDTH), F32)
    for gi in range(N_FGROUPS):
        sl = slice(gi * FGROUP_DIM, (gi + 1) * FGROUP_DIM)
        wf = wf.at[sl, sl].set(w_fourier[gi])
    w_out = w_out.astype(BF16)
    wr, rb, tri = router
    return pl.pallas_call(
        _out0_kernel,
        grid=(NT // TM,),
        in_specs=_split_row_specs(ATT_WIDTH) + _split_row_specs(FOURIER_WIDTH)
        + [_const_spec((FOURIER_WIDTH, FOURIER_WIDTH)), _const_spec((ATT_WIDTH, D_MODEL)),
           _const_spec((FOURIER_WIDTH, D_MODEL))]
        + _split_row_specs(D_MODEL) + _epilogue_specs(0),
        out_specs=_EPILOGUE_OUT_SPECS,
        out_shape=_EPILOGUE_OUT_SHAPE,
        scratch_shapes=_EPILOGUE_SCRATCH,
        compiler_params=_cparams(("arbitrary",)),
        name="out_proj_att",
    )(att[0], att[1], mixed[0], mixed[1], wf.astype(BF16), w_out[:ATT_WIDTH], w_out[ATT_WIDTH:], xp, xs, mods,
      norm_ffn0.reshape(1, D_MODEL), mods, mods, wr, rb, tri)


def _dispatch_tables(meta, counts):
    bucket, rank = meta[0], meta[1]
    cnt = counts[:N_BUCKETS, 0].astype(I32)
    padded = (cnt + (TE - 1)) // TE * TE
    ends = jnp.cumsum(padded)
    starts = ends - padded
    kk = jnp.arange(N_BUCKETS, dtype=I32)
    pos = rank + jnp.sum(jnp.where(bucket[None, :] == kk[:, None], starts[:, None], 0), axis=0)
    tile0 = jnp.arange(N_ETILES, dtype=I32) * TE
    used = tile0 < ends[-1]
    tb = jnp.sum((tile0[:, None] >= ends[None, :]).astype(I32), axis=1)
    tb = jnp.where(used, tb, jnp.max(jnp.where(used, tb, 0)))
    pick = tb[:, None] == kk[None, :]

    def per_tile(table):
        return jnp.sum(jnp.where(pick, table[None, :], 0), axis=1)

    nrow = jnp.where(used, jnp.clip(per_tile(starts + cnt) - tile0, 0, TE), 0)
    pos3 = pos.reshape(SC_WORKERS, NT // (SC_WORKERS * SC_CHUNK), SC_CHUNK)
    return pos3, per_tile(jnp.asarray(BUCKET_A, I32)), per_tile(jnp.asarray(BUCKET_B, I32)), nrow


def _sc_permute(src, pos3, n_out, scatter):
    width = src.shape[1]
    _, n_chunks, chunk = pos3.shape
    rows_per_worker = n_chunks * chunk
    mesh = plsc.VectorSubcoreMesh(core_axis_name="c", subcore_axis_name="s")

    @functools.partial(pl.kernel, out_type=jax.ShapeDtypeStruct((n_out, width), src.dtype), mesh=mesh,
                       scratch_types=[pltpu.VMEM((n_chunks, chunk), I32), pltpu.VMEM((chunk, width), src.dtype)])
    def permute(src_hbm, pos_hbm, out_hbm, pos_v, buf):
        worker = lax.axis_index("s") * SC_CORES + lax.axis_index("c")
        base = worker * rows_per_worker
        pltpu.sync_copy(pos_hbm.at[worker], pos_v)
        for j in range(n_chunks):
            own = pl.ds(base + j * chunk, chunk)
            if scatter:
                pltpu.sync_copy(src_hbm.at[own], buf)
                pltpu.sync_copy(buf, out_hbm.at[pos_v.at[j]])
            else:
                pltpu.sync_copy(src_hbm.at[pos_v.at[j]], buf)
                pltpu.sync_copy(buf, out_hbm.at[own])

    return permute(src, pos3)


def _experts_kernel(ea_ref, eb_ref, nrow_ref, rows_ref, wga_ref, wua_ref, wda_ref, wgb_ref, wub_ref, wdb_ref, y_ref,
                    cga, cua, cda, cgb, cub, cdb):
    j = pl.program_id(0)
    n = nrow_ref[j]

    @pl.when(n == 0)
    def _():
        y_ref[...] = jnp.zeros_like(y_ref)

    @pl.when(n > 0)
    def _():
        prev = jnp.maximum(j - 1, 0)

        @pl.when(jnp.logical_or(j == 0, ea_ref[j] != ea_ref[prev]))
        def _():
            cga[...] = wga_ref[...].astype(BF16)
            cua[...] = wua_ref[...].astype(BF16)
            cda[...] = wda_ref[...].astype(BF16)

        @pl.when(jnp.logical_or(j == 0, eb_ref[j] != eb_ref[prev]))
        def _():
            cgb[...] = wgb_ref[...].astype(BF16)
            cub[...] = wub_ref[...].astype(BF16)
            cdb[...] = wdb_ref[...].astype(BF16)

        valid = lax.broadcasted_iota(I32, (TE, 1), 0) < n
        h = jnp.where(valid, _unpack_bf16_pairs(rows_ref[:, :HALF]), 0.0).astype(BF16)
        gates = jnp.where(valid, pltpu.bitcast(rows_ref[:, HALF:], F32), 0.0)
        y = None
        for cg, cu, cd, col in ((cga, cua, cda, 0), (cgb, cub, cdb, 1)):
            a = jnp.dot(h, cg[...], preferred_element_type=F32)
            u = jnp.dot(h, cu[...], preferred_element_type=F32)
            hid = _silu(a) * u * gates[:, col:col + 1]
            o = jnp.dot(hid.astype(BF16), cd[...], preferred_element_type=F32)
            y = o if y is None else y + o
        y_ref[...] = _pack_bf16_pairs(y)


def _experts(rows_sorted, ea, eb, nrow, layer, w_gate, w_up, w_down):
    def w_spec(shape, which):
        return pl.BlockSpec((None, None) + shape, lambda j, ea, eb, nr: (layer, (ea, eb)[which][j], 0, 0))

    up_shape, down_shape = (D_MODEL, EXPERT_FF), (EXPERT_FF, D_MODEL)
    grid_spec = pltpu.PrefetchScalarGridSpec(
        num_scalar_prefetch=3,
        grid=(N_ETILES,),
        in_specs=[pl.BlockSpec((TE, ROW_WORDS), lambda j, *_: (j, 0)),
                  w_spec(up_shape, 0), w_spec(up_shape, 0), w_spec(down_shape, 0),
                  w_spec(up_shape, 1), w_spec(up_shape, 1), w_spec(down_shape, 1)],
        out_specs=pl.BlockSpec((TE, HALF), lambda j, *_: (j, 0)),
        scratch_shapes=[pltpu.VMEM(up_shape, BF16), pltpu.VMEM(up_shape, BF16), pltpu.VMEM(down_shape, BF16),
                        pltpu.VMEM(up_shape, BF16), pltpu.VMEM(up_shape, BF16), pltpu.VMEM(down_shape, BF16)])
    return pl.pallas_call(
        _experts_kernel,
        grid_spec=grid_spec,
        out_shape=jax.ShapeDtypeStruct((P_MAX, HALF), U32),
        compiler_params=_cparams(("arbitrary",)),
        name="experts_layer%d" % layer,
    )(ea, eb, nrow, rows_sorted, w_gate, w_up, w_down, w_gate, w_up, w_down)


def _moe(rows, meta, counts, layer, w_gate, w_up, w_down):
    pos3, ea, eb, nrow = _dispatch_tables(meta, counts)
    rows_sorted = _sc_permute(rows, pos3, P_MAX, scatter=True)
    y_sorted = _experts(rows_sorted, ea, eb, nrow, layer, w_gate, w_up, w_down)
    return _sc_permute(y_sorted, pos3, NT, scatter=False)


def _final_kernel(x_ref, y_ref, g2_ref, nfin_ref, yp_ref, ys_ref):
    i = pl.program_id(0)
    x = x_ref[...] + g2_ref[...] * _unpack_bf16_pairs(y_ref[...])
    out = x * lax.rsqrt(jnp.mean(x * x, axis=-1, keepdims=True) + EPS) * nfin_ref[...]

    @pl.when(i < N_CTX_TILES)
    def _():
        yp_ref[...] = out

    @pl.when(i >= N_CTX_TILES)
    def _():
        ys_ref[...] = out


def _final(x1, y_tok, mods, norm_final):
    return pl.pallas_call(
        _final_kernel,
        grid=(NT // TM,),
        in_specs=[_row_spec(D_MODEL), _row_spec(HALF), _mod_spec(DEPTH - 1, 5, TM), _const_spec((1, D_MODEL))],
        out_specs=_split_row_specs(D_MODEL),
        out_shape=[jax.ShapeDtypeStruct((N_CTX, D_MODEL), F32), jax.ShapeDtypeStruct((N_LAT, D_MODEL), F32)],
        compiler_params=_cparams(("arbitrary",)),
        name="final_norm",
    )(x1, y_tok, mods, norm_final.reshape(1, D_MODEL))


def _in1_kernel(x_ref, y_ref, g2_ref, nm_ref, sc_ref, sh_ref, w_ref, x2_ref, z_ref, xbc_ref, dt_ref):
    x = x_ref[...] + g2_ref[...] * _unpack_bf16_pairs(y_ref[...])
    x2_ref[...] = x
    h = _mod_norm(x, nm_ref[...], sc_ref[...], sh_ref[...]).astype(BF16)
    step = 512
    for c0 in range(0, SSM_INNER, step):
        z_ref[:, c0:c0 + step] = jnp.dot(h, w_ref[:, c0:c0 + step], preferred_element_type=F32).astype(BF16)
    for c0 in range(0, SSM_CONV_CH, step):
        xbc_ref[:, c0:c0 + step] = jnp.dot(h, w_ref[:, SSM_INNER + c0:SSM_INNER + c0 + step],
                                            preferred_element_type=F32).astype(BF16)
    dt_ref[...] = jnp.dot(h, w_ref[:, SSM_INNER + SSM_CONV_CH:], preferred_element_type=F32)


def _in1(x1, y_tok, mods, norm_mix1, w_in):
    w = jnp.concatenate([w_in, jnp.zeros((D_MODEL, ODD_IN_PAD - ODD_IN), F32)], axis=1).astype(BF16)
    return pl.pallas_call(
        _in1_kernel,
        grid=(NT // TM,),
        in_specs=[_row_spec(D_MODEL), _row_spec(HALF), _mod_spec(0, 5, TM), _const_spec((1, D_MODEL)),
                  _mod_spec(1, 1, TM), _mod_spec(1, 0, TM), _const_spec((D_MODEL, ODD_IN_PAD))],
        out_specs=[_row_spec(D_MODEL), _row_spec(SSM_INNER), _row_spec(SSM_CONV_CH), _row_spec(LANE)],
        out_shape=[jax.ShapeDtypeStruct((NT, D_MODEL), F32), jax.ShapeDtypeStruct((NT, SSM_INNER), BF16),
                   jax.ShapeDtypeStruct((NT, SSM_CONV_CH), BF16), jax.ShapeDtypeStruct((NT, LANE), F32)],
        compiler_params=_cparams(("arbitrary",)),
        name="in_proj_ssm",
    )(x1, y_tok, mods, norm_mix1.reshape(1, D_MODEL), mods, mods, w)


TCV = 256


def _conv_kernel(x_ref, prev_ref, next_ref, w_ref, b_ref, o_ref):
    i = pl.program_id(0)
    nct = N_CTX // TCV
    j = (i - nct) % (DEC_SEQ // TCV)
    first = jnp.logical_or(i < nct, j == 0)
    last = jnp.logical_or(i < nct, j == DEC_SEQ // TCV - 1)
    x = x_ref[...].astype(F32)
    prev_row = jnp.where(first, 0.0, prev_ref[7:8, :].astype(F32))
    next_row = jnp.where(last, 0.0, next_ref[0:1, :].astype(F32))
    row = lax.broadcasted_iota(jnp.int32, x.shape, 0)
    xm1 = jnp.where(row == 0, prev_row, pltpu.roll(x, 1, 0))
    xp1 = jnp.where(row == TCV - 1, next_row, pltpu.roll(x, TCV - 1, 0))
    y = xm1 * w_ref[0:1, :] + x * w_ref[1:2, :] + xp1 * w_ref[2:3, :] + b_ref[...]
    o_ref[...] = _silu(y).astype(BF16)


def _conv(xbc, conv_w, conv_b):
    r8 = TCV // 8
    nblk8 = NT // 8
    return pl.pallas_call(
        _conv_kernel,
        grid=(NT // TCV,),
        in_specs=[_row_spec(SSM_CONV_CH, TCV),
                  pl.BlockSpec((8, SSM_CONV_CH), lambda i: (jnp.maximum(i * r8 - 1, 0), 0)),
                  pl.BlockSpec((8, SSM_CONV_CH), lambda i: (jnp.minimum((i + 1) * r8, nblk8 - 1), 0)),
                  _const_spec((3, SSM_CONV_CH)), _const_spec((1, SSM_CONV_CH))],
        out_specs=_row_spec(SSM_CONV_CH, TCV),
        out_shape=jax.ShapeDtypeStruct((NT, SSM_CONV_CH), BF16),
        compiler_params=_cparams(("arbitrary",)),
        name="ssm_conv",
    )(xbc, xbc, xbc, conv_w, conv_b.reshape(1, SSM_CONV_CH))


HPG = SSM_HEADS // SSM_GROUPS
GW = HPG * SSM_HEAD_DIM


def _expand_heads(cols, first_col):
    rows = cols.shape[0]
    lane = lax.broadcasted_iota(jnp.int32, (rows, LANE), 1)
    low = lane < SSM_HEAD_DIM
    parts = []
    for j in range(HPG // 2):
        a = jnp.broadcast_to(cols[:, first_col + 2 * j:first_col + 2 * j + 1], (rows, LANE))
        b = jnp.broadcast_to(cols[:, first_col + 2 * j + 1:first_col + 2 * j + 2], (rows, LANE))
        parts.append(jnp.where(low, a, b))
    return jnp.concatenate(parts, axis=1)


def _ssd_kernel(*refs, reverse, has_init, write_state):
    refs = list(refs)
    xc_ref, dt_ref, bias_ref, alog_ref = refs[:4]
    refs = refs[4:]
    h0_ref = refs.pop(0) if has_init else None
    y_ref = refs.pop(0)
    hout_ref = refs.pop(0) if write_state else None
    (h_ref,) = refs
    c = pl.program_id(1)
    nc = pl.num_programs(1)
    col0 = SSM_HEADS if reverse else 0
    Q = SSM_CHUNK

    @pl.when(c == 0)
    def _():
        for g in range(SSM_GROUPS):
            if has_init:
                h_ref[g] = h0_ref[g * GW:(g + 1) * GW, :].T
            else:
                h_ref[g] = jnp.zeros((SSM_STATE, GW), F32)

    dt = jax.nn.softplus(dt_ref[...] + bias_ref[...])
    a = dt * -jnp.exp(alog_ref[...])
    row = lax.broadcasted_iota(jnp.int32, (Q, Q), 0)
    col = lax.broadcasted_iota(jnp.int32, (Q, Q), 1)
    keep = (col >= row) if reverse else (col <= row)
    tri = keep.astype(F32)
    acs = jnp.dot(tri, a, precision=HIGHEST, preferred_element_type=F32)
    acs_t = acs.T
    dt_t = dt.T
    edge = (0 if reverse else Q - 1)
    acs_end = acs[edge:edge + 1, :]
    decay_in = jnp.exp(acs)
    decay_out = jnp.exp(acs_end - acs) * dt
    chunk_decay = jnp.exp(jnp.broadcast_to(acs_end, (8, LANE)))
    nt_dims = (((1,), (1,)), ((), ()))
    tn_dims = (((0,), (0,)), ((), ()))
    lane = lax.broadcasted_iota(jnp.int32, (Q, LANE), 1)
    low = lane < SSM_HEAD_DIM
    for g in range(SSM_GROUPS):
        xs = xc_ref[:, g * GW:(g + 1) * GW]
        bg = xc_ref[:, SSM_INNER + g * SSM_STATE:SSM_INNER + (g + 1) * SSM_STATE]
        cg = xc_ref[:, SSM_INNER + SSM_GROUPS * SSM_STATE + g * SSM_STATE:
                    SSM_INNER + SSM_GROUPS * SSM_STATE + (g + 1) * SSM_STATE]
        cb = lax.dot_general(cg, bg, nt_dims, preferred_element_type=F32)
        ht = h_ref[g]
        y_off = jnp.dot(cg, ht.astype(BF16), preferred_element_type=F32) * _expand_heads(decay_in, col0 + g * HPG)
        xw = (xs.astype(F32) * _expand_heads(decay_out, col0 + g * HPG)).astype(BF16)
        st = lax.dot_general(bg, xw, tn_dims, preferred_element_type=F32)
        h_ref[g] = ht * _expand_heads(chunk_decay, col0 + g * HPG)[0:1, :] + st
        for j in range(HPG // 2):
            ms = []
            for hh in (2 * j, 2 * j + 1):
                cidx = col0 + g * HPG + hh
                seg = acs[:, cidx:cidx + 1] - acs_t[cidx:cidx + 1, :]
                lmat = jnp.where(keep, jnp.exp(jnp.where(keep, seg, 0.0)), 0.0)
                ms.append((cb * lmat * dt_t[cidx:cidx + 1, :]).astype(BF16))
            m2 = jnp.concatenate(ms, axis=1)
            xp = xs[:, j * LANE:(j + 1) * LANE]
            zero = jnp.zeros_like(xp)
            xbd = jnp.concatenate([jnp.where(low, xp, zero), jnp.where(low, zero, xp)], axis=0)
            y_diag = jnp.dot(m2, xbd, preferred_element_type=F32)
            y_ref[:, g * GW + j * LANE:g * GW + (j + 1) * LANE] = (
                y_diag + y_off[:, j * LANE:(j + 1) * LANE]).astype(BF16)

    if write_state:
        @pl.when(c == nc - 1)
        def _():
            for g in range(SSM_GROUPS):
                hout_ref[g * GW:(g + 1) * GW, :] = h_ref[g].T


def _ssd_direction(xc, dt_raw, bias128, alog128, h0, reverse):
    outs = []
    for (row0, nseq, seqlen, has_init, write_state) in ((0, BATCH, SEQ, False, True),
                                                         (N_CTX, DEC_BATCH, DEC_SEQ, True, False)):
        nc = seqlen // SSM_CHUNK
        base = row0 // SSM_CHUNK

        def chunk_idx(s, c, nc=nc, base=base):
            cc = (nc - 1 - c) if reverse else c
            return (base + s * nc + cc, 0)

        def out_idx(s, c, nc=nc):
            cc = (nc - 1 - c) if reverse else c
            return (s * nc + cc, 0)

        state_spec = pl.BlockSpec((None, SSM_INNER, SSM_STATE), lambda s, c: (s, 0, 0))
        in_specs = [pl.BlockSpec((SSM_CHUNK, SSM_CONV_CH), chunk_idx), pl.BlockSpec((SSM_CHUNK, LANE), chunk_idx),
                    _const_spec((1, LANE)), _const_spec((1, LANE))]
        args = [xc, dt_raw, bias128, alog128]
        if has_init:
            in_specs.append(state_spec)
            args.append(h0)
        out_specs = [pl.BlockSpec((SSM_CHUNK, SSM_INNER), out_idx)]
        out_shape = [jax.ShapeDtypeStruct((nseq * seqlen, SSM_INNER), BF16)]
        if write_state:
            out_specs.append(state_spec)
            out_shape.append(jax.ShapeDtypeStruct((nseq, SSM_INNER, SSM_STATE), F32))
        outs.append(pl.pallas_call(
            functools.partial(_ssd_kernel, reverse=reverse, has_init=has_init, write_state=write_state),
            grid=(nseq, nc),
            in_specs=in_specs,
            out_specs=out_specs,
            out_shape=out_shape,
            scratch_shapes=[pltpu.VMEM((SSM_GROUPS, SSM_STATE, GW), F32)],
            compiler_params=_cparams(("arbitrary", "arbitrary")),
            name="ssd_%s_%s" % ("bwd" if reverse else "fwd", "lat" if has_init else "ctx"),
        )(*args))
    return (outs[0][0], outs[1][0]), outs[0][1]


def _out1_kernel(yfc_ref, yfl_ref, ybc_ref, ybl_ref, xs_ref, z_ref, dskip_ref, ng_ref, wo_ref, x_ref, g1_ref, nf_ref,
                 sc2_ref, sh2_ref, wr_ref, rb_ref, tri_ref, x1_ref, rows_ref, meta_ref, count_ref, carry_ref):
    is_lat = pl.program_id(0) >= N_CTX_TILES
    yf = jnp.where(is_lat, yfl_ref[...], yfc_ref[...])
    yb = jnp.where(is_lat, ybl_ref[...], ybc_ref[...])
    y = yf.astype(F32) + yb.astype(F32) + dskip_ref[...] * xs_ref[...].astype(F32)
    y = y * _silu(z_ref[...].astype(F32))
    y = y * lax.rsqrt(jnp.mean(y * y, axis=-1, keepdims=True) + EPS) * ng_ref[...]
    out = jnp.dot(y.astype(BF16), wo_ref[...], preferred_element_type=F32)
    _residual_router(x_ref[...], out, g1_ref, nf_ref, sc2_ref, sh2_ref, wr_ref, rb_ref, tri_ref,
                     x1_ref, rows_ref, meta_ref, count_ref, carry_ref)


def _out1(yf, yb, xc, z, d_skip, ssm_norm, w_out, x, mods, norm_ffn1, router):
    wr, rb, tri = router
    return pl.pallas_call(
        _out1_kernel,
        grid=(NT // TM,),
        in_specs=_split_row_specs(SSM_INNER) + _split_row_specs(SSM_INNER)
        + [_row_spec(SSM_INNER), _row_spec(SSM_INNER),
           _const_spec((1, SSM_INNER)), _const_spec((1, SSM_INNER)), _const_spec((SSM_INNER, D_MODEL)),
           _row_spec(D_MODEL)] + _epilogue_specs(1),
        out_specs=_EPILOGUE_OUT_SPECS,
        out_shape=_EPILOGUE_OUT_SHAPE,
        scratch_shapes=_EPILOGUE_SCRATCH,
        compiler_params=_cparams(("arbitrary",)),
        name="out_proj_ssm",
    )(yf[0], yf[1], yb[0], yb[1], xc, z, jnp.repeat(d_skip.astype(F32), SSM_HEAD_DIM).reshape(1, SSM_INNER),
      ssm_norm.reshape(1, SSM_INNER), w_out.astype(BF16), x, mods, norm_ffn1.reshape(1, D_MODEL), mods, mods,
      wr, rb, tri)


def kernel(x_prompt, x_sample, cache_k, cache_v, state_fwd, state_bwd, c, c_ctx, w_ada, b_ada, norm_mix, norm_ffn,
           w_in_att, q_gain, k_gain, w_fourier, w_out_att, w_in_ssm, conv_w, conv_b, dt_bias_f, dt_bias_b, a_log_f,
           a_log_b, d_skip, ssm_norm, w_out_ssm, w_router, router_bias, w_gate, w_up, w_down, norm_final):
    xp = x_prompt.reshape(N_CTX, D_MODEL)
    xs = x_sample.reshape(N_LAT, D_MODEL)
    cond8 = jnp.zeros((8, D_MODEL), F32).at[0].set(c_ctx).at[1:1 + DEC_BATCH].set(c)
    mods = _ada_mods(cond8, w_ada, b_ada)
    router = _router_operands(w_router, router_bias)
    cos2, sin2 = _rope_tables()

    q, kp, vp, u, new_k, new_v = _in0(xp, xs, mods, norm_mix[0], w_in_att[0], q_gain[0], k_gain[0], cos2, sin2)
    att = _attention(q, kp, vp, cache_k[:, 0], cache_v[:, 0])
    mixed = _fourier(u)
    x1, rows, meta, counts = _out0(att, mixed, xp, xs, mods, w_fourier[0], w_out_att[0], norm_ffn[0], router)
    y_tok = _moe(rows, meta, counts, 0, w_gate, w_up, w_down)

    x2, z, xbc, dt_raw = _in1(x1, y_tok, mods, norm_mix[1], w_in_ssm[0])
    xc = _conv(xbc, conv_w[0], conv_b[0])
    pad = jnp.zeros((LANE - 2 * SSM_HEADS,), F32)
    bias128 = jnp.concatenate([dt_bias_f[0], dt_bias_b[0], pad]).astype(F32).reshape(1, LANE)
    alog128 = jnp.concatenate([a_log_f[0], a_log_b[0], pad]).astype(F32).reshape(1, LANE)
    yf, sf = _ssd_direction(xc, dt_raw, bias128, alog128, state_fwd.reshape(DEC_BATCH, SSM_INNER, SSM_STATE), False)
    yb, sb = _ssd_direction(xc, dt_raw, bias128, alog128, state_bwd.reshape(DEC_BATCH, SSM_INNER, SSM_STATE), True)
    x3, rows, meta, counts = _out1(yf, yb, xc, z, d_skip[0], ssm_norm[0], w_out_ssm[0], x2, mods, norm_ffn[1], router)
    y_tok = _moe(rows, meta, counts, 1, w_gate, w_up, w_down)
    y_prompt, y_sample = _final(x3, y_tok, mods, norm_final)

    state_shape = (BATCH, 1, SSM_HEADS, SSM_HEAD_DIM, SSM_STATE)
    return (y_prompt.reshape(BATCH, SEQ, D_MODEL), y_sample.reshape(DEC_BATCH, DEC_SEQ, D_MODEL),
            new_k.reshape(BATCH, 1, SEQ, N_KV_HEADS, HEAD_DIM), new_v.reshape(BATCH, 1, SEQ, N_KV_HEADS, HEAD_DIM),
            sf.reshape(state_shape), sb.reshape(state_shape))
```

```python
import functools
import math

import numpy as np
import jax
import jax.numpy as jnp
from jax import lax
from jax.experimental import pallas as pl
from jax.experimental.pallas import tpu as pltpu
from jax.experimental.pallas import tpu_sc as plsc

F32 = jnp.float32
BF16 = jnp.bfloat16
U32 = jnp.uint32
I32 = jnp.int32
HIGHEST = lax.Precision.HIGHEST

D_MODEL = 1024
BATCH = 32
SEQ = 256
DEPTH = 2
DEC_BATCH = 2
DEC_SEQ = 4096
PAST_LEN = 256
GRID_W = 64
EPS = 1e-6
N_HEADS = 8
N_KV_HEADS = 2
HEAD_DIM = 64
ATT_WIDTH = N_HEADS * HEAD_DIM
KV_WIDTH = N_KV_HEADS * HEAD_DIM
ROPE_THETA = 10000.0
N_FGROUPS = 8
FGROUP_DIM = 64
FOURIER_WIDTH = N_FGROUPS * FGROUP_DIM
EVEN_IN = ATT_WIDTH + 2 * KV_WIDTH + FOURIER_WIDTH
SSM_INNER = 2 * D_MODEL
SSM_HEAD_DIM = 64
SSM_HEADS = SSM_INNER // SSM_HEAD_DIM
SSM_GROUPS = 4
SSM_STATE = 128
SSM_CHUNK = 128
SSM_CONV_CH = SSM_INNER + 2 * SSM_GROUPS * SSM_STATE
ODD_IN = SSM_INNER + SSM_CONV_CH + 2 * SSM_HEADS
ODD_IN_PAD = ODD_IN + 64
N_EXPERTS = 16
EXPERTS_PER_GROUP = 4
N_EXPERT_GROUPS = 4
EXPERT_FF = 512

N_CTX = BATCH * SEQ
N_LAT = DEC_BATCH * DEC_SEQ
NT = N_CTX + N_LAT
N_SEG = 1 + DEC_BATCH
LANE = 128
VMEM_LIMIT = 56 * 1024 * 1024

TM = 512
N_CTX_TILES = N_CTX // TM
LAT_TILES_PER_SEQ = DEC_SEQ // TM

PAIR_ORDER = ((0, 1), (0, 2), (0, 3), (1, 3), (1, 2), (3, 2))
N_BUCKETS = N_EXPERT_GROUPS * len(PAIR_ORDER)
BUCKET_A = tuple(g * EXPERTS_PER_GROUP + a for g in range(N_EXPERT_GROUPS) for a, _ in PAIR_ORDER)
BUCKET_B = tuple(g * EXPERTS_PER_GROUP + b for g in range(N_EXPERT_GROUPS) for _, b in PAIR_ORDER)
BUCKET_ROWS = 32
TE = 256
N_ETILES = (NT + N_BUCKETS * (TE - 1) + TE - 1) // TE
P_MAX = N_ETILES * TE
HALF = D_MODEL // 2
ROW_WORDS = HALF + LANE
SC_CORES = 2
SC_SUBCORES = 16
SC_WORKERS = SC_CORES * SC_SUBCORES
SC_CHUNK = 64


def _cparams(sem):
    return pltpu.CompilerParams(dimension_semantics=sem, vmem_limit_bytes=VMEM_LIMIT)


def _seg_of_tile(i, tm):
    nct = N_CTX // tm
    return jnp.where(i < nct, 0, 1 + (i - nct) // (DEC_SEQ // tm))


def _mod_spec(layer, which, tm):
    return pl.BlockSpec((None, None, None, 1, D_MODEL),
                        lambda i, *_: (layer, which, _seg_of_tile(i, tm), 0, 0))


def _row_spec(width, tm=TM):
    return pl.BlockSpec((tm, width), lambda i, *_: (i, 0))


def _const_spec(shape):
    nd = len(shape)
    return pl.BlockSpec(shape, lambda *_: (0,) * nd)


def _silu(x):
    return x * jax.nn.sigmoid(x)


def _ada_kernel(cond_ref, w_ref, b_ref, o_ref):
    c = cond_ref[...]
    o_ref[...] = jnp.dot(_silu(c), w_ref[...], precision=HIGHEST, preferred_element_type=F32) + b_ref[...]


def _ada_mods(cond8, w_ada, b_ada):
    tn = 1536
    out = pl.pallas_call(
        _ada_kernel,
        grid=(DEPTH, 6 * D_MODEL // tn),
        in_specs=[pl.BlockSpec((8, D_MODEL), lambda l, n: (0, 0)),
                  pl.BlockSpec((None, D_MODEL, tn), lambda l, n: (l, 0, n)),
                  pl.BlockSpec((None, 1, tn), lambda l, n: (l, 0, n))],
        out_specs=pl.BlockSpec((None, 8, tn), lambda l, n: (l, 0, n)),
        out_shape=jax.ShapeDtypeStruct((DEPTH, 8, 6 * D_MODEL), F32),
        compiler_params=_cparams(("arbitrary", "arbitrary")),
        name="ada_mod",
    )(cond8, w_ada, b_ada.reshape(DEPTH, 1, 6 * D_MODEL))
    return out.reshape(DEPTH, 8, 6, D_MODEL)[:, :N_SEG].transpose(0, 2, 1, 3)[:, :, :, None, :]


def _rope_tables():
    t = np.arange(DEC_SEQ)
    row = (t // GRID_W).astype(np.float64)
    col = (t % GRID_W).astype(np.float64)
    axis_dim = HEAD_DIM // 2
    freqs = ROPE_THETA ** (-np.arange(0, axis_dim, 2, dtype=np.float64) / axis_dim)
    ang = np.concatenate([row[:, None] * freqs, col[:, None] * freqs], axis=-1)
    cos = np.repeat(np.cos(ang), 2, axis=1)
    sin = np.repeat(np.sin(ang), 2, axis=1)
    sign = np.where(np.arange(HEAD_DIM) % 2 == 0, -1.0, 1.0)
    cos2 = np.tile(cos, (1, 2)).astype(np.float32)
    sin2 = np.tile(sin * sign, (1, 2)).astype(np.float32)
    return jnp.asarray(cos2), jnp.asarray(sin2)


def _dft_cos_sin(n, scale):
    k = np.arange(n)
    ang = 2.0 * np.pi * ((k[:, None] * k[None, :]) % n) / n
    return np.cos(ang) * scale, np.sin(ang) * scale


def _block_diag(m, reps):
    n = m.shape[0]
    out = np.zeros((n * reps, n * reps), m.dtype)
    for r in range(reps):
        out[r * n:(r + 1) * n, r * n:(r + 1) * n] = m
    return out


def _channel_dft():
    c, s = _dft_cos_sin(FGROUP_DIM, FGROUP_DIM ** -0.5)
    return jnp.asarray(np.concatenate([_block_diag(c, N_FGROUPS), _block_diag(s, N_FGROUPS)], axis=1), BF16)


def _group_ones(width):
    return jnp.asarray(_block_diag(np.ones((HEAD_DIM, HEAD_DIM), np.float32), width // HEAD_DIM), BF16)


def _pad_heads(x):
    lane = lax.broadcasted_iota(jnp.int32, x.shape, 1)
    low = lane < HEAD_DIM
    xr = pltpu.roll(x, HEAD_DIM, 1)
    zero = jnp.zeros_like(x)
    return [jnp.where(low, x, zero), jnp.where(low, zero, xr), jnp.where(low, xr, zero), jnp.where(low, zero, x)]


def _ctx_tile(i, *_):
    return (jnp.minimum(i, N_CTX_TILES - 1), 0)


def _lat_tile(i, *_):
    return (jnp.maximum(i - N_CTX_TILES, 0), 0)


def _split_row_specs(width):
    return [pl.BlockSpec((TM, width), _ctx_tile), pl.BlockSpec((TM, width), _lat_tile)]


def _in0_kernel(xp_ref, xs_ref, nm_ref, sc_ref, sh_ref, w_ref, qg_ref, kg_ref, ones_ref, cos_ref, sin_ref, dft_ref,
                q_ref, kp_ref, vp_ref, u_ref, nk_ref, nv_ref, h_ref):
    i = pl.program_id(0)
    is_lat = i >= N_CTX_TILES
    gain = nm_ref[...] * (1.0 + sc_ref[...])
    shift = sh_ref[...]

    def norm_rows(rows):
        x = jnp.where(is_lat, xs_ref[rows, :], xp_ref[rows, :])
        h_ref[rows, :] = (x * lax.rsqrt(jnp.mean(x * x, axis=-1, keepdims=True) + EPS) * gain + shift).astype(BF16)

    _row_slabs(TM, SLAB, norm_rows)
    p = jnp.dot(h_ref[...], w_ref[...], preferred_element_type=F32)
    q = p[:, :ATT_WIDTH]
    k = p[:, ATT_WIDTH:ATT_WIDTH + KV_WIDTH]
    v = p[:, ATT_WIDTH + KV_WIDTH:ATT_WIDTH + 2 * KV_WIDTH]
    f = p[:, ATT_WIDTH + 2 * KV_WIDTH:]
    ones = ones_ref[...]
    qss = jnp.dot((q * q).astype(BF16), ones, preferred_element_type=F32)
    kss = jnp.dot((k * k).astype(BF16), ones[:KV_WIDTH, :KV_WIDTH], preferred_element_type=F32)
    qn = q * lax.rsqrt(qss * (1.0 / HEAD_DIM) + EPS) * qg_ref[...]
    kn = k * lax.rsqrt(kss * (1.0 / HEAD_DIM) + EPS) * kg_ref[...]

    cos = jnp.where(is_lat, cos_ref[...], 1.0)
    sin = jnp.where(is_lat, sin_ref[...], 0.0)
    lane = lax.broadcasted_iota(jnp.int32, (TM, LANE), 1)
    even = (lane % 2) == 0

    def rope(xc):
        swapped = jnp.where(even, pltpu.roll(xc, LANE - 1, 1), pltpu.roll(xc, 1, 1))
        return xc * cos + swapped * sin

    scale = HEAD_DIM ** -0.5 * math.log2(math.e)
    for j in range(ATT_WIDTH // LANE):
        q_ref[:, j * LANE:(j + 1) * LANE] = (rope(qn[:, j * LANE:(j + 1) * LANE]) * scale).astype(BF16)
    for j, c in enumerate(_pad_heads(rope(kn))):
        kp_ref[:, j * LANE:(j + 1) * LANE] = c.astype(BF16)
    for j, c in enumerate(_pad_heads(v)):
        vp_ref[:, j * LANE:(j + 1) * LANE] = c.astype(BF16)
    u_ref[...] = jnp.dot(f.astype(BF16), dft_ref[...], preferred_element_type=F32).astype(BF16)

    @pl.when(jnp.logical_not(is_lat))
    def _():
        nk_ref[...] = kn
        nv_ref[...] = v


def _in0(xp, xs, mods, norm_mix0, w_in, q_gain, k_gain, cos2, sin2):
    def table_idx(i):
        return (jnp.where(i < N_CTX_TILES, 0, (i - N_CTX_TILES) % LAT_TILES_PER_SEQ), 0)

    ctx_idx = _ctx_tile
    outs = pl.pallas_call(
        _in0_kernel,
        grid=(NT // TM,),
        in_specs=_split_row_specs(D_MODEL) + [_const_spec((1, D_MODEL)), _mod_spec(0, 1, TM), _mod_spec(0, 0, TM),
                  _const_spec((D_MODEL, EVEN_IN)), _const_spec((1, ATT_WIDTH)), _const_spec((1, KV_WIDTH)),
                  _const_spec((ATT_WIDTH, ATT_WIDTH)),
                  pl.BlockSpec((TM, LANE), table_idx), pl.BlockSpec((TM, LANE), table_idx),
                  _const_spec((FOURIER_WIDTH, 2 * FOURIER_WIDTH))],
        out_specs=[_row_spec(ATT_WIDTH), _row_spec(4 * LANE), _row_spec(4 * LANE), _row_spec(2 * FOURIER_WIDTH),
                   pl.BlockSpec((TM, KV_WIDTH), ctx_idx), pl.BlockSpec((TM, KV_WIDTH), ctx_idx)],
        out_shape=[jax.ShapeDtypeStruct((NT, ATT_WIDTH), BF16), jax.ShapeDtypeStruct((NT, 4 * LANE), BF16),
                   jax.ShapeDtypeStruct((NT, 4 * LANE), BF16), jax.ShapeDtypeStruct((NT, 2 * FOURIER_WIDTH), BF16),
                   jax.ShapeDtypeStruct((N_CTX, KV_WIDTH), F32), jax.ShapeDtypeStruct((N_CTX, KV_WIDTH), F32)],
        scratch_shapes=[pltpu.VMEM((TM, D_MODEL), BF16)],
        compiler_params=_cparams(("arbitrary",)),
        name="in_proj_att",
    )(xp, xs, norm_mix0.reshape(1, D_MODEL), mods, mods, w_in.astype(BF16),
      jnp.tile(q_gain, N_HEADS).reshape(1, ATT_WIDTH), jnp.tile(k_gain, N_KV_HEADS).reshape(1, KV_WIDTH),
      _group_ones(ATT_WIDTH), cos2, sin2, _channel_dft())
    return outs


def _att_kernel(*refs, has_cache):
    if has_cache:
        q_ref, kp_ref, vp_ref, ck_ref, cv_ref, o_ref = refs
        ckp = [c.astype(BF16) for c in _pad_heads(ck_ref[...])]
        cvp = [c.astype(BF16) for c in _pad_heads(cv_ref[...])]
    else:
        q_ref, kp_ref, vp_ref, o_ref = refs
    nt_dims = (((1,), (1,)), ((), ()))
    for j in range(ATT_WIDTH // LANE):
        qj = q_ref[:, j * LANE:(j + 1) * LANE]
        g = j // 2
        acc = None
        for half in range(2):
            c = 2 * g + half
            kk = kp_ref[:, c * LANE:(c + 1) * LANE]
            vv = vp_ref[:, c * LANE:(c + 1) * LANE]
            s = lax.dot_general(qj, kk, nt_dims, preferred_element_type=F32)
            m = jnp.max(s, axis=-1, keepdims=True)
            if has_cache:
                sc = lax.dot_general(qj, ckp[c], nt_dims, preferred_element_type=F32)
                m = jnp.maximum(m, jnp.max(sc, axis=-1, keepdims=True))
            p = jnp.exp2(s - m)
            d = jnp.sum(p, axis=-1, keepdims=True)
            o = jnp.dot(p.astype(BF16), vv, preferred_element_type=F32)
            if has_cache:
                pc = jnp.exp2(sc - m)
                d = d + jnp.sum(pc, axis=-1, keepdims=True)
                o = o + jnp.dot(pc.astype(BF16), cvp[c], preferred_element_type=F32)
            o = o * (1.0 / d)
            acc = o if acc is None else acc + o
        o_ref[:, j * LANE:(j + 1) * LANE] = acc.astype(BF16)


def _attention(q, kp, vp, cache_k, cache_v):
    att_ctx = pl.pallas_call(
        functools.partial(_att_kernel, has_cache=False),
        grid=(BATCH,),
        in_specs=[pl.BlockSpec((SEQ, ATT_WIDTH), lambda b: (b, 0)),
                  pl.BlockSpec((SEQ, 4 * LANE), lambda b: (b, 0)),
                  pl.BlockSpec((SEQ, 4 * LANE), lambda b: (b, 0))],
        out_specs=pl.BlockSpec((SEQ, ATT_WIDTH), lambda b: (b, 0)),
        out_shape=jax.ShapeDtypeStruct((N_CTX, ATT_WIDTH), BF16),
        compiler_params=_cparams(("arbitrary",)),
        name="attention_ctx",
    )(q, kp, vp)
    tq = 256
    off = N_CTX // DEC_SEQ
    att_lat = pl.pallas_call(
        functools.partial(_att_kernel, has_cache=True),
        grid=(DEC_BATCH, DEC_SEQ // tq),
        in_specs=[pl.BlockSpec((tq, ATT_WIDTH), lambda b, i: (N_CTX // tq + b * (DEC_SEQ // tq) + i, 0)),
                  pl.BlockSpec((DEC_SEQ, 4 * LANE), lambda b, i: (off + b, 0)),
                  pl.BlockSpec((DEC_SEQ, 4 * LANE), lambda b, i: (off + b, 0)),
                  pl.BlockSpec((None, PAST_LEN, KV_WIDTH), lambda b, i: (b, 0, 0)),
                  pl.BlockSpec((None, PAST_LEN, KV_WIDTH), lambda b, i: (b, 0, 0))],
        out_specs=pl.BlockSpec((tq, ATT_WIDTH), lambda b, i: (b * (DEC_SEQ // tq) + i, 0)),
        out_shape=jax.ShapeDtypeStruct((N_LAT, ATT_WIDTH), BF16),
        compiler_params=_cparams(("arbitrary", "arbitrary")),
        name="attention_lat",
    )(q, kp, vp, cache_k.reshape(DEC_BATCH, PAST_LEN, KV_WIDTH), cache_v.reshape(DEC_BATCH, PAST_LEN, KV_WIDTH))
    return att_ctx, att_lat


def _four_ctx_kernel(u_ref, c_ref, s_ref, o_ref):
    uc = u_ref[:, :FOURIER_WIDTH]
    us = u_ref[:, FOURIER_WIDTH:]
    o_ref[...] = (jnp.dot(c_ref[...], uc, preferred_element_type=F32)
                  - jnp.dot(s_ref[...], us, preferred_element_type=F32))


FCH = 8


def _four_lat_a_kernel(u_ref, w1_ref, w2_ref, tc_ref, ts_ref, o_ref):
    w1 = w1_ref[...]
    w2 = w2_ref[...]
    for j in range(FCH):
        uc = u_ref[:, j * 2 * FOURIER_WIDTH:j * 2 * FOURIER_WIDTH + FOURIER_WIDTH]
        us = u_ref[:, j * 2 * FOURIER_WIDTH + FOURIER_WIDTH:(j + 1) * 2 * FOURIER_WIDTH]
        z = jnp.dot(w1, uc, preferred_element_type=F32) + jnp.dot(w2, us, preferred_element_type=F32)
        zr = z[:GRID_W]
        zi = z[GRID_W:]
        tc = jnp.concatenate([tc_ref[j]] * (FOURIER_WIDTH // LANE), axis=1)
        ts = jnp.concatenate([ts_ref[j]] * (FOURIER_WIDTH // LANE), axis=1)
        o_ref[j, :, :FOURIER_WIDTH] = (zr * tc - zi * ts).astype(BF16)
        o_ref[j, :, FOURIER_WIDTH:] = (zr * ts + zi * tc).astype(BF16)


def _four_lat_b_kernel(b_ref, c_ref, s_ref, o_ref):
    c = c_ref[...]
    s = s_ref[...]
    for j in range(FCH):
        br = b_ref[:, j * 2 * FOURIER_WIDTH:j * 2 * FOURIER_WIDTH + FOURIER_WIDTH]
        bi = b_ref[:, j * 2 * FOURIER_WIDTH + FOURIER_WIDTH:(j + 1) * 2 * FOURIER_WIDTH]
        o_ref[:, j, :] = (jnp.dot(c, br, preferred_element_type=F32) - jnp.dot(s, bi, preferred_element_type=F32))


def _fourier(u):
    c256, s256 = _dft_cos_sin(SEQ, SEQ ** -0.5)
    mixed_ctx = pl.pallas_call(
        _four_ctx_kernel,
        grid=(BATCH,),
        in_specs=[pl.BlockSpec((SEQ, 2 * FOURIER_WIDTH), lambda b: (b, 0)),
                  _const_spec((SEQ, SEQ)), _const_spec((SEQ, SEQ))],
        out_specs=pl.BlockSpec((SEQ, FOURIER_WIDTH), lambda b: (b, 0)),
        out_shape=jax.ShapeDtypeStruct((N_CTX, FOURIER_WIDTH), F32),
        compiler_params=_cparams(("arbitrary",)),
        name="fourier_ctx",
    )(u, jnp.asarray(c256, BF16), jnp.asarray(s256, BF16))

    g = GRID_W
    c64, s64 = _dft_cos_sin(g, g ** -0.5)
    w1 = jnp.asarray(np.concatenate([c64, s64], axis=0), BF16)
    w2 = jnp.asarray(np.concatenate([-s64, c64], axis=0), BF16)
    t2 = np.arange(g)[:, None]
    k1 = np.arange(g)[None, :]
    ang = 2.0 * np.pi * (t2 * k1) / (g * g)
    tw_c = jnp.asarray(np.broadcast_to(np.cos(ang)[:, :, None], (g, g, LANE)), F32)
    tw_s = jnp.asarray(np.broadcast_to(np.sin(ang)[:, :, None], (g, g, LANE)), F32)
    width = 2 * FOURIER_WIDTH
    u_lat = u[N_CTX:].reshape(DEC_BATCH, g, g * width)
    stage1 = pl.pallas_call(
        _four_lat_a_kernel,
        grid=(DEC_BATCH, g // FCH),
        in_specs=[pl.BlockSpec((None, g, FCH * width), lambda b, i: (b, 0, i)),
                  _const_spec((2 * g, g)), _const_spec((2 * g, g)),
                  pl.BlockSpec((FCH, g, LANE), lambda b, i: (i, 0, 0)),
                  pl.BlockSpec((FCH, g, LANE), lambda b, i: (i, 0, 0))],
        out_specs=pl.BlockSpec((None, FCH, g, width), lambda b, i: (b, i, 0, 0)),
        out_shape=jax.ShapeDtypeStruct((DEC_BATCH, g, g, width), BF16),
        compiler_params=_cparams(("arbitrary", "arbitrary")),
        name="fourier_lat_rows",
    )(u_lat, w1, w2, tw_c, tw_s)
    stage1 = stage1.reshape(DEC_BATCH, g, g * width)
    mixed_lat = pl.pallas_call(
        _four_lat_b_kernel,
        grid=(DEC_BATCH, g // FCH),
        in_specs=[pl.BlockSpec((None, g, FCH * width), lambda b, i: (b, 0, i)),
                  _const_spec((g, g)), _const_spec((g, g))],
        out_specs=pl.BlockSpec((None, g, FCH, FOURIER_WIDTH), lambda b, i: (b, 0, i, 0)),
        out_shape=jax.ShapeDtypeStruct((DEC_BATCH, g, g, FOURIER_WIDTH), F32),
        compiler_params=_cparams(("arbitrary", "arbitrary")),
        name="fourier_lat_cols",
    )(stage1, jnp.asarray(c64, BF16), jnp.asarray(s64, BF16))
    return mixed_ctx, mixed_lat.reshape(N_LAT, FOURIER_WIDTH)


def _pack_bf16_pairs(x):
    n = x.shape[1] // 2
    lo = pltpu.bitcast(x[:, :n].astype(BF16).astype(F32), U32)
    hi = pltpu.bitcast(x[:, n:].astype(BF16).astype(F32), U32)
    return (hi & jnp.uint32(0xFFFF0000)) | (lo >> 16)


def _unpack_bf16_pairs(w):
    lo = pltpu.bitcast(w << 16, F32)
    hi = pltpu.bitcast(w & jnp.uint32(0xFFFF0000), F32)
    return jnp.concatenate([lo, hi], axis=1)


def _route(logits, rb_ref, tri_ref, carry_ref):
    lt = logits.T
    score = [jax.nn.sigmoid(lt[e:e + 1, :]) for e in range(N_EXPERTS)]
    choice = [score[e] + rb_ref[e:e + 1, :] for e in range(N_EXPERTS)]
    best = jnp.zeros_like(score[0], dtype=jnp.int32)
    best_v = None
    for gi in range(N_EXPERT_GROUPS):
        c = choice[gi * EXPERTS_PER_GROUP:(gi + 1) * EXPERTS_PER_GROUP]
        top2 = None
        for a in range(EXPERTS_PER_GROUP):
            for b in range(a + 1, EXPERTS_PER_GROUP):
                pair = c[a] + c[b]
                top2 = pair if top2 is None else jnp.maximum(top2, pair)
        if best_v is None:
            best_v = top2
        else:
            better = top2 > best_v
            best = jnp.where(better, gi, best)
            best_v = jnp.where(better, top2, best_v)
    sel = []
    picked = []
    for e in range(N_EXPERTS):
        gi = e // EXPERTS_PER_GROUP
        rank = jnp.zeros_like(best)
        for o in range(gi * EXPERTS_PER_GROUP, (gi + 1) * EXPERTS_PER_GROUP):
            if o == e:
                continue
            ahead = (choice[o] > choice[e]) | ((choice[o] == choice[e]) & (o < e))
            rank = rank + ahead.astype(jnp.int32)
        chosen = (best == gi) & (rank < 2)
        sel.append(jnp.where(chosen, 1.0, 0.0))
        picked.append(jnp.where(chosen, score[e], 0.0))
    total = picked[0]
    for e in range(1, N_EXPERTS):
        total = total + picked[e]
    inv = 1.0 / total
    gate = [w * inv for w in picked]

    member = [sel[BUCKET_A[k]] * sel[BUCKET_B[k]] for k in range(N_BUCKETS)]
    bucket = member[1]
    wa = member[0] * gate[BUCKET_A[0]]
    wb = member[0] * gate[BUCKET_B[0]]
    for k in range(1, N_BUCKETS):
        if k > 1:
            bucket = bucket + float(k) * member[k]
        wa = wa + member[k] * gate[BUCKET_A[k]]
        wb = wb + member[k] * gate[BUCKET_B[k]]
    tm = bucket.shape[1]
    onehot = jnp.concatenate(member + [jnp.zeros((BUCKET_ROWS - N_BUCKETS, tm), F32)], axis=0)
    earlier = jnp.dot(onehot.astype(BF16), tri_ref[...], preferred_element_type=F32)
    carry = carry_ref[...]
    rank = jnp.sum(onehot * (earlier + carry[:, 0:1]), axis=0, keepdims=True)
    carry_ref[...] = carry + jnp.sum(onehot, axis=1, keepdims=True)
    return bucket.astype(I32), rank.astype(I32), wa, wb


SLAB = 32


def _row_slabs(n_rows, slab, body):
    def step(i, carry):
        body(pl.ds(pl.multiple_of(i * slab, slab), slab))
        return carry
    lax.fori_loop(0, n_rows // slab, step, 0)


def _residual_router(get_x, acc_ref, g1_ref, nf_ref, sc2_ref, sh2_ref, wr_ref, rb_ref, tri_ref,
                     x1_ref, rows_ref, meta_ref, count_ref, carry_ref, hhi_ref, hlo_ref):
    @pl.when(pl.program_id(0) == 0)
    def _():
        carry_ref[...] = jnp.zeros_like(carry_ref)

    g1 = g1_ref[...]
    gain = nf_ref[...] * (1.0 + sc2_ref[...])
    shift = sh2_ref[...]

    def slab(rows):
        x1 = get_x(rows) + g1 * acc_ref[rows, :]
        x1_ref[rows, :] = x1
        h2 = x1 * lax.rsqrt(jnp.mean(x1 * x1, axis=-1, keepdims=True) + EPS) * gain + shift
        hi = h2.astype(BF16)
        hi32 = hi.astype(F32)
        hhi_ref[rows, :] = hi
        hlo_ref[rows, :] = (h2 - hi32).astype(BF16)
        bits = pltpu.bitcast(hi32, U32)
        rows_ref[rows, :HALF] = (bits[:, HALF:] & jnp.uint32(0xFFFF0000)) | (bits[:, :HALF] >> 16)

    _row_slabs(acc_ref.shape[0], SLAB, slab)
    both = jnp.dot(hhi_ref[...], wr_ref[...], preferred_element_type=F32)
    logits = (both[:, :LANE] + both[:, LANE:]
              + jnp.dot(hlo_ref[...], wr_ref[:, :LANE], preferred_element_type=F32))
    bucket, rank, wa, wb = _route(logits, rb_ref, tri_ref, carry_ref)
    tm = logits.shape[0]
    gates_t = jnp.concatenate([wa, wb, jnp.zeros((LANE - 2, tm), F32)], axis=0)
    rows_ref[:, HALF:] = pltpu.bitcast(gates_t.T, U32)
    meta_ref[...] = jnp.concatenate([bucket, rank, jnp.zeros((6, tm), I32)], axis=0)
    count_ref[...] = carry_ref[...]


def _out0_kernel(attc_ref, attl_ref, mixc_ref, mixl_ref, wf_ref, woa_ref, wof_ref, xp_ref, xs_ref, g1_ref, nf_ref,
                 sc2_ref, sh2_ref, wr_ref, rb_ref, tri_ref, x1_ref, rows_ref, meta_ref, count_ref,
                 carry_ref, hhi_ref, hlo_ref, acc_ref):
    is_lat = pl.program_id(0) >= N_CTX_TILES
    att = jnp.where(is_lat, attl_ref[...], attc_ref[...])
    mix = jnp.where(is_lat, mixl_ref[...], mixc_ref[...])
    four = jnp.dot(mix.astype(BF16), wf_ref[...], preferred_element_type=F32)
    acc_ref[...] = (jnp.dot(att, woa_ref[...], preferred_element_type=F32)
                    + jnp.dot(four.astype(BF16), wof_ref[...], preferred_element_type=F32))

    def get_x(rows):
        return jnp.where(is_lat, xs_ref[rows, :], xp_ref[rows, :])

    _residual_router(get_x, acc_ref, g1_ref, nf_ref, sc2_ref, sh2_ref, wr_ref, rb_ref, tri_ref,
                     x1_ref, rows_ref, meta_ref, count_ref, carry_ref, hhi_ref, hlo_ref)


def _router_operands(w_router, router_bias):
    wr = jnp.zeros((D_MODEL, LANE), F32).at[:, :N_EXPERTS].set(w_router)
    wr_hi = wr.astype(BF16)
    wr_lo = (wr - wr_hi.astype(F32)).astype(BF16)
    rb = jnp.broadcast_to(router_bias.astype(F32)[:, None], (N_EXPERTS, TM))
    tri = jnp.asarray(np.triu(np.ones((TM, TM), np.float32), 1), BF16)
    return jnp.concatenate([wr_hi, wr_lo], axis=1), rb, tri


_EPILOGUE_OUT_SPECS = [_row_spec(D_MODEL), _row_spec(ROW_WORDS), pl.BlockSpec((8, TM), lambda i: (0, i)),
                       _const_spec((BUCKET_ROWS, LANE))]
_EPILOGUE_OUT_SHAPE = [jax.ShapeDtypeStruct((NT, D_MODEL), F32), jax.ShapeDtypeStruct((NT, ROW_WORDS), U32),
                       jax.ShapeDtypeStruct((8, NT), I32), jax.ShapeDtypeStruct((BUCKET_ROWS, LANE), F32)]
_EPILOGUE_SCRATCH = [pltpu.VMEM((BUCKET_ROWS, LANE), F32), pltpu.VMEM((TM, D_MODEL), BF16),
                     pltpu.VMEM((TM, D_MODEL), BF16), pltpu.VMEM((TM, D_MODEL), F32)]


def _epilogue_specs(layer):
    return [_mod_spec(layer, 2, TM), _const_spec((1, D_MODEL)), _mod_spec(layer, 4, TM),
            _mod_spec(layer, 3, TM), _const_spec((D_MODEL, 2 * LANE)), _const_spec((N_EXPERTS, TM)),
            _const_spec((TM, TM))]


def _out0(att, mixed, xp, xs, mods, w_fourier, w_out, norm_ffn0, router):
    wf = jnp.zeros((FOURIER_WIDTH, FOURIER_WIDTH), F32)
    for gi in range(N_FGROUPS):
        sl = slice(gi * FGROUP_DIM, (gi + 1) * FGROUP_DIM)
        wf = wf.at[sl, sl].set(w_fourier[gi])
    w_out = w_out.astype(BF16)
    wr, rb, tri = router
    return pl.pallas_call(
        _out0_kernel,
        grid=(NT // TM,),
        in_specs=_split_row_specs(ATT_WIDTH) + _split_row_specs(FOURIER_WIDTH)
        + [_const_spec((FOURIER_WIDTH, FOURIER_WIDTH)), _const_spec((ATT_WIDTH, D_MODEL)),
           _const_spec((FOURIER_WIDTH, D_MODEL))]
        + _split_row_specs(D_MODEL) + _epilogue_specs(0),
        out_specs=_EPILOGUE_OUT_SPECS,
        out_shape=_EPILOGUE_OUT_SHAPE,
        scratch_shapes=_EPILOGUE_SCRATCH,
        compiler_params=_cparams(("arbitrary",)),
        name="out_proj_att",
    )(att[0], att[1], mixed[0], mixed[1], wf.astype(BF16), w_out[:ATT_WIDTH], w_out[ATT_WIDTH:], xp, xs, mods,
      norm_ffn0.reshape(1, D_MODEL), mods, mods, wr, rb, tri)


def _dispatch_tables(meta, counts):
    bucket, rank = meta[0], meta[1]
    cnt = counts[:N_BUCKETS, 0].astype(I32)
    padded = (cnt + (TE - 1)) // TE * TE
    ends = jnp.cumsum(padded)
    starts = ends - padded
    kk = jnp.arange(N_BUCKETS, dtype=I32)
    pos = rank + jnp.sum(jnp.where(bucket[None, :] == kk[:, None], starts[:, None], 0), axis=0)
    tile0 = jnp.arange(N_ETILES, dtype=I32) * TE
    used = tile0 < ends[-1]
    tb = jnp.sum((tile0[:, None] >= ends[None, :]).astype(I32), axis=1)
    tb = jnp.where(used, tb, jnp.max(jnp.where(used, tb, 0)))
    pick = tb[:, None] == kk[None, :]

    def per_tile(table):
        return jnp.sum(jnp.where(pick, table[None, :], 0), axis=1)

    nrow = jnp.where(used, jnp.clip(per_tile(starts + cnt) - tile0, 0, TE), 0)
    pos3 = pos.reshape(SC_WORKERS, NT // (SC_WORKERS * SC_CHUNK), SC_CHUNK)
    return pos3, per_tile(jnp.asarray(BUCKET_A, I32)), per_tile(jnp.asarray(BUCKET_B, I32)), nrow


def _sc_permute(src, pos3, n_out, scatter):
    width = src.shape[1]
    _, n_chunks, chunk = pos3.shape
    rows_per_worker = n_chunks * chunk
    mesh = plsc.VectorSubcoreMesh(core_axis_name="c", subcore_axis_name="s")

    @functools.partial(pl.kernel, out_type=jax.ShapeDtypeStruct((n_out, width), src.dtype), mesh=mesh,
                       scratch_types=[pltpu.VMEM((n_chunks, chunk), I32), pltpu.VMEM((chunk, width), src.dtype)])
    def permute(src_hbm, pos_hbm, out_hbm, pos_v, buf):
        worker = lax.axis_index("s") * SC_CORES + lax.axis_index("c")
        base = worker * rows_per_worker
        pltpu.sync_copy(pos_hbm.at[worker], pos_v)
        for j in range(n_chunks):
            own = pl.ds(base + j * chunk, chunk)
            if scatter:
                pltpu.sync_copy(src_hbm.at[own], buf)
                pltpu.sync_copy(buf, out_hbm.at[pos_v.at[j]])
            else:
                pltpu.sync_copy(src_hbm.at[pos_v.at[j]], buf)
                pltpu.sync_copy(buf, out_hbm.at[own])

    return permute(src, pos3)


def _experts_kernel(ea_ref, eb_ref, nrow_ref, rows_ref, wga_ref, wua_ref, wda_ref, wgb_ref, wub_ref, wdb_ref, y_ref,
                    cga, cua, cda, cgb, cub, cdb):
    j = pl.program_id(0)
    n = nrow_ref[j]

    @pl.when(n == 0)
    def _():
        y_ref[...] = jnp.zeros_like(y_ref)

    @pl.when(n > 0)
    def _():
        prev = jnp.maximum(j - 1, 0)

        @pl.when(jnp.logical_or(j == 0, ea_ref[j] != ea_ref[prev]))
        def _():
            cga[...] = wga_ref[...].astype(BF16)
            cua[...] = wua_ref[...].astype(BF16)
            cda[...] = wda_ref[...].astype(BF16)

        @pl.when(jnp.logical_or(j == 0, eb_ref[j] != eb_ref[prev]))
        def _():
            cgb[...] = wgb_ref[...].astype(BF16)
            cub[...] = wub_ref[...].astype(BF16)
            cdb[...] = wdb_ref[...].astype(BF16)

        valid = lax.broadcasted_iota(I32, (TE, 1), 0) < n
        h = jnp.where(valid, _unpack_bf16_pairs(rows_ref[:, :HALF]), 0.0).astype(BF16)
        gates = jnp.where(valid, pltpu.bitcast(rows_ref[:, HALF:], F32), 0.0)
        y = None
        for cg, cu, cd, col in ((cga, cua, cda, 0), (cgb, cub, cdb, 1)):
            a = jnp.dot(h, cg[...], preferred_element_type=F32)
            u = jnp.dot(h, cu[...], preferred_element_type=F32)
            hid = _silu(a) * u * gates[:, col:col + 1]
            o = jnp.dot(hid.astype(BF16), cd[...], preferred_element_type=F32)
            y = o if y is None else y + o
        y_ref[...] = _pack_bf16_pairs(y)


def _experts(rows_sorted, ea, eb, nrow, layer, w_gate, w_up, w_down):
    def w_spec(shape, which):
        return pl.BlockSpec((None, None) + shape, lambda j, ea, eb, nr: (layer, (ea, eb)[which][j], 0, 0))

    up_shape, down_shape = (D_MODEL, EXPERT_FF), (EXPERT_FF, D_MODEL)
    grid_spec = pltpu.PrefetchScalarGridSpec(
        num_scalar_prefetch=3,
        grid=(N_ETILES,),
        in_specs=[pl.BlockSpec((TE, ROW_WORDS), lambda j, *_: (j, 0)),
                  w_spec(up_shape, 0), w_spec(up_shape, 0), w_spec(down_shape, 0),
                  w_spec(up_shape, 1), w_spec(up_shape, 1), w_spec(down_shape, 1)],
        out_specs=pl.BlockSpec((TE, HALF), lambda j, *_: (j, 0)),
        scratch_shapes=[pltpu.VMEM(up_shape, BF16), pltpu.VMEM(up_shape, BF16), pltpu.VMEM(down_shape, BF16),
                        pltpu.VMEM(up_shape, BF16), pltpu.VMEM(up_shape, BF16), pltpu.VMEM(down_shape, BF16)])
    return pl.pallas_call(
        _experts_kernel,
        grid_spec=grid_spec,
        out_shape=jax.ShapeDtypeStruct((P_MAX, HALF), U32),
        compiler_params=_cparams(("arbitrary",)),
        name="experts_layer%d" % layer,
    )(ea, eb, nrow, rows_sorted, w_gate, w_up, w_down, w_gate, w_up, w_down)


def _moe(rows, meta, counts, layer, w_gate, w_up, w_down):
    pos3, ea, eb, nrow = _dispatch_tables(meta, counts)
    rows_sorted = _sc_permute(rows, pos3, P_MAX, scatter=True)
    y_sorted = _experts(rows_sorted, ea, eb, nrow, layer, w_gate, w_up, w_down)
    return _sc_permute(y_sorted, pos3, NT, scatter=False)


def _final_kernel(x_ref, y_ref, g2_ref, nfin_ref, yp_ref, ys_ref):
    i = pl.program_id(0)
    x = x_ref[...] + g2_ref[...] * _unpack_bf16_pairs(y_ref[...])
    out = x * lax.rsqrt(jnp.mean(x * x, axis=-1, keepdims=True) + EPS) * nfin_ref[...]

    @pl.when(i < N_CTX_TILES)
    def _():
        yp_ref[...] = out

    @pl.when(i >= N_CTX_TILES)
    def _():
        ys_ref[...] = out


def _final(x1, y_tok, mods, norm_final):
    return pl.pallas_call(
        _final_kernel,
        grid=(NT // TM,),
        in_specs=[_row_spec(D_MODEL), _row_spec(HALF), _mod_spec(DEPTH - 1, 5, TM), _const_spec((1, D_MODEL))],
        out_specs=_split_row_specs(D_MODEL),
        out_shape=[jax.ShapeDtypeStruct((N_CTX, D_MODEL), F32), jax.ShapeDtypeStruct((N_LAT, D_MODEL), F32)],
        compiler_params=_cparams(("arbitrary",)),
        name="final_norm",
    )(x1, y_tok, mods, norm_final.reshape(1, D_MODEL))


def _in1_kernel(x_ref, y_ref, g2_ref, nm_ref, sc_ref, sh_ref, w_ref, x2_ref, z_ref, xbc_ref, dt_ref, h_ref):
    g2 = g2_ref[...]
    gain = nm_ref[...] * (1.0 + sc_ref[...])
    shift = sh_ref[...]

    def norm_rows(rows):
        x = x_ref[rows, :] + g2 * _unpack_bf16_pairs(y_ref[rows, :])
        x2_ref[rows, :] = x
        h_ref[rows, :] = (x * lax.rsqrt(jnp.mean(x * x, axis=-1, keepdims=True) + EPS) * gain + shift).astype(BF16)

    _row_slabs(TM, SLAB, norm_rows)
    h = h_ref[...]
    step = 512
    for c0 in range(0, SSM_INNER, step):
        z_ref[:, c0:c0 + step] = jnp.dot(h, w_ref[:, c0:c0 + step], preferred_element_type=F32).astype(BF16)
    for c0 in range(0, SSM_CONV_CH, step):
        xbc_ref[:, c0:c0 + step] = jnp.dot(h, w_ref[:, SSM_INNER + c0:SSM_INNER + c0 + step],
                                            preferred_element_type=F32).astype(BF16)
    dt_ref[...] = jnp.dot(h, w_ref[:, SSM_INNER + SSM_CONV_CH:], preferred_element_type=F32)


def _in1(x1, y_tok, mods, norm_mix1, w_in):
    w = jnp.concatenate([w_in, jnp.zeros((D_MODEL, ODD_IN_PAD - ODD_IN), F32)], axis=1).astype(BF16)
    return pl.pallas_call(
        _in1_kernel,
        grid=(NT // TM,),
        in_specs=[_row_spec(D_MODEL), _row_spec(HALF), _mod_spec(0, 5, TM), _const_spec((1, D_MODEL)),
                  _mod_spec(1, 1, TM), _mod_spec(1, 0, TM), _const_spec((D_MODEL, ODD_IN_PAD))],
        out_specs=[_row_spec(D_MODEL), _row_spec(SSM_INNER), _row_spec(SSM_CONV_CH), _row_spec(LANE)],
        out_shape=[jax.ShapeDtypeStruct((NT, D_MODEL), F32), jax.ShapeDtypeStruct((NT, SSM_INNER), BF16),
                   jax.ShapeDtypeStruct((NT, SSM_CONV_CH), BF16), jax.ShapeDtypeStruct((NT, LANE), F32)],
        scratch_shapes=[pltpu.VMEM((TM, D_MODEL), BF16)],
        compiler_params=_cparams(("arbitrary",)),
        name="in_proj_ssm",
    )(x1, y_tok, mods, norm_mix1.reshape(1, D_MODEL), mods, mods, w)


TCV = 256


def _conv_kernel(x_ref, prev_ref, next_ref, w_ref, b_ref, o_ref):
    i = pl.program_id(0)
    nct = N_CTX // TCV
    j = (i - nct) % (DEC_SEQ // TCV)
    first = jnp.logical_or(i < nct, j == 0)
    last = jnp.logical_or(i < nct, j == DEC_SEQ // TCV - 1)
    row = lax.broadcasted_iota(jnp.int32, (TCV, LANE), 0)
    top = row == 0
    bottom = row == TCV - 1
    for c in range(SSM_CONV_CH // LANE):
        cols = slice(c * LANE, (c + 1) * LANE)
        x = x_ref[:, cols].astype(F32)
        prev_row = jnp.where(first, 0.0, prev_ref[7:8, cols].astype(F32))
        next_row = jnp.where(last, 0.0, next_ref[0:1, cols].astype(F32))
        xm1 = jnp.where(top, prev_row, pltpu.roll(x, 1, 0))
        xp1 = jnp.where(bottom, next_row, pltpu.roll(x, TCV - 1, 0))
        y = xm1 * w_ref[0:1, cols] + x * w_ref[1:2, cols] + xp1 * w_ref[2:3, cols] + b_ref[:, cols]
        o_ref[:, cols] = _silu(y).astype(BF16)


def _conv(xbc, conv_w, conv_b):
    r8 = TCV // 8
    nblk8 = NT // 8
    return pl.pallas_call(
        _conv_kernel,
        grid=(NT // TCV,),
        in_specs=[_row_spec(SSM_CONV_CH, TCV),
                  pl.BlockSpec((8, SSM_CONV_CH), lambda i: (jnp.maximum(i * r8 - 1, 0), 0)),
                  pl.BlockSpec((8, SSM_CONV_CH), lambda i: (jnp.minimum((i + 1) * r8, nblk8 - 1), 0)),
                  _const_spec((3, SSM_CONV_CH)), _const_spec((1, SSM_CONV_CH))],
        out_specs=_row_spec(SSM_CONV_CH, TCV),
        out_shape=jax.ShapeDtypeStruct((NT, SSM_CONV_CH), BF16),
        compiler_params=_cparams(("arbitrary",)),
        name="ssm_conv",
    )(xbc, xbc, xbc, conv_w, conv_b.reshape(1, SSM_CONV_CH))


HPG = SSM_HEADS // SSM_GROUPS
GW = HPG * SSM_HEAD_DIM


def _ssd_kernel(*refs, reverse, has_init, write_state):
    refs = list(refs)
    xc_ref, dt_ref, bias_ref, alog_ref = refs[:4]
    refs = refs[4:]
    h0_ref = refs.pop(0) if has_init else None
    y_ref = refs.pop(0)
    hout_ref = refs.pop(0) if write_state else None
    (h_ref,) = refs
    c = pl.program_id(1)
    nc = pl.num_programs(1)
    col0 = SSM_HEADS if reverse else 0
    Q = SSM_CHUNK

    @pl.when(c == 0)
    def _():
        for g in range(SSM_GROUPS):
            if has_init:
                h_ref[g] = h0_ref[g * GW:(g + 1) * GW, :].T
            else:
                h_ref[g] = jnp.zeros((SSM_STATE, GW), F32)

    dt = jax.nn.softplus(dt_ref[...] + bias_ref[...])
    a = dt * -jnp.exp(alog_ref[...])
    row = lax.broadcasted_iota(jnp.int32, (Q, Q), 0)
    col = lax.broadcasted_iota(jnp.int32, (Q, Q), 1)
    keep = (col >= row) if reverse else (col <= row)
    tri = keep.astype(F32)
    acs = jnp.dot(tri, a, precision=HIGHEST, preferred_element_type=F32)
    edge = (0 if reverse else Q - 1)
    acs_end = acs[edge:edge + 1, :]
    log2e = math.log2(math.e)
    acs2 = acs * log2e
    src_t = (acs2 - jnp.log2(dt)).T
    out_t = (jnp.exp(acs_end - acs) * dt).T
    chunk_decay = jnp.exp(jnp.broadcast_to(acs_end, (8, LANE)))
    nt_dims = (((1,), (1,)), ((), ()))
    lane = lax.broadcasted_iota(jnp.int32, (Q, LANE), 1)
    low = lane < SSM_HEAD_DIM
    low8 = low[:8]

    def two_heads(v):
        zero = jnp.zeros_like(v)
        return jnp.concatenate([jnp.where(low, v, zero), jnp.where(low, zero, v)], axis=0)

    for g in range(SSM_GROUPS):
        bg = xc_ref[:, SSM_INNER + g * SSM_STATE:SSM_INNER + (g + 1) * SSM_STATE]
        cg = xc_ref[:, SSM_INNER + SSM_GROUPS * SSM_STATE + g * SSM_STATE:
                    SSM_INNER + SSM_GROUPS * SSM_STATE + (g + 1) * SSM_STATE]
        cb = lax.dot_general(cg, bg, nt_dims, preferred_element_type=F32).astype(BF16)
        bg_t = bg.astype(F32).T
        for j in range(HPG // 2):
            pair = slice(g * GW + j * LANE, g * GW + (j + 1) * LANE)
            mats, c_in, b_out, cdec = [], [], [], []
            for hh in (2 * j, 2 * j + 1):
                cidx = col0 + g * HPG + hh
                to_l = jnp.broadcast_to(acs2[:, cidx:cidx + 1], (Q, Q))
                lmat = jnp.where(keep, jnp.exp2(to_l - src_t[cidx:cidx + 1, :]), 0.0)
                mats.append(cb * lmat.astype(BF16))
                c_in.append(cg * jnp.exp2(to_l).astype(BF16))
                b_out.append((bg_t * out_t[cidx:cidx + 1, :]).astype(BF16))
                cdec.append(jnp.broadcast_to(chunk_decay[:, cidx:cidx + 1], (8, LANE)))
            hp = h_ref[g, :, j * LANE:(j + 1) * LANE]
            x2 = two_heads(xc_ref[:, pair])
            rhs = jnp.concatenate([x2, two_heads(hp.astype(BF16))], axis=0)
            y = jnp.dot(jnp.concatenate(mats + c_in, axis=1), rhs, preferred_element_type=F32)
            y_ref[:, pair] = y.astype(BF16)
            st = jnp.dot(jnp.concatenate(b_out, axis=1), x2, preferred_element_type=F32)
            h_ref[g, :, j * LANE:(j + 1) * LANE] = hp * jnp.where(low8, cdec[0], cdec[1])[0:1, :] + st

    if write_state:
        @pl.when(c == nc - 1)
        def _():
            for g in range(SSM_GROUPS):
                hout_ref[g * GW:(g + 1) * GW, :] = h_ref[g].T


def _ssd_direction(xc, dt_raw, bias128, alog128, h0, reverse):
    outs = []
    for (row0, nseq, seqlen, has_init, write_state) in ((0, BATCH, SEQ, False, True),
                                                         (N_CTX, DEC_BATCH, DEC_SEQ, True, False)):
        nc = seqlen // SSM_CHUNK
        base = row0 // SSM_CHUNK

        def chunk_idx(s, c, nc=nc, base=base):
            cc = (nc - 1 - c) if reverse else c
            return (base + s * nc + cc, 0)

        def out_idx(s, c, nc=nc):
            cc = (nc - 1 - c) if reverse else c
            return (s * nc + cc, 0)

        state_spec = pl.BlockSpec((None, SSM_INNER, SSM_STATE), lambda s, c: (s, 0, 0))
        in_specs = [pl.BlockSpec((SSM_CHUNK, SSM_CONV_CH), chunk_idx), pl.BlockSpec((SSM_CHUNK, LANE), chunk_idx),
                    _const_spec((1, LANE)), _const_spec((1, LANE))]
        args = [xc, dt_raw, bias128, alog128]
        if has_init:
            in_specs.append(state_spec)
            args.append(h0)
        out_specs = [pl.BlockSpec((SSM_CHUNK, SSM_INNER), out_idx)]
        out_shape = [jax.ShapeDtypeStruct((nseq * seqlen, SSM_INNER), BF16)]
        if write_state:
            out_specs.append(state_spec)
            out_shape.append(jax.ShapeDtypeStruct((nseq, SSM_INNER, SSM_STATE), F32))
        outs.append(pl.pallas_call(
            functools.partial(_ssd_kernel, reverse=reverse, has_init=has_init, write_state=write_state),
            grid=(nseq, nc),
            in_specs=in_specs,
            out_specs=out_specs,
            out_shape=out_shape,
            scratch_shapes=[pltpu.VMEM((SSM_GROUPS, SSM_STATE, GW), F32)],
            compiler_params=_cparams(("arbitrary", "arbitrary")),
            name="ssd_%s_%s" % ("bwd" if reverse else "fwd", "lat" if has_init else "ctx"),
        )(*args))
    return (outs[0][0], outs[1][0]), outs[0][1]


def _out1_kernel(yfc_ref, yfl_ref, ybc_ref, ybl_ref, xs_ref, z_ref, dskip_ref, ng_ref, wo_ref, x_ref, g1_ref, nf_ref,
                 sc2_ref, sh2_ref, wr_ref, rb_ref, tri_ref, x1_ref, rows_ref, meta_ref, count_ref,
                 carry_ref, hhi_ref, hlo_ref, acc_ref, y_ref):
    is_lat = pl.program_id(0) >= N_CTX_TILES
    dskip = dskip_ref[...]
    gain = ng_ref[...]

    def gate_norm(rows):
        yf = jnp.where(is_lat, yfl_ref[rows, :], yfc_ref[rows, :])
        yb = jnp.where(is_lat, ybl_ref[rows, :], ybc_ref[rows, :])
        y = yf.astype(F32) + yb.astype(F32) + dskip * xs_ref[rows, :].astype(F32)
        y = y * _silu(z_ref[rows, :].astype(F32))
        y_ref[rows, :] = (y * lax.rsqrt(jnp.mean(y * y, axis=-1, keepdims=True) + EPS) * gain).astype(BF16)

    _row_slabs(TM, SLAB // 2, gate_norm)
    acc_ref[...] = jnp.dot(y_ref[...], wo_ref[...], preferred_element_type=F32)
    _residual_router(lambda rows: x_ref[rows, :], acc_ref, g1_ref, nf_ref, sc2_ref, sh2_ref, wr_ref, rb_ref,
                     tri_ref, x1_ref, rows_ref, meta_ref, count_ref, carry_ref, hhi_ref, hlo_ref)


def _out1(yf, yb, xc, z, d_skip, ssm_norm, w_out, x, mods, norm_ffn1, router):
    wr, rb, tri = router
    return pl.pallas_call(
        _out1_kernel,
        grid=(NT // TM,),
        in_specs=_split_row_specs(SSM_INNER) + _split_row_specs(SSM_INNER)
        + [_row_spec(SSM_INNER), _row_spec(SSM_INNER),
           _const_spec((1, SSM_INNER)), _const_spec((1, SSM_INNER)), _const_spec((SSM_INNER, D_MODEL)),
           _row_spec(D_MODEL)] + _epilogue_specs(1),
        out_specs=_EPILOGUE_OUT_SPECS,
        out_shape=_EPILOGUE_OUT_SHAPE,
        scratch_shapes=_EPILOGUE_SCRATCH + [pltpu.VMEM((TM, SSM_INNER), BF16)],
        compiler_params=_cparams(("arbitrary",)),
        name="out_proj_ssm",
    )(yf[0], yf[1], yb[0], yb[1], xc, z, jnp.repeat(d_skip.astype(F32), SSM_HEAD_DIM).reshape(1, SSM_INNER),
      ssm_norm.reshape(1, SSM_INNER), w_out.astype(BF16), x, mods, norm_ffn1.reshape(1, D_MODEL), mods, mods,
      wr, rb, tri)


def kernel(x_prompt, x_sample, cache_k, cache_v, state_fwd, state_bwd, c, c_ctx, w_ada, b_ada, norm_mix, norm_ffn,
           w_in_att, q_gain, k_gain, w_fourier, w_out_att, w_in_ssm, conv_w, conv_b, dt_bias_f, dt_bias_b, a_log_f,
           a_log_b, d_skip, ssm_norm, w_out_ssm, w_router, router_bias, w_gate, w_up, w_down, norm_final):
    xp = x_prompt.reshape(N_CTX, D_MODEL)
    xs = x_sample.reshape(N_LAT, D_MODEL)
    cond8 = jnp.zeros((8, D_MODEL), F32).at[0].set(c_ctx).at[1:1 + DEC_BATCH].set(c)
    mods = _ada_mods(cond8, w_ada, b_ada)
    router = _router_operands(w_router, router_bias)
    cos2, sin2 = _rope_tables()

    q, kp, vp, u, new_k, new_v = _in0(xp, xs, mods, norm_mix[0], w_in_att[0], q_gain[0], k_gain[0], cos2, sin2)
    att = _attention(q, kp, vp, cache_k[:, 0], cache_v[:, 0])
    mixed = _fourier(u)
    x1, rows, meta, counts = _out0(att, mixed, xp, xs, mods, w_fourier[0], w_out_att[0], norm_ffn[0], router)
    y_tok = _moe(rows, meta, counts, 0, w_gate, w_up, w_down)

    x2, z, xbc, dt_raw = _in1(x1, y_tok, mods, norm_mix[1], w_in_ssm[0])
    xc = _conv(xbc, conv_w[0], conv_b[0])
    pad = jnp.zeros((LANE - 2 * SSM_HEADS,), F32)
    bias128 = jnp.concatenate([dt_bias_f[0], dt_bias_b[0], pad]).astype(F32).reshape(1, LANE)
    alog128 = jnp.concatenate([a_log_f[0], a_log_b[0], pad]).astype(F32).reshape(1, LANE)
    yf, sf = _ssd_direction(xc, dt_raw, bias128, alog128, state_fwd.reshape(DEC_BATCH, SSM_INNER, SSM_STATE), False)
    yb, sb = _ssd_direction(xc, dt_raw, bias128, alog128, state_bwd.reshape(DEC_BATCH, SSM_INNER, SSM_STATE), True)
    x3, rows, meta, counts = _out1(yf, yb, xc, z, d_skip[0], ssm_norm[0], w_out_ssm[0], x2, mods, norm_ffn[1], router)
    y_tok = _moe(rows, meta, counts, 1, w_gate, w_up, w_down)
    y_prompt, y_sample = _final(x3, y_tok, mods, norm_final)

    state_shape = (BATCH, 1, SSM_HEADS, SSM_HEAD_DIM, SSM_STATE)
    return (y_prompt.reshape(BATCH, SEQ, D_MODEL), y_sample.reshape(DEC_BATCH, DEC_SEQ, D_MODEL),
            new_k.reshape(BATCH, 1, SEQ, N_KV_HEADS, HEAD_DIM), new_v.reshape(BATCH, 1, SEQ, N_KV_HEADS, HEAD_DIM),
            sf.reshape(state_shape), sb.reshape(state_shape))
```

```python
import functools
import math

import numpy as np
import jax
import jax.numpy as jnp
from jax import lax
from jax.experimental import pallas as pl
from jax.experimental.pallas import tpu as pltpu
from jax.experimental.pallas import tpu_sc as plsc

F32 = jnp.float32
BF16 = jnp.bfloat16
U32 = jnp.uint32
I32 = jnp.int32
HIGHEST = lax.Precision.HIGHEST

D_MODEL = 1024
BATCH = 32
SEQ = 256
DEPTH = 2
DEC_BATCH = 2
DEC_SEQ = 4096
PAST_LEN = 256
GRID_W = 64
EPS = 1e-6
N_HEADS = 8
N_KV_HEADS = 2
HEAD_DIM = 64
ATT_WIDTH = N_HEADS * HEAD_DIM
KV_WIDTH = N_KV_HEADS * HEAD_DIM
ROPE_THETA = 10000.0
N_FGROUPS = 8
FGROUP_DIM = 64
FOURIER_WIDTH = N_FGROUPS * FGROUP_DIM
EVEN_IN = ATT_WIDTH + 2 * KV_WIDTH + FOURIER_WIDTH
SSM_INNER = 2 * D_MODEL
SSM_HEAD_DIM = 64
SSM_HEADS = SSM_INNER // SSM_HEAD_DIM
SSM_GROUPS = 4
SSM_STATE = 128
SSM_CHUNK = 128
SSM_CONV_CH = SSM_INNER + 2 * SSM_GROUPS * SSM_STATE
ODD_IN = SSM_INNER + SSM_CONV_CH + 2 * SSM_HEADS
ODD_IN_PAD = ODD_IN + 64
N_EXPERTS = 16
EXPERTS_PER_GROUP = 4
N_EXPERT_GROUPS = 4
EXPERT_FF = 512

N_CTX = BATCH * SEQ
N_LAT = DEC_BATCH * DEC_SEQ
NT = N_CTX + N_LAT
N_SEG = 1 + DEC_BATCH
LANE = 128
VMEM_LIMIT = 56 * 1024 * 1024

TM = 512
N_CTX_TILES = N_CTX // TM
LAT_TILES_PER_SEQ = DEC_SEQ // TM

PAIR_ORDER = ((0, 1), (0, 2), (0, 3), (1, 3), (1, 2), (3, 2))
N_BUCKETS = N_EXPERT_GROUPS * len(PAIR_ORDER)
BUCKET_A = tuple(g * EXPERTS_PER_GROUP + a for g in range(N_EXPERT_GROUPS) for a, _ in PAIR_ORDER)
BUCKET_B = tuple(g * EXPERTS_PER_GROUP + b for g in range(N_EXPERT_GROUPS) for _, b in PAIR_ORDER)
BUCKET_ROWS = 32
TE = 256
N_ETILES = (NT + N_BUCKETS * (TE - 1) + TE - 1) // TE
P_MAX = N_ETILES * TE
HALF = D_MODEL // 2
ROW_WORDS = HALF + LANE
SC_CORES = 2
SC_SUBCORES = 16
SC_WORKERS = SC_CORES * SC_SUBCORES
SC_CHUNK = 64


def _cparams(sem):
    return pltpu.CompilerParams(dimension_semantics=sem, vmem_limit_bytes=VMEM_LIMIT)


def _seg_of_tile(i, tm):
    nct = N_CTX // tm
    return jnp.where(i < nct, 0, 1 + (i - nct) // (DEC_SEQ // tm))


def _mod_spec(layer, which, tm):
    return pl.BlockSpec((None, None, None, 1, D_MODEL),
                        lambda i, *_: (layer, which, _seg_of_tile(i, tm), 0, 0))


def _row_spec(width, tm=TM):
    return pl.BlockSpec((tm, width), lambda i, *_: (i, 0))


def _const_spec(shape):
    nd = len(shape)
    return pl.BlockSpec(shape, lambda *_: (0,) * nd)


def _silu(x):
    return x * jax.nn.sigmoid(x)


def _ada_kernel(cond_ref, w_ref, b_ref, o_ref):
    c = cond_ref[...]
    o_ref[...] = jnp.dot(_silu(c), w_ref[...], precision=HIGHEST, preferred_element_type=F32) + b_ref[...]


def _ada_mods(cond8, w_ada, b_ada):
    tn = 1536
    out = pl.pallas_call(
        _ada_kernel,
        grid=(DEPTH, 6 * D_MODEL // tn),
        in_specs=[pl.BlockSpec((8, D_MODEL), lambda l, n: (0, 0)),
                  pl.BlockSpec((None, D_MODEL, tn), lambda l, n: (l, 0, n)),
                  pl.BlockSpec((None, 1, tn), lambda l, n: (l, 0, n))],
        out_specs=pl.BlockSpec((None, 8, tn), lambda l, n: (l, 0, n)),
        out_shape=jax.ShapeDtypeStruct((DEPTH, 8, 6 * D_MODEL), F32),
        compiler_params=_cparams(("arbitrary", "arbitrary")),
        name="ada_mod",
    )(cond8, w_ada, b_ada.reshape(DEPTH, 1, 6 * D_MODEL))
    return out.reshape(DEPTH, 8, 6, D_MODEL)[:, :N_SEG].transpose(0, 2, 1, 3)[:, :, :, None, :]


def _rope_tables():
    t = np.arange(DEC_SEQ)
    row = (t // GRID_W).astype(np.float64)
    col = (t % GRID_W).astype(np.float64)
    axis_dim = HEAD_DIM // 2
    freqs = ROPE_THETA ** (-np.arange(0, axis_dim, 2, dtype=np.float64) / axis_dim)
    ang = np.concatenate([row[:, None] * freqs, col[:, None] * freqs], axis=-1)
    cos = np.repeat(np.cos(ang), 2, axis=1)
    sin = np.repeat(np.sin(ang), 2, axis=1)
    sign = np.where(np.arange(HEAD_DIM) % 2 == 0, -1.0, 1.0)
    cos2 = np.tile(cos, (1, 2)).astype(np.float32)
    sin2 = np.tile(sin * sign, (1, 2)).astype(np.float32)
    return jnp.asarray(cos2), jnp.asarray(sin2)


def _dft_cos_sin(n, scale):
    k = np.arange(n)
    ang = 2.0 * np.pi * ((k[:, None] * k[None, :]) % n) / n
    return np.cos(ang) * scale, np.sin(ang) * scale


def _block_diag(m, reps):
    n = m.shape[0]
    out = np.zeros((n * reps, n * reps), m.dtype)
    for r in range(reps):
        out[r * n:(r + 1) * n, r * n:(r + 1) * n] = m
    return out


def _channel_dft():
    c, s = _dft_cos_sin(FGROUP_DIM, FGROUP_DIM ** -0.5)
    return jnp.asarray(np.concatenate([_block_diag(c, N_FGROUPS), _block_diag(s, N_FGROUPS)], axis=1), BF16)


def _group_ones(width):
    return jnp.asarray(_block_diag(np.ones((HEAD_DIM, HEAD_DIM), np.float32), width // HEAD_DIM), BF16)


def _pad_heads(x):
    lane = lax.broadcasted_iota(jnp.int32, x.shape, 1)
    low = lane < HEAD_DIM
    xr = pltpu.roll(x, HEAD_DIM, 1)
    zero = jnp.zeros_like(x)
    return [jnp.where(low, x, zero), jnp.where(low, zero, xr), jnp.where(low, xr, zero), jnp.where(low, zero, x)]


def _ctx_tile(i, *_):
    return (jnp.minimum(i, N_CTX_TILES - 1), 0)


def _lat_tile(i, *_):
    return (jnp.maximum(i - N_CTX_TILES, 0), 0)


def _split_row_specs(width):
    return [pl.BlockSpec((TM, width), _ctx_tile), pl.BlockSpec((TM, width), _lat_tile)]


def _in0_kernel(xp_ref, xs_ref, nm_ref, sc_ref, sh_ref, w_ref, qg_ref, kg_ref, ones_ref, cos_ref, sin_ref, dft_ref,
                q_ref, kp_ref, vp_ref, u_ref, nk_ref, nv_ref, h_ref):
    i = pl.program_id(0)
    is_lat = i >= N_CTX_TILES
    gain = nm_ref[...] * (1.0 + sc_ref[...])
    shift = sh_ref[...]

    def norm_rows(rows):
        x = jnp.where(is_lat, xs_ref[rows, :], xp_ref[rows, :])
        h_ref[rows, :] = (x * lax.rsqrt(jnp.mean(x * x, axis=-1, keepdims=True) + EPS) * gain + shift).astype(BF16)

    _row_slabs(TM, SLAB, norm_rows)
    p = jnp.dot(h_ref[...], w_ref[...], preferred_element_type=F32)
    q = p[:, :ATT_WIDTH]
    k = p[:, ATT_WIDTH:ATT_WIDTH + KV_WIDTH]
    v = p[:, ATT_WIDTH + KV_WIDTH:ATT_WIDTH + 2 * KV_WIDTH]
    f = p[:, ATT_WIDTH + 2 * KV_WIDTH:]
    ones = ones_ref[...]
    qss = jnp.dot((q * q).astype(BF16), ones, preferred_element_type=F32)
    kss = jnp.dot((k * k).astype(BF16), ones[:KV_WIDTH, :KV_WIDTH], preferred_element_type=F32)
    qn = q * lax.rsqrt(qss * (1.0 / HEAD_DIM) + EPS) * qg_ref[...]
    kn = k * lax.rsqrt(kss * (1.0 / HEAD_DIM) + EPS) * kg_ref[...]

    cos = jnp.where(is_lat, cos_ref[...], 1.0)
    sin = jnp.where(is_lat, sin_ref[...], 0.0)
    lane = lax.broadcasted_iota(jnp.int32, (TM, LANE), 1)
    even = (lane % 2) == 0

    def rope(xc):
        swapped = jnp.where(even, pltpu.roll(xc, LANE - 1, 1), pltpu.roll(xc, 1, 1))
        return xc * cos + swapped * sin

    scale = HEAD_DIM ** -0.5 * math.log2(math.e)
    for j in range(ATT_WIDTH // LANE):
        q_ref[:, j * LANE:(j + 1) * LANE] = (rope(qn[:, j * LANE:(j + 1) * LANE]) * scale).astype(BF16)
    for j, c in enumerate(_pad_heads(rope(kn))):
        kp_ref[:, j * LANE:(j + 1) * LANE] = c.astype(BF16)
    for j, c in enumerate(_pad_heads(v)):
        vp_ref[:, j * LANE:(j + 1) * LANE] = c.astype(BF16)
    u_ref[...] = jnp.dot(f.astype(BF16), dft_ref[...], preferred_element_type=F32).astype(BF16)

    @pl.when(jnp.logical_not(is_lat))
    def _():
        nk_ref[...] = kn
        nv_ref[...] = v


def _in0(xp, xs, mods, norm_mix0, w_in, q_gain, k_gain, cos2, sin2):
    def table_idx(i):
        return (jnp.where(i < N_CTX_TILES, 0, (i - N_CTX_TILES) % LAT_TILES_PER_SEQ), 0)

    ctx_idx = _ctx_tile
    outs = pl.pallas_call(
        _in0_kernel,
        grid=(NT // TM,),
        in_specs=_split_row_specs(D_MODEL) + [_const_spec((1, D_MODEL)), _mod_spec(0, 1, TM), _mod_spec(0, 0, TM),
                  _const_spec((D_MODEL, EVEN_IN)), _const_spec((1, ATT_WIDTH)), _const_spec((1, KV_WIDTH)),
                  _const_spec((ATT_WIDTH, ATT_WIDTH)),
                  pl.BlockSpec((TM, LANE), table_idx), pl.BlockSpec((TM, LANE), table_idx),
                  _const_spec((FOURIER_WIDTH, 2 * FOURIER_WIDTH))],
        out_specs=[_row_spec(ATT_WIDTH), _row_spec(4 * LANE), _row_spec(4 * LANE), _row_spec(2 * FOURIER_WIDTH),
                   pl.BlockSpec((TM, KV_WIDTH), ctx_idx), pl.BlockSpec((TM, KV_WIDTH), ctx_idx)],
        out_shape=[jax.ShapeDtypeStruct((NT, ATT_WIDTH), BF16), jax.ShapeDtypeStruct((NT, 4 * LANE), BF16),
                   jax.ShapeDtypeStruct((NT, 4 * LANE), BF16), jax.ShapeDtypeStruct((NT, 2 * FOURIER_WIDTH), BF16),
                   jax.ShapeDtypeStruct((N_CTX, KV_WIDTH), F32), jax.ShapeDtypeStruct((N_CTX, KV_WIDTH), F32)],
        scratch_shapes=[pltpu.VMEM((TM, D_MODEL), BF16)],
        compiler_params=_cparams(("arbitrary",)),
        name="in_proj_att",
    )(xp, xs, norm_mix0.reshape(1, D_MODEL), mods, mods, w_in.astype(BF16),
      jnp.tile(q_gain, N_HEADS).reshape(1, ATT_WIDTH), jnp.tile(k_gain, N_KV_HEADS).reshape(1, KV_WIDTH),
      _group_ones(ATT_WIDTH), cos2, sin2, _channel_dft())
    return outs


def _att_kernel(*refs, has_cache):
    if has_cache:
        q_ref, kp_ref, vp_ref, ck_ref, cv_ref, o_ref = refs
        ckp = [c.astype(BF16) for c in _pad_heads(ck_ref[...])]
        cvp = [c.astype(BF16) for c in _pad_heads(cv_ref[...])]
    else:
        q_ref, kp_ref, vp_ref, o_ref = refs
    nt_dims = (((1,), (1,)), ((), ()))
    for j in range(ATT_WIDTH // LANE):
        qj = q_ref[:, j * LANE:(j + 1) * LANE]
        g = j // 2
        acc = None
        for half in range(2):
            c = 2 * g + half
            kk = kp_ref[:, c * LANE:(c + 1) * LANE]
            vv = vp_ref[:, c * LANE:(c + 1) * LANE]
            s = lax.dot_general(qj, kk, nt_dims, preferred_element_type=F32)
            m = jnp.max(s, axis=-1, keepdims=True)
            if has_cache:
                sc = lax.dot_general(qj, ckp[c], nt_dims, preferred_element_type=F32)
                m = jnp.maximum(m, jnp.max(sc, axis=-1, keepdims=True))
            p = jnp.exp2(s - m)
            d = jnp.sum(p, axis=-1, keepdims=True)
            o = jnp.dot(p.astype(BF16), vv, preferred_element_type=F32)
            if has_cache:
                pc = jnp.exp2(sc - m)
                d = d + jnp.sum(pc, axis=-1, keepdims=True)
                o = o + jnp.dot(pc.astype(BF16), cvp[c], preferred_element_type=F32)
            o = o * (1.0 / d)
            acc = o if acc is None else acc + o
        o_ref[:, j * LANE:(j + 1) * LANE] = acc.astype(BF16)


def _attention(q, kp, vp, cache_k, cache_v):
    att_ctx = pl.pallas_call(
        functools.partial(_att_kernel, has_cache=False),
        grid=(BATCH,),
        in_specs=[pl.BlockSpec((SEQ, ATT_WIDTH), lambda b: (b, 0)),
                  pl.BlockSpec((SEQ, 4 * LANE), lambda b: (b, 0)),
                  pl.BlockSpec((SEQ, 4 * LANE), lambda b: (b, 0))],
        out_specs=pl.BlockSpec((SEQ, ATT_WIDTH), lambda b: (b, 0)),
        out_shape=jax.ShapeDtypeStruct((N_CTX, ATT_WIDTH), BF16),
        compiler_params=_cparams(("arbitrary",)),
        name="attention_ctx",
    )(q, kp, vp)
    tq = 256
    off = N_CTX // DEC_SEQ
    att_lat = pl.pallas_call(
        functools.partial(_att_kernel, has_cache=True),
        grid=(DEC_BATCH, DEC_SEQ // tq),
        in_specs=[pl.BlockSpec((tq, ATT_WIDTH), lambda b, i: (N_CTX // tq + b * (DEC_SEQ // tq) + i, 0)),
                  pl.BlockSpec((DEC_SEQ, 4 * LANE), lambda b, i: (off + b, 0)),
                  pl.BlockSpec((DEC_SEQ, 4 * LANE), lambda b, i: (off + b, 0)),
                  pl.BlockSpec((None, PAST_LEN, KV_WIDTH), lambda b, i: (b, 0, 0)),
                  pl.BlockSpec((None, PAST_LEN, KV_WIDTH), lambda b, i: (b, 0, 0))],
        out_specs=pl.BlockSpec((tq, ATT_WIDTH), lambda b, i: (b * (DEC_SEQ // tq) + i, 0)),
        out_shape=jax.ShapeDtypeStruct((N_LAT, ATT_WIDTH), BF16),
        compiler_params=_cparams(("arbitrary", "arbitrary")),
        name="attention_lat",
    )(q, kp, vp, cache_k.reshape(DEC_BATCH, PAST_LEN, KV_WIDTH), cache_v.reshape(DEC_BATCH, PAST_LEN, KV_WIDTH))
    return att_ctx, att_lat


def _four_ctx_kernel(u_ref, c_ref, s_ref, o_ref):
    uc = u_ref[:, :FOURIER_WIDTH]
    us = u_ref[:, FOURIER_WIDTH:]
    o_ref[...] = (jnp.dot(c_ref[...], uc, preferred_element_type=F32)
                  - jnp.dot(s_ref[...], us, preferred_element_type=F32))


FCH = 8


def _four_lat_a_kernel(u_ref, w1_ref, w2_ref, tc_ref, ts_ref, o_ref):
    w1 = w1_ref[...]
    w2 = w2_ref[...]
    for j in range(FCH):
        uc = u_ref[:, j * 2 * FOURIER_WIDTH:j * 2 * FOURIER_WIDTH + FOURIER_WIDTH]
        us = u_ref[:, j * 2 * FOURIER_WIDTH + FOURIER_WIDTH:(j + 1) * 2 * FOURIER_WIDTH]
        z = jnp.dot(w1, uc, preferred_element_type=F32) + jnp.dot(w2, us, preferred_element_type=F32)
        zr = z[:GRID_W]
        zi = z[GRID_W:]
        tc = jnp.concatenate([tc_ref[j]] * (FOURIER_WIDTH // LANE), axis=1)
        ts = jnp.concatenate([ts_ref[j]] * (FOURIER_WIDTH // LANE), axis=1)
        o_ref[j, :, :FOURIER_WIDTH] = (zr * tc - zi * ts).astype(BF16)
        o_ref[j, :, FOURIER_WIDTH:] = (zr * ts + zi * tc).astype(BF16)


def _four_lat_b_kernel(b_ref, c_ref, s_ref, o_ref):
    c = c_ref[...]
    s = s_ref[...]
    for j in range(FCH):
        br = b_ref[:, j * 2 * FOURIER_WIDTH:j * 2 * FOURIER_WIDTH + FOURIER_WIDTH]
        bi = b_ref[:, j * 2 * FOURIER_WIDTH + FOURIER_WIDTH:(j + 1) * 2 * FOURIER_WIDTH]
        o_ref[:, j, :] = (jnp.dot(c, br, preferred_element_type=F32) - jnp.dot(s, bi, preferred_element_type=F32))


def _fourier(u):
    c256, s256 = _dft_cos_sin(SEQ, SEQ ** -0.5)
    mixed_ctx = pl.pallas_call(
        _four_ctx_kernel,
        grid=(BATCH,),
        in_specs=[pl.BlockSpec((SEQ, 2 * FOURIER_WIDTH), lambda b: (b, 0)),
                  _const_spec((SEQ, SEQ)), _const_spec((SEQ, SEQ))],
        out_specs=pl.BlockSpec((SEQ, FOURIER_WIDTH), lambda b: (b, 0)),
        out_shape=jax.ShapeDtypeStruct((N_CTX, FOURIER_WIDTH), F32),
        compiler_params=_cparams(("arbitrary",)),
        name="fourier_ctx",
    )(u, jnp.asarray(c256, BF16), jnp.asarray(s256, BF16))

    g = GRID_W
    c64, s64 = _dft_cos_sin(g, g ** -0.5)
    w1 = jnp.asarray(np.concatenate([c64, s64], axis=0), BF16)
    w2 = jnp.asarray(np.concatenate([-s64, c64], axis=0), BF16)
    t2 = np.arange(g)[:, None]
    k1 = np.arange(g)[None, :]
    ang = 2.0 * np.pi * (t2 * k1) / (g * g)
    tw_c = jnp.asarray(np.broadcast_to(np.cos(ang)[:, :, None], (g, g, LANE)), F32)
    tw_s = jnp.asarray(np.broadcast_to(np.sin(ang)[:, :, None], (g, g, LANE)), F32)
    width = 2 * FOURIER_WIDTH
    u_lat = u[N_CTX:].reshape(DEC_BATCH, g, g * width)
    stage1 = pl.pallas_call(
        _four_lat_a_kernel,
        grid=(DEC_BATCH, g // FCH),
        in_specs=[pl.BlockSpec((None, g, FCH * width), lambda b, i: (b, 0, i)),
                  _const_spec((2 * g, g)), _const_spec((2 * g, g)),
                  pl.BlockSpec((FCH, g, LANE), lambda b, i: (i, 0, 0)),
                  pl.BlockSpec((FCH, g, LANE), lambda b, i: (i, 0, 0))],
        out_specs=pl.BlockSpec((None, FCH, g, width), lambda b, i: (b, i, 0, 0)),
        out_shape=jax.ShapeDtypeStruct((DEC_BATCH, g, g, width), BF16),
        compiler_params=_cparams(("arbitrary", "arbitrary")),
        name="fourier_lat_rows",
    )(u_lat, w1, w2, tw_c, tw_s)
    stage1 = stage1.reshape(DEC_BATCH, g, g * width)
    mixed_lat = pl.pallas_call(
        _four_lat_b_kernel,
        grid=(DEC_BATCH, g // FCH),
        in_specs=[pl.BlockSpec((None, g, FCH * width), lambda b, i: (b, 0, i)),
                  _const_spec((g, g)), _const_spec((g, g))],
        out_specs=pl.BlockSpec((None, g, FCH, FOURIER_WIDTH), lambda b, i: (b, 0, i, 0)),
        out_shape=jax.ShapeDtypeStruct((DEC_BATCH, g, g, FOURIER_WIDTH), F32),
        compiler_params=_cparams(("arbitrary", "arbitrary")),
        name="fourier_lat_cols",
    )(stage1, jnp.asarray(c64, BF16), jnp.asarray(s64, BF16))
    return mixed_ctx, mixed_lat.reshape(N_LAT, FOURIER_WIDTH)


def _pack_bf16_pairs(x):
    n = x.shape[1] // 2
    lo = pltpu.bitcast(x[:, :n].astype(BF16).astype(F32), U32)
    hi = pltpu.bitcast(x[:, n:].astype(BF16).astype(F32), U32)
    return (hi & jnp.uint32(0xFFFF0000)) | (lo >> 16)


def _unpack_bf16_pairs(w):
    lo = pltpu.bitcast(w << 16, F32)
    hi = pltpu.bitcast(w & jnp.uint32(0xFFFF0000), F32)
    return jnp.concatenate([lo, hi], axis=1)


def _route(logits, rb_ref, tri_ref, carry_ref):
    lt = logits.T
    score = [jax.nn.sigmoid(lt[e:e + 1, :]) for e in range(N_EXPERTS)]
    choice = [score[e] + rb_ref[e:e + 1, :] for e in range(N_EXPERTS)]
    best = jnp.zeros_like(score[0], dtype=jnp.int32)
    best_v = None
    for gi in range(N_EXPERT_GROUPS):
        c = choice[gi * EXPERTS_PER_GROUP:(gi + 1) * EXPERTS_PER_GROUP]
        top2 = None
        for a in range(EXPERTS_PER_GROUP):
            for b in range(a + 1, EXPERTS_PER_GROUP):
                pair = c[a] + c[b]
                top2 = pair if top2 is None else jnp.maximum(top2, pair)
        if best_v is None:
            best_v = top2
        else:
            better = top2 > best_v
            best = jnp.where(better, gi, best)
            best_v = jnp.where(better, top2, best_v)
    sel = []
    picked = []
    for e in range(N_EXPERTS):
        gi = e // EXPERTS_PER_GROUP
        rank = jnp.zeros_like(best)
        for o in range(gi * EXPERTS_PER_GROUP, (gi + 1) * EXPERTS_PER_GROUP):
            if o == e:
                continue
            ahead = (choice[o] > choice[e]) | ((choice[o] == choice[e]) & (o < e))
            rank = rank + ahead.astype(jnp.int32)
        chosen = (best == gi) & (rank < 2)
        sel.append(jnp.where(chosen, 1.0, 0.0))
        picked.append(jnp.where(chosen, score[e], 0.0))
    total = picked[0]
    for e in range(1, N_EXPERTS):
        total = total + picked[e]
    inv = 1.0 / total
    gate = [w * inv for w in picked]

    member = [sel[BUCKET_A[k]] * sel[BUCKET_B[k]] for k in range(N_BUCKETS)]
    bucket = member[1]
    wa = member[0] * gate[BUCKET_A[0]]
    wb = member[0] * gate[BUCKET_B[0]]
    for k in range(1, N_BUCKETS):
        if k > 1:
            bucket = bucket + float(k) * member[k]
        wa = wa + member[k] * gate[BUCKET_A[k]]
        wb = wb + member[k] * gate[BUCKET_B[k]]
    tm = bucket.shape[1]
    onehot = jnp.concatenate(member + [jnp.zeros((BUCKET_ROWS - N_BUCKETS, tm), F32)], axis=0)
    earlier = jnp.dot(onehot.astype(BF16), tri_ref[...], preferred_element_type=F32)
    carry = carry_ref[...]
    rank = jnp.sum(onehot * (earlier + carry[:, 0:1]), axis=0, keepdims=True)
    carry_ref[...] = carry + jnp.sum(onehot, axis=1, keepdims=True)
    return bucket.astype(I32), rank.astype(I32), wa, wb


SLAB = 32


def _row_slabs(n_rows, slab, body):
    for i in range(n_rows // slab):
        body(pl.ds(i * slab, slab))


def _residual_router(get_x, acc_ref, g1_ref, nf_ref, sc2_ref, sh2_ref, wr_ref, rb_ref, tri_ref,
                     x1_ref, rows_ref, meta_ref, count_ref, carry_ref, hhi_ref, hlo_ref):
    @pl.when(pl.program_id(0) == 0)
    def _():
        carry_ref[...] = jnp.zeros_like(carry_ref)

    g1 = g1_ref[...]
    gain = nf_ref[...] * (1.0 + sc2_ref[...])
    shift = sh2_ref[...]

    def slab(rows):
        x1 = get_x(rows) + g1 * acc_ref[rows, :]
        x1_ref[rows, :] = x1
        h2 = x1 * lax.rsqrt(jnp.mean(x1 * x1, axis=-1, keepdims=True) + EPS) * gain + shift
        hi = h2.astype(BF16)
        hi32 = hi.astype(F32)
        hhi_ref[rows, :] = hi
        hlo_ref[rows, :] = (h2 - hi32).astype(BF16)
        bits = pltpu.bitcast(hi32, U32)
        rows_ref[rows, :HALF] = (bits[:, HALF:] & jnp.uint32(0xFFFF0000)) | (bits[:, :HALF] >> 16)

    _row_slabs(acc_ref.shape[0], SLAB, slab)
    both = jnp.dot(hhi_ref[...], wr_ref[...], preferred_element_type=F32)
    logits = (both[:, :LANE] + both[:, LANE:]
              + jnp.dot(hlo_ref[...], wr_ref[:, :LANE], preferred_element_type=F32))
    bucket, rank, wa, wb = _route(logits, rb_ref, tri_ref, carry_ref)
    tm = logits.shape[0]
    gates_t = jnp.concatenate([wa, wb, jnp.zeros((LANE - 2, tm), F32)], axis=0)
    rows_ref[:, HALF:] = pltpu.bitcast(gates_t.T, U32)
    meta_ref[...] = jnp.concatenate([bucket, rank, jnp.zeros((6, tm), I32)], axis=0)
    count_ref[...] = carry_ref[...]


def _out0_kernel(attc_ref, attl_ref, mixc_ref, mixl_ref, wf_ref, woa_ref, wof_ref, xp_ref, xs_ref, g1_ref, nf_ref,
                 sc2_ref, sh2_ref, wr_ref, rb_ref, tri_ref, x1_ref, rows_ref, meta_ref, count_ref,
                 carry_ref, hhi_ref, hlo_ref, acc_ref):
    is_lat = pl.program_id(0) >= N_CTX_TILES
    att = jnp.where(is_lat, attl_ref[...], attc_ref[...])
    mix = jnp.where(is_lat, mixl_ref[...], mixc_ref[...])
    four = jnp.dot(mix.astype(BF16), wf_ref[...], preferred_element_type=F32)
    acc_ref[...] = (jnp.dot(att, woa_ref[...], preferred_element_type=F32)
                    + jnp.dot(four.astype(BF16), wof_ref[...], preferred_element_type=F32))

    def get_x(rows):
        return jnp.where(is_lat, xs_ref[rows, :], xp_ref[rows, :])

    _residual_router(get_x, acc_ref, g1_ref, nf_ref, sc2_ref, sh2_ref, wr_ref, rb_ref, tri_ref,
                     x1_ref, rows_ref, meta_ref, count_ref, carry_ref, hhi_ref, hlo_ref)


def _router_operands(w_router, router_bias):
    wr = jnp.zeros((D_MODEL, LANE), F32).at[:, :N_EXPERTS].set(w_router)
    wr_hi = wr.astype(BF16)
    wr_lo = (wr - wr_hi.astype(F32)).astype(BF16)
    rb = jnp.broadcast_to(router_bias.astype(F32)[:, None], (N_EXPERTS, TM))
    tri = jnp.asarray(np.triu(np.ones((TM, TM), np.float32), 1), BF16)
    return jnp.concatenate([wr_hi, wr_lo], axis=1), rb, tri


_EPILOGUE_OUT_SPECS = [_row_spec(D_MODEL), _row_spec(ROW_WORDS), pl.BlockSpec((8, TM), lambda i: (0, i)),
                       _const_spec((BUCKET_ROWS, LANE))]
_EPILOGUE_OUT_SHAPE = [jax.ShapeDtypeStruct((NT, D_MODEL), F32), jax.ShapeDtypeStruct((NT, ROW_WORDS), U32),
                       jax.ShapeDtypeStruct((8, NT), I32), jax.ShapeDtypeStruct((BUCKET_ROWS, LANE), F32)]
_EPILOGUE_SCRATCH = [pltpu.VMEM((BUCKET_ROWS, LANE), F32), pltpu.VMEM((TM, D_MODEL), BF16),
                     pltpu.VMEM((TM, D_MODEL), BF16), pltpu.VMEM((TM, D_MODEL), F32)]


def _epilogue_specs(layer):
    return [_mod_spec(layer, 2, TM), _const_spec((1, D_MODEL)), _mod_spec(layer, 4, TM),
            _mod_spec(layer, 3, TM), _const_spec((D_MODEL, 2 * LANE)), _const_spec((N_EXPERTS, TM)),
            _const_spec((TM, TM))]


def _out0(att, mixed, xp, xs, mods, w_fourier, w_out, norm_ffn0, router):
    wf = jnp.zeros((FOURIER_WIDTH, FOURIER_WIDTH), F32)
    for gi in range(N_FGROUPS):
        sl = slice(gi * FGROUP_DIM, (gi + 1) * FGROUP_DIM)
        wf = wf.at[sl, sl].set(w_fourier[gi])
    w_out = w_out.astype(BF16)
    wr, rb, tri = router
    return pl.pallas_call(
        _out0_kernel,
        grid=(NT // TM,),
        in_specs=_split_row_specs(ATT_WIDTH) + _split_row_specs(FOURIER_WIDTH)
        + [_const_spec((FOURIER_WIDTH, FOURIER_WIDTH)), _const_spec((ATT_WIDTH, D_MODEL)),
           _const_spec((FOURIER_WIDTH, D_MODEL))]
        + _split_row_specs(D_MODEL) + _epilogue_specs(0),
        out_specs=_EPILOGUE_OUT_SPECS,
        out_shape=_EPILOGUE_OUT_SHAPE,
        scratch_shapes=_EPILOGUE_SCRATCH,
        compiler_params=_cparams(("arbitrary",)),
        name="out_proj_att",
    )(att[0], att[1], mixed[0], mixed[1], wf.astype(BF16), w_out[:ATT_WIDTH], w_out[ATT_WIDTH:], xp, xs, mods,
      norm_ffn0.reshape(1, D_MODEL), mods, mods, wr, rb, tri)


def _dispatch_tables(meta, counts):
    bucket, rank = meta[0], meta[1]
    cnt = counts[:N_BUCKETS, 0].astype(I32)
    padded = (cnt + (TE - 1)) // TE * TE
    ends = jnp.cumsum(padded)
    starts = ends - padded
    kk = jnp.arange(N_BUCKETS, dtype=I32)
    pos = rank + jnp.sum(jnp.where(bucket[None, :] == kk[:, None], starts[:, None], 0), axis=0)
    tile0 = jnp.arange(N_ETILES, dtype=I32) * TE
    used = tile0 < ends[-1]
    tb = jnp.sum((tile0[:, None] >= ends[None, :]).astype(I32), axis=1)
    tb = jnp.where(used, tb, jnp.max(jnp.where(used, tb, 0)))
    pick = tb[:, None] == kk[None, :]

    def per_tile(table):
        return jnp.sum(jnp.where(pick, table[None, :], 0), axis=1)

    nrow = jnp.where(used, jnp.clip(per_tile(starts + cnt) - tile0, 0, TE), 0)
    pos3 = pos.reshape(SC_WORKERS, NT // (SC_WORKERS * SC_CHUNK), SC_CHUNK)
    return pos3, per_tile(jnp.asarray(BUCKET_A, I32)), per_tile(jnp.asarray(BUCKET_B, I32)), nrow


def _sc_permute(src, pos3, n_out, scatter):
    width = src.shape[1]
    _, n_chunks, chunk = pos3.shape
    rows_per_worker = n_chunks * chunk
    mesh = plsc.VectorSubcoreMesh(core_axis_name="c", subcore_axis_name="s")

    @functools.partial(pl.kernel, out_type=jax.ShapeDtypeStruct((n_out, width), src.dtype), mesh=mesh,
                       scratch_types=[pltpu.VMEM((n_chunks, chunk), I32), pltpu.VMEM((2, chunk, width), src.dtype),
                                      pltpu.SemaphoreType.DMA((2,))])
    def permute(src_hbm, pos_hbm, out_hbm, pos_v, buf, sem):
        worker = lax.axis_index("s") * SC_CORES + lax.axis_index("c")
        base = worker * rows_per_worker
        pltpu.sync_copy(pos_hbm.at[worker], pos_v)

        def own(j):
            return pl.ds(base + j * chunk, chunk)

        def load(j):
            rows = src_hbm.at[own(j)] if scatter else src_hbm.at[pos_v.at[j]]
            return pltpu.async_copy(rows, buf.at[j % 2], sem.at[j % 2])

        pending = load(0)
        for j in range(n_chunks):
            following = load(j + 1) if j + 1 < n_chunks else None
            pending.wait()
            pltpu.sync_copy(buf.at[j % 2], out_hbm.at[pos_v.at[j]] if scatter else out_hbm.at[own(j)])
            pending = following

    return permute(src, pos3)


def _experts_kernel(ea_ref, eb_ref, nrow_ref, rows_ref, wga_ref, wua_ref, wda_ref, wgb_ref, wub_ref, wdb_ref, y_ref,
                    cga, cua, cda, cgb, cub, cdb):
    j = pl.program_id(0)
    n = nrow_ref[j]

    @pl.when(n == 0)
    def _():
        y_ref[...] = jnp.zeros_like(y_ref)

    @pl.when(n > 0)
    def _():
        prev = jnp.maximum(j - 1, 0)

        @pl.when(jnp.logical_or(j == 0, ea_ref[j] != ea_ref[prev]))
        def _():
            cga[...] = wga_ref[...].astype(BF16)
            cua[...] = wua_ref[...].astype(BF16)
            cda[...] = wda_ref[...].astype(BF16)

        @pl.when(jnp.logical_or(j == 0, eb_ref[j] != eb_ref[prev]))
        def _():
            cgb[...] = wgb_ref[...].astype(BF16)
            cub[...] = wub_ref[...].astype(BF16)
            cdb[...] = wdb_ref[...].astype(BF16)

        valid = lax.broadcasted_iota(I32, (TE, 1), 0) < n
        h = jnp.where(valid, _unpack_bf16_pairs(rows_ref[:, :HALF]), 0.0).astype(BF16)
        gates = jnp.where(valid, pltpu.bitcast(rows_ref[:, HALF:], F32), 0.0)
        y = None
        for cg, cu, cd, col in ((cga, cua, cda, 0), (cgb, cub, cdb, 1)):
            a = jnp.dot(h, cg[...], preferred_element_type=F32)
            u = jnp.dot(h, cu[...], preferred_element_type=F32)
            hid = _silu(a) * u * gates[:, col:col + 1]
            o = jnp.dot(hid.astype(BF16), cd[...], preferred_element_type=F32)
            y = o if y is None else y + o
        y_ref[...] = _pack_bf16_pairs(y)


def _experts(rows_sorted, ea, eb, nrow, layer, w_gate, w_up, w_down):
    def w_spec(shape, which):
        return pl.BlockSpec((None, None) + shape, lambda j, ea, eb, nr: (layer, (ea, eb)[which][j], 0, 0))

    up_shape, down_shape = (D_MODEL, EXPERT_FF), (EXPERT_FF, D_MODEL)
    grid_spec = pltpu.PrefetchScalarGridSpec(
        num_scalar_prefetch=3,
        grid=(N_ETILES,),
        in_specs=[pl.BlockSpec((TE, ROW_WORDS), lambda j, *_: (j, 0)),
                  w_spec(up_shape, 0), w_spec(up_shape, 0), w_spec(down_shape, 0),
                  w_spec(up_shape, 1), w_spec(up_shape, 1), w_spec(down_shape, 1)],
        out_specs=pl.BlockSpec((TE, HALF), lambda j, *_: (j, 0)),
        scratch_shapes=[pltpu.VMEM(up_shape, BF16), pltpu.VMEM(up_shape, BF16), pltpu.VMEM(down_shape, BF16),
                        pltpu.VMEM(up_shape, BF16), pltpu.VMEM(up_shape, BF16), pltpu.VMEM(down_shape, BF16)])
    return pl.pallas_call(
        _experts_kernel,
        grid_spec=grid_spec,
        out_shape=jax.ShapeDtypeStruct((P_MAX, HALF), U32),
        compiler_params=_cparams(("arbitrary",)),
        name="experts_layer%d" % layer,
    )(ea, eb, nrow, rows_sorted, w_gate, w_up, w_down, w_gate, w_up, w_down)


def _moe(rows, meta, counts, layer, w_gate, w_up, w_down):
    pos3, ea, eb, nrow = _dispatch_tables(meta, counts)
    rows_sorted = _sc_permute(rows, pos3, P_MAX, scatter=True)
    y_sorted = _experts(rows_sorted, ea, eb, nrow, layer, w_gate, w_up, w_down)
    return _sc_permute(y_sorted, pos3, NT, scatter=False)


def _final_kernel(x_ref, y_ref, g2_ref, nfin_ref, yp_ref, ys_ref):
    i = pl.program_id(0)
    x = x_ref[...] + g2_ref[...] * _unpack_bf16_pairs(y_ref[...])
    out = x * lax.rsqrt(jnp.mean(x * x, axis=-1, keepdims=True) + EPS) * nfin_ref[...]

    @pl.when(i < N_CTX_TILES)
    def _():
        yp_ref[...] = out

    @pl.when(i >= N_CTX_TILES)
    def _():
        ys_ref[...] = out


def _final(x1, y_tok, mods, norm_final):
    return pl.pallas_call(
        _final_kernel,
        grid=(NT // TM,),
        in_specs=[_row_spec(D_MODEL), _row_spec(HALF), _mod_spec(DEPTH - 1, 5, TM), _const_spec((1, D_MODEL))],
        out_specs=_split_row_specs(D_MODEL),
        out_shape=[jax.ShapeDtypeStruct((N_CTX, D_MODEL), F32), jax.ShapeDtypeStruct((N_LAT, D_MODEL), F32)],
        compiler_params=_cparams(("arbitrary",)),
        name="final_norm",
    )(x1, y_tok, mods, norm_final.reshape(1, D_MODEL))


def _in1_kernel(x_ref, y_ref, g2_ref, nm_ref, sc_ref, sh_ref, w_ref, x2_ref, z_ref, xbc_ref, dt_ref, h_ref):
    g2 = g2_ref[...]
    gain = nm_ref[...] * (1.0 + sc_ref[...])
    shift = sh_ref[...]

    def norm_rows(rows):
        x = x_ref[rows, :] + g2 * _unpack_bf16_pairs(y_ref[rows, :])
        x2_ref[rows, :] = x
        h_ref[rows, :] = (x * lax.rsqrt(jnp.mean(x * x, axis=-1, keepdims=True) + EPS) * gain + shift).astype(BF16)

    _row_slabs(TM, SLAB, norm_rows)
    h = h_ref[...]
    step = 512
    for c0 in range(0, SSM_INNER, step):
        z_ref[:, c0:c0 + step] = jnp.dot(h, w_ref[:, c0:c0 + step], preferred_element_type=F32).astype(BF16)
    for c0 in range(0, SSM_CONV_CH, step):
        xbc_ref[:, c0:c0 + step] = jnp.dot(h, w_ref[:, SSM_INNER + c0:SSM_INNER + c0 + step],
                                            preferred_element_type=F32).astype(BF16)
    dt_ref[...] = jnp.dot(h, w_ref[:, SSM_INNER + SSM_CONV_CH:], preferred_element_type=F32)


def _in1(x1, y_tok, mods, norm_mix1, w_in):
    w = jnp.concatenate([w_in, jnp.zeros((D_MODEL, ODD_IN_PAD - ODD_IN), F32)], axis=1).astype(BF16)
    return pl.pallas_call(
        _in1_kernel,
        grid=(NT // TM,),
        in_specs=[_row_spec(D_MODEL), _row_spec(HALF), _mod_spec(0, 5, TM), _const_spec((1, D_MODEL)),
                  _mod_spec(1, 1, TM), _mod_spec(1, 0, TM), _const_spec((D_MODEL, ODD_IN_PAD))],
        out_specs=[_row_spec(D_MODEL), _row_spec(SSM_INNER), _row_spec(SSM_CONV_CH), _row_spec(LANE)],
        out_shape=[jax.ShapeDtypeStruct((NT, D_MODEL), F32), jax.ShapeDtypeStruct((NT, SSM_INNER), BF16),
                   jax.ShapeDtypeStruct((NT, SSM_CONV_CH), BF16), jax.ShapeDtypeStruct((NT, LANE), F32)],
        scratch_shapes=[pltpu.VMEM((TM, D_MODEL), BF16)],
        compiler_params=_cparams(("arbitrary",)),
        name="in_proj_ssm",
    )(x1, y_tok, mods, norm_mix1.reshape(1, D_MODEL), mods, mods, w)


TCV = 256


def _conv_kernel(x_ref, prev_ref, next_ref, w_ref, b_ref, o_ref):
    i = pl.program_id(0)
    nct = N_CTX // TCV
    j = (i - nct) % (DEC_SEQ // TCV)
    first = jnp.logical_or(i < nct, j == 0)
    last = jnp.logical_or(i < nct, j == DEC_SEQ // TCV - 1)
    row = lax.broadcasted_iota(jnp.int32, (TCV, LANE), 0)
    top = row == 0
    bottom = row == TCV - 1
    for c in range(SSM_CONV_CH // LANE):
        cols = slice(c * LANE, (c + 1) * LANE)
        x = x_ref[:, cols].astype(F32)
        prev_row = jnp.where(first, 0.0, prev_ref[7:8, cols].astype(F32))
        next_row = jnp.where(last, 0.0, next_ref[0:1, cols].astype(F32))
        xm1 = jnp.where(top, prev_row, pltpu.roll(x, 1, 0))
        xp1 = jnp.where(bottom, next_row, pltpu.roll(x, TCV - 1, 0))
        y = xm1 * w_ref[0:1, cols] + x * w_ref[1:2, cols] + xp1 * w_ref[2:3, cols] + b_ref[:, cols]
        o_ref[:, cols] = _silu(y).astype(BF16)


def _conv(xbc, conv_w, conv_b):
    r8 = TCV // 8
    nblk8 = NT // 8
    return pl.pallas_call(
        _conv_kernel,
        grid=(NT // TCV,),
        in_specs=[_row_spec(SSM_CONV_CH, TCV),
                  pl.BlockSpec((8, SSM_CONV_CH), lambda i: (jnp.maximum(i * r8 - 1, 0), 0)),
                  pl.BlockSpec((8, SSM_CONV_CH), lambda i: (jnp.minimum((i + 1) * r8, nblk8 - 1), 0)),
                  _const_spec((3, SSM_CONV_CH)), _const_spec((1, SSM_CONV_CH))],
        out_specs=_row_spec(SSM_CONV_CH, TCV),
        out_shape=jax.ShapeDtypeStruct((NT, SSM_CONV_CH), BF16),
        compiler_params=_cparams(("arbitrary",)),
        name="ssm_conv",
    )(xbc, xbc, xbc, conv_w, conv_b.reshape(1, SSM_CONV_CH))


HPG = SSM_HEADS // SSM_GROUPS
GW = HPG * SSM_HEAD_DIM


def _ssd_kernel(*refs, reverse, has_init, write_state):
    refs = list(refs)
    xc_ref, dt_ref, bias_ref, alog_ref = refs[:4]
    refs = refs[4:]
    h0_ref = refs.pop(0) if has_init else None
    y_ref = refs.pop(0)
    hout_ref = refs.pop(0) if write_state else None
    (h_ref,) = refs
    c = pl.program_id(1)
    nc = pl.num_programs(1)
    col0 = SSM_HEADS if reverse else 0
    Q = SSM_CHUNK

    @pl.when(c == 0)
    def _():
        for g in range(SSM_GROUPS):
            if has_init:
                h_ref[g] = h0_ref[g * GW:(g + 1) * GW, :].T
            else:
                h_ref[g] = jnp.zeros((SSM_STATE, GW), F32)

    dt = jax.nn.softplus(dt_ref[...] + bias_ref[...])
    a = dt * -jnp.exp(alog_ref[...])
    row = lax.broadcasted_iota(jnp.int32, (Q, Q), 0)
    col = lax.broadcasted_iota(jnp.int32, (Q, Q), 1)
    keep = (col >= row) if reverse else (col <= row)
    tri = keep.astype(F32)
    acs = jnp.dot(tri, a, precision=HIGHEST, preferred_element_type=F32)
    edge = (0 if reverse else Q - 1)
    acs_end = acs[edge:edge + 1, :]
    log2e = math.log2(math.e)
    acs2 = acs * log2e
    src_t = (acs2 - jnp.log2(dt)).T
    out_t = (jnp.exp(acs_end - acs) * dt).T
    chunk_decay = jnp.exp(jnp.broadcast_to(acs_end, (8, LANE)))
    nt_dims = (((1,), (1,)), ((), ()))
    lane = lax.broadcasted_iota(jnp.int32, (Q, LANE), 1)
    low = lane < SSM_HEAD_DIM
    low8 = low[:8]

    def two_heads(v):
        zero = jnp.zeros_like(v)
        return jnp.concatenate([jnp.where(low, v, zero), jnp.where(low, zero, v)], axis=0)

    for g in range(SSM_GROUPS):
        bg = xc_ref[:, SSM_INNER + g * SSM_STATE:SSM_INNER + (g + 1) * SSM_STATE]
        cg = xc_ref[:, SSM_INNER + SSM_GROUPS * SSM_STATE + g * SSM_STATE:
                    SSM_INNER + SSM_GROUPS * SSM_STATE + (g + 1) * SSM_STATE]
        cb = lax.dot_general(cg, bg, nt_dims, preferred_element_type=F32).astype(BF16)
        bg_t = bg.astype(F32).T
        for j in range(HPG // 2):
            pair = slice(g * GW + j * LANE, g * GW + (j + 1) * LANE)
            mats, c_in, b_out, cdec = [], [], [], []
            for hh in (2 * j, 2 * j + 1):
                cidx = col0 + g * HPG + hh
                to_l = jnp.broadcast_to(acs2[:, cidx:cidx + 1], (Q, Q))
                lmat = jnp.where(keep, jnp.exp2(to_l - src_t[cidx:cidx + 1, :]), 0.0)
                mats.append(cb * lmat.astype(BF16))
                c_in.append(cg * jnp.exp2(to_l).astype(BF16))
                b_out.append((bg_t * out_t[cidx:cidx + 1, :]).astype(BF16))
                cdec.append(jnp.broadcast_to(chunk_decay[:, cidx:cidx + 1], (8, LANE)))
            hp = h_ref[g, :, j * LANE:(j + 1) * LANE]
            x2 = two_heads(xc_ref[:, pair])
            rhs = jnp.concatenate([x2, two_heads(hp.astype(BF16))], axis=0)
            y = jnp.dot(jnp.concatenate(mats + c_in, axis=1), rhs, preferred_element_type=F32)
            y_ref[:, pair] = y.astype(BF16)
            st = jnp.dot(jnp.concatenate(b_out, axis=1), x2, preferred_element_type=F32)
            h_ref[g, :, j * LANE:(j + 1) * LANE] = hp * jnp.where(low8, cdec[0], cdec[1])[0:1, :] + st

    if write_state:
        @pl.when(c == nc - 1)
        def _():
            for g in range(SSM_GROUPS):
                hout_ref[g * GW:(g + 1) * GW, :] = h_ref[g].T


def _ssd_direction(xc, dt_raw, bias128, alog128, h0, reverse):
    outs = []
    for (row0, nseq, seqlen, has_init, write_state) in ((0, BATCH, SEQ, False, True),
                                                         (N_CTX, DEC_BATCH, DEC_SEQ, True, False)):
        nc = seqlen // SSM_CHUNK
        base = row0 // SSM_CHUNK

        def chunk_idx(s, c, nc=nc, base=base):
            cc = (nc - 1 - c) if reverse else c
            return (base + s * nc + cc, 0)

        def out_idx(s, c, nc=nc):
            cc = (nc - 1 - c) if reverse else c
            return (s * nc + cc, 0)

        state_spec = pl.BlockSpec((None, SSM_INNER, SSM_STATE), lambda s, c: (s, 0, 0))
        in_specs = [pl.BlockSpec((SSM_CHUNK, SSM_CONV_CH), chunk_idx), pl.BlockSpec((SSM_CHUNK, LANE), chunk_idx),
                    _const_spec((1, LANE)), _const_spec((1, LANE))]
        args = [xc, dt_raw, bias128, alog128]
        if has_init:
            in_specs.append(state_spec)
            args.append(h0)
        out_specs = [pl.BlockSpec((SSM_CHUNK, SSM_INNER), out_idx)]
        out_shape = [jax.ShapeDtypeStruct((nseq * seqlen, SSM_INNER), BF16)]
        if write_state:
            out_specs.append(state_spec)
            out_shape.append(jax.ShapeDtypeStruct((nseq, SSM_INNER, SSM_STATE), F32))
        outs.append(pl.pallas_call(
            functools.partial(_ssd_kernel, reverse=reverse, has_init=has_init, write_state=write_state),
            grid=(nseq, nc),
            in_specs=in_specs,
            out_specs=out_specs,
            out_shape=out_shape,
            scratch_shapes=[pltpu.VMEM((SSM_GROUPS, SSM_STATE, GW), F32)],
            compiler_params=_cparams(("arbitrary", "arbitrary")),
            name="ssd_%s_%s" % ("bwd" if reverse else "fwd", "lat" if has_init else "ctx"),
        )(*args))
    return (outs[0][0], outs[1][0]), outs[0][1]


def _out1_kernel(yfc_ref, yfl_ref, ybc_ref, ybl_ref, xs_ref, z_ref, dskip_ref, ng_ref, wo_ref, x_ref, g1_ref, nf_ref,
                 sc2_ref, sh2_ref, wr_ref, rb_ref, tri_ref, x1_ref, rows_ref, meta_ref, count_ref,
                 carry_ref, hhi_ref, hlo_ref, acc_ref, y_ref):
    is_lat = pl.program_id(0) >= N_CTX_TILES
    dskip = dskip_ref[...]
    gain = ng_ref[...]

    def gate_norm(rows):
        yf = jnp.where(is_lat, yfl_ref[rows, :], yfc_ref[rows, :])
        yb = jnp.where(is_lat, ybl_ref[rows, :], ybc_ref[rows, :])
        y = yf.astype(F32) + yb.astype(F32) + dskip * xs_ref[rows, :].astype(F32)
        y = y * _silu(z_ref[rows, :].astype(F32))
        y_ref[rows, :] = (y * lax.rsqrt(jnp.mean(y * y, axis=-1, keepdims=True) + EPS) * gain).astype(BF16)

    _row_slabs(TM, SLAB // 2, gate_norm)
    acc_ref[...] = jnp.dot(y_ref[...], wo_ref[...], preferred_element_type=F32)
    _residual_router(lambda rows: x_ref[rows, :], acc_ref, g1_ref, nf_ref, sc2_ref, sh2_ref, wr_ref, rb_ref,
                     tri_ref, x1_ref, rows_ref, meta_ref, count_ref, carry_ref, hhi_ref, hlo_ref)


def _out1(yf, yb, xc, z, d_skip, ssm_norm, w_out, x, mods, norm_ffn1, router):
    wr, rb, tri = router
    return pl.pallas_call(
        _out1_kernel,
        grid=(NT // TM,),
        in_specs=_split_row_specs(SSM_INNER) + _split_row_specs(SSM_INNER)
        + [_row_spec(SSM_INNER), _row_spec(SSM_INNER),
           _const_spec((1, SSM_INNER)), _const_spec((1, SSM_INNER)), _const_spec((SSM_INNER, D_MODEL)),
           _row_spec(D_MODEL)] + _epilogue_specs(1),
        out_specs=_EPILOGUE_OUT_SPECS,
        out_shape=_EPILOGUE_OUT_SHAPE,
        scratch_shapes=_EPILOGUE_SCRATCH + [pltpu.VMEM((TM, SSM_INNER), BF16)],
        compiler_params=_cparams(("arbitrary",)),
        name="out_proj_ssm",
    )(yf[0], yf[1], yb[0], yb[1], xc, z, jnp.repeat(d_skip.astype(F32), SSM_HEAD_DIM).reshape(1, SSM_INNER),
      ssm_norm.reshape(1, SSM_INNER), w_out.astype(BF16), x, mods, norm_ffn1.reshape(1, D_MODEL), mods, mods,
      wr, rb, tri)


def kernel(x_prompt, x_sample, cache_k, cache_v, state_fwd, state_bwd, c, c_ctx, w_ada, b_ada, norm_mix, norm_ffn,
           w_in_att, q_gain, k_gain, w_fourier, w_out_att, w_in_ssm, conv_w, conv_b, dt_bias_f, dt_bias_b, a_log_f,
           a_log_b, d_skip, ssm_norm, w_out_ssm, w_router, router_bias, w_gate, w_up, w_down, norm_final):
    xp = x_prompt.reshape(N_CTX, D_MODEL)
    xs = x_sample.reshape(N_LAT, D_MODEL)
    cond8 = jnp.zeros((8, D_MODEL), F32).at[0].set(c_ctx).at[1:1 + DEC_BATCH].set(c)
    mods = _ada_mods(cond8, w_ada, b_ada)
    router = _router_operands(w_router, router_bias)
    cos2, sin2 = _rope_tables()

    q, kp, vp, u, new_k, new_v = _in0(xp, xs, mods, norm_mix[0], w_in_att[0], q_gain[0], k_gain[0], cos2, sin2)
    att = _attention(q, kp, vp, cache_k[:, 0], cache_v[:, 0])
    mixed = _fourier(u)
    x1, rows, meta, counts = _out0(att, mixed, xp, xs, mods, w_fourier[0], w_out_att[0], norm_ffn[0], router)
    y_tok = _moe(rows, meta, counts, 0, w_gate, w_up, w_down)

    x2, z, xbc, dt_raw = _in1(x1, y_tok, mods, norm_mix[1], w_in_ssm[0])
    xc = _conv(xbc, conv_w[0], conv_b[0])
    pad = jnp.zeros((LANE - 2 * SSM_HEADS,), F32)
    bias128 = jnp.concatenate([dt_bias_f[0], dt_bias_b[0], pad]).astype(F32).reshape(1, LANE)
    alog128 = jnp.concatenate([a_log_f[0], a_log_b[0], pad]).astype(F32).reshape(1, LANE)
    yf, sf = _ssd_direction(xc, dt_raw, bias128, alog128, state_fwd.reshape(DEC_BATCH, SSM_INNER, SSM_STATE), False)
    yb, sb = _ssd_direction(xc, dt_raw, bias128, alog128, state_bwd.reshape(DEC_BATCH, SSM_INNER, SSM_STATE), True)
    x3, rows, meta, counts = _out1(yf, yb, xc, z, d_skip[0], ssm_norm[0], w_out_ssm[0], x2, mods, norm_ffn[1], router)
    y_tok = _moe(rows, meta, counts, 1, w_gate, w_up, w_down)
    y_prompt, y_sample = _final(x3, y_tok, mods, norm_final)

    state_shape = (BATCH, 1, SSM_HEADS, SSM_HEAD_DIM, SSM_STATE)
    return (y_prompt.reshape(BATCH, SEQ, D_MODEL), y_sample.reshape(DEC_BATCH, DEC_SEQ, D_MODEL),
            new_k.reshape(BATCH, 1, SEQ, N_KV_HEADS, HEAD_DIM), new_v.reshape(BATCH, 1, SEQ, N_KV_HEADS, HEAD_DIM),
            sf.reshape(state_shape), sb.reshape(state_shape))
```

```python
import functools
import math

import numpy as np
import jax
import jax.numpy as jnp
from jax import lax
from jax.experimental import pallas as pl
from jax.experimental.pallas import tpu as pltpu
from jax.experimental.pallas import tpu_sc as plsc

F32 = jnp.float32
BF16 = jnp.bfloat16
U32 = jnp.uint32
I32 = jnp.int32
HIGHEST = lax.Precision.HIGHEST

D_MODEL = 1024
BATCH = 32
SEQ = 256
DEPTH = 2
DEC_BATCH = 2
DEC_SEQ = 4096
PAST_LEN = 256
GRID_W = 64
EPS = 1e-6
N_HEADS = 8
N_KV_HEADS = 2
HEAD_DIM = 64
ATT_WIDTH = N_HEADS * HEAD_DIM
KV_WIDTH = N_KV_HEADS * HEAD_DIM
ROPE_THETA = 10000.0
N_FGROUPS = 8
FGROUP_DIM = 64
FOURIER_WIDTH = N_FGROUPS * FGROUP_DIM
EVEN_IN = ATT_WIDTH + 2 * KV_WIDTH + FOURIER_WIDTH
SSM_INNER = 2 * D_MODEL
SSM_HEAD_DIM = 64
SSM_HEADS = SSM_INNER // SSM_HEAD_DIM
SSM_GROUPS = 4
SSM_STATE = 128
SSM_CHUNK = 128
SSM_CONV_CH = SSM_INNER + 2 * SSM_GROUPS * SSM_STATE
ODD_IN = SSM_INNER + SSM_CONV_CH + 2 * SSM_HEADS
N_EXPERTS = 16
EXPERTS_PER_GROUP = 4
N_EXPERT_GROUPS = 4
EXPERT_FF = 512

N_CTX = BATCH * SEQ
N_LAT = DEC_BATCH * DEC_SEQ
NT = N_CTX + N_LAT
N_SEG = 1 + DEC_BATCH
LANE = 128
VMEM_LIMIT = 56 * 1024 * 1024

TM = 512
N_CTX_TILES = N_CTX // TM
LAT_TILES_PER_SEQ = DEC_SEQ // TM

PAIR_ORDER = ((0, 1), (0, 2), (0, 3), (1, 3), (1, 2), (3, 2))
N_BUCKETS = N_EXPERT_GROUPS * len(PAIR_ORDER)
BUCKET_A = tuple(g * EXPERTS_PER_GROUP + a for g in range(N_EXPERT_GROUPS) for a, _ in PAIR_ORDER)
BUCKET_B = tuple(g * EXPERTS_PER_GROUP + b for g in range(N_EXPERT_GROUPS) for _, b in PAIR_ORDER)
BUCKET_ROWS = 32
TE = 256
N_ETILES = (NT + N_BUCKETS * (TE - 1) + TE - 1) // TE
P_MAX = N_ETILES * TE
HALF = D_MODEL // 2
ROW_WORDS = HALF + LANE
SC_CORES = 2
SC_SUBCORES = 16
SC_WORKERS = SC_CORES * SC_SUBCORES
SC_CHUNK = 64


def _cparams(sem):
    return pltpu.CompilerParams(dimension_semantics=sem, vmem_limit_bytes=VMEM_LIMIT)


def _seg_of_tile(i, tm):
    nct = N_CTX // tm
    return jnp.where(i < nct, 0, 1 + (i - nct) // (DEC_SEQ // tm))


def _mod_spec(layer, which, tm):
    return pl.BlockSpec((None, None, None, 1, D_MODEL),
                        lambda i, *_: (layer, which, _seg_of_tile(i, tm), 0, 0))


def _row_spec(width, tm=TM):
    return pl.BlockSpec((tm, width), lambda i, *_: (i, 0))


def _const_spec(shape):
    nd = len(shape)
    return pl.BlockSpec(shape, lambda *_: (0,) * nd)


def _silu(x):
    return x * jax.nn.sigmoid(x)


def _ada_kernel(cond_ref, w_ref, b_ref, o_ref):
    c = cond_ref[...]
    o_ref[...] = jnp.dot(_silu(c), w_ref[...], precision=HIGHEST, preferred_element_type=F32) + b_ref[...]


def _ada_mods(cond8, w_ada, b_ada):
    tn = 1536
    out = pl.pallas_call(
        _ada_kernel,
        grid=(DEPTH, 6 * D_MODEL // tn),
        in_specs=[pl.BlockSpec((8, D_MODEL), lambda l, n: (0, 0)),
                  pl.BlockSpec((None, D_MODEL, tn), lambda l, n: (l, 0, n)),
                  pl.BlockSpec((None, 1, tn), lambda l, n: (l, 0, n))],
        out_specs=pl.BlockSpec((None, 8, tn), lambda l, n: (l, 0, n)),
        out_shape=jax.ShapeDtypeStruct((DEPTH, 8, 6 * D_MODEL), F32),
        compiler_params=_cparams(("arbitrary", "arbitrary")),
        name="ada_mod",
    )(cond8, w_ada, b_ada.reshape(DEPTH, 1, 6 * D_MODEL))
    return out.reshape(DEPTH, 8, 6, D_MODEL)[:, :N_SEG].transpose(0, 2, 1, 3)[:, :, :, None, :]


def _rope_tables():
    t = np.arange(DEC_SEQ)
    row = (t // GRID_W).astype(np.float64)
    col = (t % GRID_W).astype(np.float64)
    axis_dim = HEAD_DIM // 2
    freqs = ROPE_THETA ** (-np.arange(0, axis_dim, 2, dtype=np.float64) / axis_dim)
    ang = np.concatenate([row[:, None] * freqs, col[:, None] * freqs], axis=-1)
    cos = np.repeat(np.cos(ang), 2, axis=1)
    sin = np.repeat(np.sin(ang), 2, axis=1)
    sign = np.where(np.arange(HEAD_DIM) % 2 == 0, -1.0, 1.0)
    cos2 = np.tile(cos, (1, 2)).astype(np.float32)
    sin2 = np.tile(sin * sign, (1, 2)).astype(np.float32)
    return jnp.asarray(cos2), jnp.asarray(sin2)


def _dft_cos_sin(n, scale):
    k = np.arange(n)
    ang = 2.0 * np.pi * ((k[:, None] * k[None, :]) % n) / n
    return np.cos(ang) * scale, np.sin(ang) * scale


def _block_diag(m, reps):
    n = m.shape[0]
    out = np.zeros((n * reps, n * reps), m.dtype)
    for r in range(reps):
        out[r * n:(r + 1) * n, r * n:(r + 1) * n] = m
    return out


def _channel_dft():
    c, s = _dft_cos_sin(FGROUP_DIM, FGROUP_DIM ** -0.5)
    return jnp.asarray(np.concatenate([_block_diag(c, N_FGROUPS), _block_diag(s, N_FGROUPS)], axis=1), BF16)


def _group_ones(width):
    return jnp.asarray(_block_diag(np.ones((HEAD_DIM, HEAD_DIM), np.float32), width // HEAD_DIM), BF16)


def _pad_heads(x):
    lane = lax.broadcasted_iota(jnp.int32, x.shape, 1)
    low = lane < HEAD_DIM
    xr = pltpu.roll(x, HEAD_DIM, 1)
    zero = jnp.zeros_like(x)
    return [jnp.where(low, x, zero), jnp.where(low, zero, xr), jnp.where(low, xr, zero), jnp.where(low, zero, x)]


def _ctx_tile(i, *_):
    return (jnp.minimum(i, N_CTX_TILES - 1), 0)


def _lat_tile(i, *_):
    return (jnp.maximum(i - N_CTX_TILES, 0), 0)


def _split_row_specs(width):
    return [pl.BlockSpec((TM, width), _ctx_tile), pl.BlockSpec((TM, width), _lat_tile)]


def _in0_kernel(xp_ref, xs_ref, nm_ref, sc_ref, sh_ref, w_ref, qg_ref, kg_ref, ones_ref, cos_ref, sin_ref, dft_ref,
                q_ref, kp_ref, vp_ref, u_ref, nk_ref, nv_ref, h_ref):
    i = pl.program_id(0)
    is_lat = i >= N_CTX_TILES
    gain = nm_ref[...] * (1.0 + sc_ref[...])
    shift = sh_ref[...]

    def norm_rows(rows):
        x = jnp.where(is_lat, xs_ref[rows, :], xp_ref[rows, :])
        h_ref[rows, :] = (x * lax.rsqrt(jnp.mean(x * x, axis=-1, keepdims=True) + EPS) * gain + shift).astype(BF16)

    _row_slabs(TM, SLAB, norm_rows)
    p = jnp.dot(h_ref[...], w_ref[...], preferred_element_type=F32)
    q = p[:, :ATT_WIDTH]
    k = p[:, ATT_WIDTH:ATT_WIDTH + KV_WIDTH]
    v = p[:, ATT_WIDTH + KV_WIDTH:ATT_WIDTH + 2 * KV_WIDTH]
    f = p[:, ATT_WIDTH + 2 * KV_WIDTH:]
    ones = ones_ref[...]
    qss = jnp.dot((q * q).astype(BF16), ones, preferred_element_type=F32)
    kss = jnp.dot((k * k).astype(BF16), ones[:KV_WIDTH, :KV_WIDTH], preferred_element_type=F32)
    qn = q * lax.rsqrt(qss * (1.0 / HEAD_DIM) + EPS) * qg_ref[...]
    kn = k * lax.rsqrt(kss * (1.0 / HEAD_DIM) + EPS) * kg_ref[...]

    cos = jnp.where(is_lat, cos_ref[...], 1.0)
    sin = jnp.where(is_lat, sin_ref[...], 0.0)
    lane = lax.broadcasted_iota(jnp.int32, (TM, LANE), 1)
    even = (lane % 2) == 0

    def rope(xc):
        swapped = jnp.where(even, pltpu.roll(xc, LANE - 1, 1), pltpu.roll(xc, 1, 1))
        return xc * cos + swapped * sin

    scale = HEAD_DIM ** -0.5 * math.log2(math.e)
    for j in range(ATT_WIDTH // LANE):
        q_ref[:, j * LANE:(j + 1) * LANE] = (rope(qn[:, j * LANE:(j + 1) * LANE]) * scale).astype(BF16)
    for j, c in enumerate(_pad_heads(rope(kn))):
        kp_ref[:, j * LANE:(j + 1) * LANE] = c.astype(BF16)
    for j, c in enumerate(_pad_heads(v)):
        vp_ref[:, j * LANE:(j + 1) * LANE] = c.astype(BF16)
    u_ref[...] = jnp.dot(f.astype(BF16), dft_ref[...], preferred_element_type=F32).astype(BF16)

    @pl.when(jnp.logical_not(is_lat))
    def _():
        for r in range(TM // SEQ):
            nk_ref[r] = kn[r * SEQ:(r + 1) * SEQ, :].T
            nv_ref[r] = v[r * SEQ:(r + 1) * SEQ, :].T


def _in0(xp, xs, mods, norm_mix0, w_in, q_gain, k_gain, cos2, sin2):
    def table_idx(i):
        return (jnp.where(i < N_CTX_TILES, 0, (i - N_CTX_TILES) % LAT_TILES_PER_SEQ), 0)

    def ctx_idx(i):
        return (jnp.minimum(i, N_CTX_TILES - 1), 0, 0)

    seqs = TM // SEQ
    outs = pl.pallas_call(
        _in0_kernel,
        grid=(NT // TM,),
        in_specs=_split_row_specs(D_MODEL) + [_const_spec((1, D_MODEL)), _mod_spec(0, 1, TM), _mod_spec(0, 0, TM),
                  _const_spec((D_MODEL, EVEN_IN)), _const_spec((1, ATT_WIDTH)), _const_spec((1, KV_WIDTH)),
                  _const_spec((ATT_WIDTH, ATT_WIDTH)),
                  pl.BlockSpec((TM, LANE), table_idx), pl.BlockSpec((TM, LANE), table_idx),
                  _const_spec((FOURIER_WIDTH, 2 * FOURIER_WIDTH))],
        out_specs=[_row_spec(ATT_WIDTH), _row_spec(4 * LANE), _row_spec(4 * LANE), _row_spec(2 * FOURIER_WIDTH),
                   pl.BlockSpec((seqs, KV_WIDTH, SEQ), ctx_idx), pl.BlockSpec((seqs, KV_WIDTH, SEQ), ctx_idx)],
        out_shape=[jax.ShapeDtypeStruct((NT, ATT_WIDTH), BF16), jax.ShapeDtypeStruct((NT, 4 * LANE), BF16),
                   jax.ShapeDtypeStruct((NT, 4 * LANE), BF16), jax.ShapeDtypeStruct((NT, 2 * FOURIER_WIDTH), BF16),
                   jax.ShapeDtypeStruct((BATCH, KV_WIDTH, SEQ), F32), jax.ShapeDtypeStruct((BATCH, KV_WIDTH, SEQ), F32)],
        scratch_shapes=[pltpu.VMEM((TM, D_MODEL), BF16)],
        compiler_params=_cparams(("arbitrary",)),
        name="in_proj_att",
    )(xp, xs, norm_mix0.reshape(1, D_MODEL), mods, mods, w_in.astype(BF16),
      jnp.tile(q_gain, N_HEADS).reshape(1, ATT_WIDTH), jnp.tile(k_gain, N_KV_HEADS).reshape(1, KV_WIDTH),
      _group_ones(ATT_WIDTH), cos2, sin2, _channel_dft())
    return outs


def _att_kernel(*refs, has_cache):
    if has_cache:
        q_ref, kp_ref, vp_ref, ck_ref, cv_ref, o_ref = refs
        ckp = [c.astype(BF16) for c in _pad_heads(ck_ref[...])]
        cvp = [c.astype(BF16) for c in _pad_heads(cv_ref[...])]
    else:
        q_ref, kp_ref, vp_ref, o_ref = refs
    nt_dims = (((1,), (1,)), ((), ()))
    for j in range(ATT_WIDTH // LANE):
        qj = q_ref[:, j * LANE:(j + 1) * LANE]
        g = j // 2
        acc = None
        for half in range(2):
            c = 2 * g + half
            kk = kp_ref[:, c * LANE:(c + 1) * LANE]
            vv = vp_ref[:, c * LANE:(c + 1) * LANE]
            s = lax.dot_general(qj, kk, nt_dims, preferred_element_type=F32)
            m = jnp.max(s, axis=-1, keepdims=True)
            if has_cache:
                sc = lax.dot_general(qj, ckp[c], nt_dims, preferred_element_type=F32)
                m = jnp.maximum(m, jnp.max(sc, axis=-1, keepdims=True))
            p = jnp.exp2(s - m)
            d = jnp.sum(p, axis=-1, keepdims=True)
            o = jnp.dot(p.astype(BF16), vv, preferred_element_type=F32)
            if has_cache:
                pc = jnp.exp2(sc - m)
                d = d + jnp.sum(pc, axis=-1, keepdims=True)
                o = o + jnp.dot(pc.astype(BF16), cvp[c], preferred_element_type=F32)
            o = o * (1.0 / d)
            acc = o if acc is None else acc + o
        o_ref[:, j * LANE:(j + 1) * LANE] = acc.astype(BF16)


def _attention(q, kp, vp, cache_k, cache_v):
    att_ctx = pl.pallas_call(
        functools.partial(_att_kernel, has_cache=False),
        grid=(BATCH,),
        in_specs=[pl.BlockSpec((SEQ, ATT_WIDTH), lambda b: (b, 0)),
                  pl.BlockSpec((SEQ, 4 * LANE), lambda b: (b, 0)),
                  pl.BlockSpec((SEQ, 4 * LANE), lambda b: (b, 0))],
        out_specs=pl.BlockSpec((SEQ, ATT_WIDTH), lambda b: (b, 0)),
        out_shape=jax.ShapeDtypeStruct((N_CTX, ATT_WIDTH), BF16),
        compiler_params=_cparams(("arbitrary",)),
        name="attention_ctx",
    )(q, kp, vp)
    tq = 256
    off = N_CTX // DEC_SEQ
    att_lat = pl.pallas_call(
        functools.partial(_att_kernel, has_cache=True),
        grid=(DEC_BATCH, DEC_SEQ // tq),
        in_specs=[pl.BlockSpec((tq, ATT_WIDTH), lambda b, i: (N_CTX // tq + b * (DEC_SEQ // tq) + i, 0)),
                  pl.BlockSpec((DEC_SEQ, 4 * LANE), lambda b, i: (off + b, 0)),
                  pl.BlockSpec((DEC_SEQ, 4 * LANE), lambda b, i: (off + b, 0)),
                  pl.BlockSpec((None, PAST_LEN, KV_WIDTH), lambda b, i: (b, 0, 0)),
                  pl.BlockSpec((None, PAST_LEN, KV_WIDTH), lambda b, i: (b, 0, 0))],
        out_specs=pl.BlockSpec((tq, ATT_WIDTH), lambda b, i: (b * (DEC_SEQ // tq) + i, 0)),
        out_shape=jax.ShapeDtypeStruct((N_LAT, ATT_WIDTH), BF16),
        compiler_params=_cparams(("arbitrary", "arbitrary")),
        name="attention_lat",
    )(q, kp, vp, cache_k.reshape(DEC_BATCH, PAST_LEN, KV_WIDTH), cache_v.reshape(DEC_BATCH, PAST_LEN, KV_WIDTH))
    return att_ctx, att_lat


def _four_ctx_kernel(u_ref, c_ref, s_ref, o_ref):
    uc = u_ref[:, :FOURIER_WIDTH]
    us = u_ref[:, FOURIER_WIDTH:]
    o_ref[...] = (jnp.dot(c_ref[...], uc, preferred_element_type=F32)
                  - jnp.dot(s_ref[...], us, preferred_element_type=F32))


FCH = 8


def _four_lat_a_kernel(u_ref, w1_ref, w2_ref, tc_ref, ts_ref, o_ref):
    w1 = w1_ref[...]
    w2 = w2_ref[...]
    for j in range(FCH):
        uc = u_ref[:, j * 2 * FOURIER_WIDTH:j * 2 * FOURIER_WIDTH + FOURIER_WIDTH]
        us = u_ref[:, j * 2 * FOURIER_WIDTH + FOURIER_WIDTH:(j + 1) * 2 * FOURIER_WIDTH]
        z = jnp.dot(w1, uc, preferred_element_type=F32) + jnp.dot(w2, us, preferred_element_type=F32)
        zr = z[:GRID_W]
        zi = z[GRID_W:]
        tc = jnp.concatenate([tc_ref[j]] * (FOURIER_WIDTH // LANE), axis=1)
        ts = jnp.concatenate([ts_ref[j]] * (FOURIER_WIDTH // LANE), axis=1)
        o_ref[j, :, :FOURIER_WIDTH] = (zr * tc - zi * ts).astype(BF16)
        o_ref[j, :, FOURIER_WIDTH:] = (zr * ts + zi * tc).astype(BF16)


def _four_lat_b_kernel(b_ref, c_ref, s_ref, o_ref):
    c = c_ref[...]
    s = s_ref[...]
    for j in range(FCH):
        br = b_ref[:, j * 2 * FOURIER_WIDTH:j * 2 * FOURIER_WIDTH + FOURIER_WIDTH]
        bi = b_ref[:, j * 2 * FOURIER_WIDTH + FOURIER_WIDTH:(j + 1) * 2 * FOURIER_WIDTH]
        o_ref[:, j, :] = (jnp.dot(c, br, preferred_element_type=F32) - jnp.dot(s, bi, preferred_element_type=F32))


def _fourier(u):
    c256, s256 = _dft_cos_sin(SEQ, SEQ ** -0.5)
    mixed_ctx = pl.pallas_call(
        _four_ctx_kernel,
        grid=(BATCH,),
        in_specs=[pl.BlockSpec((SEQ, 2 * FOURIER_WIDTH), lambda b: (b, 0)),
                  _const_spec((SEQ, SEQ)), _const_spec((SEQ, SEQ))],
        out_specs=pl.BlockSpec((SEQ, FOURIER_WIDTH), lambda b: (b, 0)),
        out_shape=jax.ShapeDtypeStruct((N_CTX, FOURIER_WIDTH), F32),
        compiler_params=_cparams(("arbitrary",)),
        name="fourier_ctx",
    )(u, jnp.asarray(c256, BF16), jnp.asarray(s256, BF16))

    g = GRID_W
    c64, s64 = _dft_cos_sin(g, g ** -0.5)
    w1 = jnp.asarray(np.concatenate([c64, s64], axis=0), BF16)
    w2 = jnp.asarray(np.concatenate([-s64, c64], axis=0), BF16)
    t2 = np.arange(g)[:, None]
    k1 = np.arange(g)[None, :]
    ang = 2.0 * np.pi * (t2 * k1) / (g * g)
    tw_c = jnp.asarray(np.broadcast_to(np.cos(ang)[:, :, None], (g, g, LANE)), F32)
    tw_s = jnp.asarray(np.broadcast_to(np.sin(ang)[:, :, None], (g, g, LANE)), F32)
    width = 2 * FOURIER_WIDTH
    u_lat = u[N_CTX:].reshape(DEC_BATCH, g, g * width)
    stage1 = pl.pallas_call(
        _four_lat_a_kernel,
        grid=(DEC_BATCH, g // FCH),
        in_specs=[pl.BlockSpec((None, g, FCH * width), lambda b, i: (b, 0, i)),
                  _const_spec((2 * g, g)), _const_spec((2 * g, g)),
                  pl.BlockSpec((FCH, g, LANE), lambda b, i: (i, 0, 0)),
                  pl.BlockSpec((FCH, g, LANE), lambda b, i: (i, 0, 0))],
        out_specs=pl.BlockSpec((None, FCH, g, width), lambda b, i: (b, i, 0, 0)),
        out_shape=jax.ShapeDtypeStruct((DEC_BATCH, g, g, width), BF16),
        compiler_params=_cparams(("arbitrary", "arbitrary")),
        name="fourier_lat_rows",
    )(u_lat, w1, w2, tw_c, tw_s)
    stage1 = stage1.reshape(DEC_BATCH, g, g * width)
    mixed_lat = pl.pallas_call(
        _four_lat_b_kernel,
        grid=(DEC_BATCH, g // FCH),
        in_specs=[pl.BlockSpec((None, g, FCH * width), lambda b, i: (b, 0, i)),
                  _const_spec((g, g)), _const_spec((g, g))],
        out_specs=pl.BlockSpec((None, g, FCH, FOURIER_WIDTH), lambda b, i: (b, 0, i, 0)),
        out_shape=jax.ShapeDtypeStruct((DEC_BATCH, g, g, FOURIER_WIDTH), F32),
        compiler_params=_cparams(("arbitrary", "arbitrary")),
        name="fourier_lat_cols",
    )(stage1, jnp.asarray(c64, BF16), jnp.asarray(s64, BF16))
    return mixed_ctx, mixed_lat.reshape(N_LAT, FOURIER_WIDTH)


def _pack_bf16_pairs(x):
    n = x.shape[1] // 2
    lo = pltpu.bitcast(x[:, :n].astype(BF16).astype(F32), U32)
    hi = pltpu.bitcast(x[:, n:].astype(BF16).astype(F32), U32)
    return (hi & jnp.uint32(0xFFFF0000)) | (lo >> 16)


def _unpack_bf16_pairs(w):
    lo = pltpu.bitcast(w << 16, F32)
    hi = pltpu.bitcast(w & jnp.uint32(0xFFFF0000), F32)
    return jnp.concatenate([lo, hi], axis=1)


def _route(logits, rb_ref, tri_ref, carry_ref):
    lt = logits.T
    score = [jax.nn.sigmoid(lt[e:e + 1, :]) for e in range(N_EXPERTS)]
    choice = [score[e] + rb_ref[e:e + 1, :] for e in range(N_EXPERTS)]
    best = jnp.zeros_like(score[0], dtype=jnp.int32)
    best_v = None
    for gi in range(N_EXPERT_GROUPS):
        c = choice[gi * EXPERTS_PER_GROUP:(gi + 1) * EXPERTS_PER_GROUP]
        top2 = None
        for a in range(EXPERTS_PER_GROUP):
            for b in range(a + 1, EXPERTS_PER_GROUP):
                pair = c[a] + c[b]
                top2 = pair if top2 is None else jnp.maximum(top2, pair)
        if best_v is None:
            best_v = top2
        else:
            better = top2 > best_v
            best = jnp.where(better, gi, best)
            best_v = jnp.where(better, top2, best_v)
    sel = []
    picked = []
    for e in range(N_EXPERTS):
        gi = e // EXPERTS_PER_GROUP
        rank = jnp.zeros_like(best)
        for o in range(gi * EXPERTS_PER_GROUP, (gi + 1) * EXPERTS_PER_GROUP):
            if o == e:
                continue
            ahead = (choice[o] > choice[e]) | ((choice[o] == choice[e]) & (o < e))
            rank = rank + ahead.astype(jnp.int32)
        chosen = (best == gi) & (rank < 2)
        sel.append(jnp.where(chosen, 1.0, 0.0))
        picked.append(jnp.where(chosen, score[e], 0.0))
    total = picked[0]
    for e in range(1, N_EXPERTS):
        total = total + picked[e]
    inv = 1.0 / total
    gate = [w * inv for w in picked]

    member = [sel[BUCKET_A[k]] * sel[BUCKET_B[k]] for k in range(N_BUCKETS)]
    bucket = member[1]
    wa = member[0] * gate[BUCKET_A[0]]
    wb = member[0] * gate[BUCKET_B[0]]
    for k in range(1, N_BUCKETS):
        if k > 1:
            bucket = bucket + float(k) * member[k]
        wa = wa + member[k] * gate[BUCKET_A[k]]
        wb = wb + member[k] * gate[BUCKET_B[k]]
    tm = bucket.shape[1]
    onehot = jnp.concatenate(member + [jnp.zeros((BUCKET_ROWS - N_BUCKETS, tm), F32)], axis=0)
    earlier = jnp.dot(onehot.astype(BF16), tri_ref[...], preferred_element_type=F32)
    carry = carry_ref[...]
    rank = jnp.sum(onehot * (earlier + carry[:, 0:1]), axis=0, keepdims=True)
    carry_ref[...] = carry + jnp.sum(onehot, axis=1, keepdims=True)
    return bucket.astype(I32), rank.astype(I32), wa, wb


SLAB = 32


def _row_slabs(n_rows, slab, body):
    for i in range(n_rows // slab):
        body(pl.ds(i * slab, slab))


def _residual_router(get_x, acc_ref, g1_ref, nf_ref, sc2_ref, sh2_ref, wr_ref, rb_ref, tri_ref,
                     x1_ref, rows_ref, meta_ref, count_ref, carry_ref, hhi_ref, hlo_ref):
    @pl.when(pl.program_id(0) == 0)
    def _():
        carry_ref[...] = jnp.zeros_like(carry_ref)

    g1 = g1_ref[...]
    gain = nf_ref[...] * (1.0 + sc2_ref[...])
    shift = sh2_ref[...]

    def slab(rows):
        x1 = get_x(rows) + g1 * acc_ref[rows, :]
        x1_ref[rows, :] = x1
        h2 = x1 * lax.rsqrt(jnp.mean(x1 * x1, axis=-1, keepdims=True) + EPS) * gain + shift
        hi = h2.astype(BF16)
        hi32 = hi.astype(F32)
        hhi_ref[rows, :] = hi
        hlo_ref[rows, :] = (h2 - hi32).astype(BF16)
        bits = pltpu.bitcast(hi32, U32)
        rows_ref[rows, :HALF] = (bits[:, HALF:] & jnp.uint32(0xFFFF0000)) | (bits[:, :HALF] >> 16)

    _row_slabs(acc_ref.shape[0], SLAB, slab)
    both = jnp.dot(hhi_ref[...], wr_ref[...], preferred_element_type=F32)
    logits = (both[:, :LANE] + both[:, LANE:]
              + jnp.dot(hlo_ref[...], wr_ref[:, :LANE], preferred_element_type=F32))
    bucket, rank, wa, wb = _route(logits, rb_ref, tri_ref, carry_ref)
    tm = logits.shape[0]
    gates_t = jnp.concatenate([wa, wb, jnp.zeros((LANE - 2, tm), F32)], axis=0)
    rows_ref[:, HALF:] = pltpu.bitcast(gates_t.T, U32)
    meta_ref[...] = jnp.concatenate([bucket, rank, jnp.zeros((6, tm), I32)], axis=0)
    count_ref[...] = carry_ref[...]


def _out0_kernel(attc_ref, attl_ref, mixc_ref, mixl_ref, wf_ref, woa_ref, wof_ref, xp_ref, xs_ref, g1_ref, nf_ref,
                 sc2_ref, sh2_ref, wr_ref, rb_ref, tri_ref, x1_ref, rows_ref, meta_ref, count_ref,
                 carry_ref, hhi_ref, hlo_ref, acc_ref):
    is_lat = pl.program_id(0) >= N_CTX_TILES
    att = jnp.where(is_lat, attl_ref[...], attc_ref[...])
    mix = jnp.where(is_lat, mixl_ref[...], mixc_ref[...])
    four = jnp.dot(mix.astype(BF16), wf_ref[...], preferred_element_type=F32)
    acc_ref[...] = (jnp.dot(att, woa_ref[...], preferred_element_type=F32)
                    + jnp.dot(four.astype(BF16), wof_ref[...], preferred_element_type=F32))

    def get_x(rows):
        return jnp.where(is_lat, xs_ref[rows, :], xp_ref[rows, :])

    _residual_router(get_x, acc_ref, g1_ref, nf_ref, sc2_ref, sh2_ref, wr_ref, rb_ref, tri_ref,
                     x1_ref, rows_ref, meta_ref, count_ref, carry_ref, hhi_ref, hlo_ref)


def _router_operands(w_router, router_bias):
    wr = jnp.zeros((D_MODEL, LANE), F32).at[:, :N_EXPERTS].set(w_router)
    wr_hi = wr.astype(BF16)
    wr_lo = (wr - wr_hi.astype(F32)).astype(BF16)
    rb = jnp.broadcast_to(router_bias.astype(F32)[:, None], (N_EXPERTS, TM))
    tri = jnp.asarray(np.triu(np.ones((TM, TM), np.float32), 1), BF16)
    return jnp.concatenate([wr_hi, wr_lo], axis=1), rb, tri


_EPILOGUE_OUT_SPECS = [_row_spec(D_MODEL), _row_spec(ROW_WORDS), pl.BlockSpec((8, TM), lambda i: (0, i)),
                       _const_spec((BUCKET_ROWS, LANE))]
_EPILOGUE_OUT_SHAPE = [jax.ShapeDtypeStruct((NT, D_MODEL), F32), jax.ShapeDtypeStruct((NT, ROW_WORDS), U32),
                       jax.ShapeDtypeStruct((8, NT), I32), jax.ShapeDtypeStruct((BUCKET_ROWS, LANE), F32)]
_EPILOGUE_SCRATCH = [pltpu.VMEM((BUCKET_ROWS, LANE), F32), pltpu.VMEM((TM, D_MODEL), BF16),
                     pltpu.VMEM((TM, D_MODEL), BF16), pltpu.VMEM((TM, D_MODEL), F32)]


def _epilogue_specs(layer):
    return [_mod_spec(layer, 2, TM), _const_spec((1, D_MODEL)), _mod_spec(layer, 4, TM),
            _mod_spec(layer, 3, TM), _const_spec((D_MODEL, 2 * LANE)), _const_spec((N_EXPERTS, TM)),
            _const_spec((TM, TM))]


def _out0(att, mixed, xp, xs, mods, w_fourier, w_out, norm_ffn0, router):
    wf = jnp.zeros((FOURIER_WIDTH, FOURIER_WIDTH), F32)
    for gi in range(N_FGROUPS):
        sl = slice(gi * FGROUP_DIM, (gi + 1) * FGROUP_DIM)
        wf = wf.at[sl, sl].set(w_fourier[gi])
    w_out = w_out.astype(BF16)
    wr, rb, tri = router
    return pl.pallas_call(
        _out0_kernel,
        grid=(NT // TM,),
        in_specs=_split_row_specs(ATT_WIDTH) + _split_row_specs(FOURIER_WIDTH)
        + [_const_spec((FOURIER_WIDTH, FOURIER_WIDTH)), _const_spec((ATT_WIDTH, D_MODEL)),
           _const_spec((FOURIER_WIDTH, D_MODEL))]
        + _split_row_specs(D_MODEL) + _epilogue_specs(0),
        out_specs=_EPILOGUE_OUT_SPECS,
        out_shape=_EPILOGUE_OUT_SHAPE,
        scratch_shapes=_EPILOGUE_SCRATCH,
        compiler_params=_cparams(("arbitrary",)),
        name="out_proj_att",
    )(att[0], att[1], mixed[0], mixed[1], wf.astype(BF16), w_out[:ATT_WIDTH], w_out[ATT_WIDTH:], xp, xs, mods,
      norm_ffn0.reshape(1, D_MODEL), mods, mods, wr, rb, tri)


def _dispatch_tables(meta, counts):
    bucket, rank = meta[0], meta[1]
    cnt = counts[:N_BUCKETS, 0].astype(I32)
    padded = (cnt + (TE - 1)) // TE * TE
    ends = jnp.cumsum(padded)
    starts = ends - padded
    kk = jnp.arange(N_BUCKETS, dtype=I32)
    pos = rank + jnp.sum(jnp.where(bucket[None, :] == kk[:, None], starts[:, None], 0), axis=0)
    tile0 = jnp.arange(N_ETILES, dtype=I32) * TE
    used = tile0 < ends[-1]
    tb = jnp.sum((tile0[:, None] >= ends[None, :]).astype(I32), axis=1)
    tb = jnp.where(used, tb, jnp.max(jnp.where(used, tb, 0)))
    pick = tb[:, None] == kk[None, :]

    def per_tile(table):
        return jnp.sum(jnp.where(pick, table[None, :], 0), axis=1)

    nrow = jnp.where(used, jnp.clip(per_tile(starts + cnt) - tile0, 0, TE), 0)
    pos3 = pos.reshape(SC_WORKERS, NT // (SC_WORKERS * SC_CHUNK), SC_CHUNK)
    return pos3, per_tile(jnp.asarray(BUCKET_A, I32)), per_tile(jnp.asarray(BUCKET_B, I32)), nrow


def _sc_permute(src, pos3, n_out, scatter):
    width = src.shape[1]
    _, n_chunks, chunk = pos3.shape
    rows_per_worker = n_chunks * chunk
    mesh = plsc.VectorSubcoreMesh(core_axis_name="c", subcore_axis_name="s")

    @functools.partial(pl.kernel, out_type=jax.ShapeDtypeStruct((n_out, width), src.dtype), mesh=mesh,
                       scratch_types=[pltpu.VMEM((n_chunks, chunk), I32), pltpu.VMEM((2, chunk, width), src.dtype),
                                      pltpu.SemaphoreType.DMA((2,))])
    def permute(src_hbm, pos_hbm, out_hbm, pos_v, buf, sem):
        worker = lax.axis_index("s") * SC_CORES + lax.axis_index("c")
        base = worker * rows_per_worker
        pltpu.sync_copy(pos_hbm.at[worker], pos_v)

        def own(j):
            return pl.ds(base + j * chunk, chunk)

        def load(j):
            rows = src_hbm.at[own(j)] if scatter else src_hbm.at[pos_v.at[j]]
            return pltpu.async_copy(rows, buf.at[j % 2], sem.at[j % 2])

        pending = load(0)
        for j in range(n_chunks):
            following = load(j + 1) if j + 1 < n_chunks else None
            pending.wait()
            pltpu.sync_copy(buf.at[j % 2], out_hbm.at[pos_v.at[j]] if scatter else out_hbm.at[own(j)])
            pending = following

    return permute(src, pos3)


def _experts_kernel(ea_ref, eb_ref, nrow_ref, rows_ref, wga_ref, wua_ref, wda_ref, wgb_ref, wub_ref, wdb_ref, y_ref,
                    cga, cua, cda, cgb, cub, cdb):
    j = pl.program_id(0)
    n = nrow_ref[j]

    @pl.when(n == 0)
    def _():
        y_ref[...] = jnp.zeros_like(y_ref)

    @pl.when(n > 0)
    def _():
        prev = jnp.maximum(j - 1, 0)

        @pl.when(jnp.logical_or(j == 0, ea_ref[j] != ea_ref[prev]))
        def _():
            cga[...] = wga_ref[...].astype(BF16)
            cua[...] = wua_ref[...].astype(BF16)
            cda[...] = wda_ref[...].astype(BF16)

        @pl.when(jnp.logical_or(j == 0, eb_ref[j] != eb_ref[prev]))
        def _():
            cgb[...] = wgb_ref[...].astype(BF16)
            cub[...] = wub_ref[...].astype(BF16)
            cdb[...] = wdb_ref[...].astype(BF16)

        valid = lax.broadcasted_iota(I32, (TE, 1), 0) < n
        h = jnp.where(valid, _unpack_bf16_pairs(rows_ref[:, :HALF]), 0.0).astype(BF16)
        gates = jnp.where(valid, pltpu.bitcast(rows_ref[:, HALF:], F32), 0.0)
        y = None
        for cg, cu, cd, col in ((cga, cua, cda, 0), (cgb, cub, cdb, 1)):
            a = jnp.dot(h, cg[...], preferred_element_type=F32)
            u = jnp.dot(h, cu[...], preferred_element_type=F32)
            hid = _silu(a) * u * gates[:, col:col + 1]
            o = jnp.dot(hid.astype(BF16), cd[...], preferred_element_type=F32)
            y = o if y is None else y + o
        y_ref[...] = _pack_bf16_pairs(y)


def _experts(rows_sorted, ea, eb, nrow, layer, w_gate, w_up, w_down):
    def w_spec(shape, which):
        return pl.BlockSpec((None, None) + shape, lambda j, ea, eb, nr: (layer, (ea, eb)[which][j], 0, 0))

    up_shape, down_shape = (D_MODEL, EXPERT_FF), (EXPERT_FF, D_MODEL)
    grid_spec = pltpu.PrefetchScalarGridSpec(
        num_scalar_prefetch=3,
        grid=(N_ETILES,),
        in_specs=[pl.BlockSpec((TE, ROW_WORDS), lambda j, *_: (j, 0)),
                  w_spec(up_shape, 0), w_spec(up_shape, 0), w_spec(down_shape, 0),
                  w_spec(up_shape, 1), w_spec(up_shape, 1), w_spec(down_shape, 1)],
        out_specs=pl.BlockSpec((TE, HALF), lambda j, *_: (j, 0)),
        scratch_shapes=[pltpu.VMEM(up_shape, BF16), pltpu.VMEM(up_shape, BF16), pltpu.VMEM(down_shape, BF16),
                        pltpu.VMEM(up_shape, BF16), pltpu.VMEM(up_shape, BF16), pltpu.VMEM(down_shape, BF16)])
    return pl.pallas_call(
        _experts_kernel,
        grid_spec=grid_spec,
        out_shape=jax.ShapeDtypeStruct((P_MAX, HALF), U32),
        compiler_params=_cparams(("arbitrary",)),
        name="experts_layer%d" % layer,
    )(ea, eb, nrow, rows_sorted, w_gate, w_up, w_down, w_gate, w_up, w_down)


def _moe(rows, meta, counts, layer, w_gate, w_up, w_down):
    pos3, ea, eb, nrow = _dispatch_tables(meta, counts)
    rows_sorted = _sc_permute(rows, pos3, P_MAX, scatter=True)
    y_sorted = _experts(rows_sorted, ea, eb, nrow, layer, w_gate, w_up, w_down)
    return _sc_permute(y_sorted, pos3, NT, scatter=False)


def _final_kernel(x_ref, y_ref, g2_ref, nfin_ref, yp_ref, ys_ref):
    i = pl.program_id(0)
    x = x_ref[...] + g2_ref[...] * _unpack_bf16_pairs(y_ref[...])
    out = x * lax.rsqrt(jnp.mean(x * x, axis=-1, keepdims=True) + EPS) * nfin_ref[...]

    @pl.when(i < N_CTX_TILES)
    def _():
        yp_ref[...] = out

    @pl.when(i >= N_CTX_TILES)
    def _():
        ys_ref[...] = out


def _final(x1, y_tok, mods, norm_final):
    return pl.pallas_call(
        _final_kernel,
        grid=(NT // TM,),
        in_specs=[_row_spec(D_MODEL), _row_spec(HALF), _mod_spec(DEPTH - 1, 5, TM), _const_spec((1, D_MODEL))],
        out_specs=_split_row_specs(D_MODEL),
        out_shape=[jax.ShapeDtypeStruct((N_CTX, D_MODEL), F32), jax.ShapeDtypeStruct((N_LAT, D_MODEL), F32)],
        compiler_params=_cparams(("arbitrary",)),
        name="final_norm",
    )(x1, y_tok, mods, norm_final.reshape(1, D_MODEL))


def _in1_kernel(x_ref, y_ref, g2_ref, nm_ref, sc_ref, sh_ref, w_ref, x2_ref, z_ref, xbc_ref, dt_ref, h_ref):
    g2 = g2_ref[...]
    gain = nm_ref[...] * (1.0 + sc_ref[...])
    shift = sh_ref[...]

    def norm_rows(rows):
        x = x_ref[rows, :] + g2 * _unpack_bf16_pairs(y_ref[rows, :])
        x2_ref[rows, :] = x
        h_ref[rows, :] = (x * lax.rsqrt(jnp.mean(x * x, axis=-1, keepdims=True) + EPS) * gain + shift).astype(BF16)

    _row_slabs(TM, SLAB, norm_rows)
    h = h_ref[...]
    step = 512
    nt_dims = (((1,), (1,)), ((), ()))

    def proj(c0, width):
        return lax.dot_general(h, w_ref[c0:c0 + width, :], nt_dims, preferred_element_type=F32)

    for c0 in range(0, SSM_INNER, step):
        z_ref[:, c0:c0 + step] = proj(c0, step).astype(BF16)
    for c0 in range(0, SSM_CONV_CH, step):
        xbc_ref[:, c0:c0 + step] = proj(SSM_INNER + c0, step).astype(BF16)
    n_dt = 2 * SSM_HEADS
    dt_ref[:, :n_dt] = proj(SSM_INNER + SSM_CONV_CH, n_dt)
    dt_ref[:, n_dt:] = jnp.zeros((TM, LANE - n_dt), F32)


def _in1(x1, y_tok, mods, norm_mix1, w_in):
    w = jnp.swapaxes(w_in, 0, 1).astype(BF16)
    return pl.pallas_call(
        _in1_kernel,
        grid=(NT // TM,),
        in_specs=[_row_spec(D_MODEL), _row_spec(HALF), _mod_spec(0, 5, TM), _const_spec((1, D_MODEL)),
                  _mod_spec(1, 1, TM), _mod_spec(1, 0, TM), _const_spec((ODD_IN, D_MODEL))],
        out_specs=[_row_spec(D_MODEL), _row_spec(SSM_INNER), _row_spec(SSM_CONV_CH), _row_spec(LANE)],
        out_shape=[jax.ShapeDtypeStruct((NT, D_MODEL), F32), jax.ShapeDtypeStruct((NT, SSM_INNER), BF16),
                   jax.ShapeDtypeStruct((NT, SSM_CONV_CH), BF16), jax.ShapeDtypeStruct((NT, LANE), F32)],
        scratch_shapes=[pltpu.VMEM((TM, D_MODEL), BF16)],
        compiler_params=_cparams(("arbitrary",)),
        name="in_proj_ssm",
    )(x1, y_tok, mods, norm_mix1.reshape(1, D_MODEL), mods, mods, w)


TCV = 256


def _conv_kernel(x_ref, prev_ref, next_ref, w_ref, b_ref, o_ref):
    i = pl.program_id(0)
    nct = N_CTX // TCV
    j = (i - nct) % (DEC_SEQ // TCV)
    first = jnp.logical_or(i < nct, j == 0)
    last = jnp.logical_or(i < nct, j == DEC_SEQ // TCV - 1)
    row = lax.broadcasted_iota(jnp.int32, (TCV, LANE), 0)
    top = row == 0
    bottom = row == TCV - 1
    for c in range(SSM_CONV_CH // LANE):
        cols = slice(c * LANE, (c + 1) * LANE)
        x = x_ref[:, cols].astype(F32)
        prev_row = jnp.where(first, 0.0, prev_ref[7:8, cols].astype(F32))
        next_row = jnp.where(last, 0.0, next_ref[0:1, cols].astype(F32))
        xm1 = jnp.where(top, prev_row, pltpu.roll(x, 1, 0))
        xp1 = jnp.where(bottom, next_row, pltpu.roll(x, TCV - 1, 0))
        y = xm1 * w_ref[0:1, cols] + x * w_ref[1:2, cols] + xp1 * w_ref[2:3, cols] + b_ref[:, cols]
        o_ref[:, cols] = _silu(y).astype(BF16)


def _conv(xbc, conv_w, conv_b):
    r8 = TCV // 8
    nblk8 = NT // 8
    return pl.pallas_call(
        _conv_kernel,
        grid=(NT // TCV,),
        in_specs=[_row_spec(SSM_CONV_CH, TCV),
                  pl.BlockSpec((8, SSM_CONV_CH), lambda i: (jnp.maximum(i * r8 - 1, 0), 0)),
                  pl.BlockSpec((8, SSM_CONV_CH), lambda i: (jnp.minimum((i + 1) * r8, nblk8 - 1), 0)),
                  _const_spec((3, SSM_CONV_CH)), _const_spec((1, SSM_CONV_CH))],
        out_specs=_row_spec(SSM_CONV_CH, TCV),
        out_shape=jax.ShapeDtypeStruct((NT, SSM_CONV_CH), BF16),
        compiler_params=_cparams(("arbitrary",)),
        name="ssm_conv",
    )(xbc, xbc, xbc, conv_w, conv_b.reshape(1, SSM_CONV_CH))


HPG = SSM_HEADS // SSM_GROUPS
GW = HPG * SSM_HEAD_DIM


def _ssd_kernel(*refs, reverse, has_init, write_state):
    refs = list(refs)
    xc_ref, dt_ref, bias_ref, alog_ref = refs[:4]
    refs = refs[4:]
    h0_ref = refs.pop(0) if has_init else None
    y_ref = refs.pop(0)
    hout_ref = refs.pop(0) if write_state else None
    (h_ref,) = refs
    c = pl.program_id(1)
    nc = pl.num_programs(1)
    col0 = SSM_HEADS if reverse else 0
    Q = SSM_CHUNK

    @pl.when(c == 0)
    def _():
        for g in range(SSM_GROUPS):
            if has_init:
                h_ref[g] = h0_ref[g * GW:(g + 1) * GW, :].T
            else:
                h_ref[g] = jnp.zeros((SSM_STATE, GW), F32)

    dt = jax.nn.softplus(dt_ref[...] + bias_ref[...])
    a = dt * -jnp.exp(alog_ref[...])
    row = lax.broadcasted_iota(jnp.int32, (Q, Q), 0)
    col = lax.broadcasted_iota(jnp.int32, (Q, Q), 1)
    keep = (col >= row) if reverse else (col <= row)
    tri = keep.astype(F32)
    acs = jnp.dot(tri, a, precision=HIGHEST, preferred_element_type=F32)
    edge = (0 if reverse else Q - 1)
    acs_end = acs[edge:edge + 1, :]
    log2e = math.log2(math.e)
    acs2 = acs * log2e
    src_t = (acs2 - jnp.log2(dt)).T
    out_t = (jnp.exp(acs_end - acs) * dt).T
    chunk_decay = jnp.exp(jnp.broadcast_to(acs_end, (8, LANE)))
    nt_dims = (((1,), (1,)), ((), ()))
    lane = lax.broadcasted_iota(jnp.int32, (Q, LANE), 1)
    low = lane < SSM_HEAD_DIM
    low8 = low[:8]

    def two_heads(v):
        zero = jnp.zeros_like(v)
        return jnp.concatenate([jnp.where(low, v, zero), jnp.where(low, zero, v)], axis=0)

    for g in range(SSM_GROUPS):
        bg = xc_ref[:, SSM_INNER + g * SSM_STATE:SSM_INNER + (g + 1) * SSM_STATE]
        cg = xc_ref[:, SSM_INNER + SSM_GROUPS * SSM_STATE + g * SSM_STATE:
                    SSM_INNER + SSM_GROUPS * SSM_STATE + (g + 1) * SSM_STATE]
        cb = lax.dot_general(cg, bg, nt_dims, preferred_element_type=F32).astype(BF16)
        bg_t = bg.astype(F32).T
        for j in range(HPG // 2):
            pair = slice(g * GW + j * LANE, g * GW + (j + 1) * LANE)
            mats, c_in, b_out, cdec = [], [], [], []
            for hh in (2 * j, 2 * j + 1):
                cidx = col0 + g * HPG + hh
                to_l = jnp.broadcast_to(acs2[:, cidx:cidx + 1], (Q, Q))
                lmat = jnp.where(keep, jnp.exp2(to_l - src_t[cidx:cidx + 1, :]), 0.0)
                mats.append(cb * lmat.astype(BF16))
                c_in.append(cg * jnp.exp2(to_l).astype(BF16))
                b_out.append((bg_t * out_t[cidx:cidx + 1, :]).astype(BF16))
                cdec.append(jnp.broadcast_to(chunk_decay[:, cidx:cidx + 1], (8, LANE)))
            hp = h_ref[g, :, j * LANE:(j + 1) * LANE]
            x2 = two_heads(xc_ref[:, pair])
            rhs = jnp.concatenate([x2, two_heads(hp.astype(BF16))], axis=0)
            y = jnp.dot(jnp.concatenate(mats + c_in, axis=1), rhs, preferred_element_type=F32)
            y_ref[:, pair] = y.astype(BF16)
            st = jnp.dot(jnp.concatenate(b_out, axis=1), x2, preferred_element_type=F32)
            h_ref[g, :, j * LANE:(j + 1) * LANE] = hp * jnp.where(low8, cdec[0], cdec[1])[0:1, :] + st

    if write_state:
        @pl.when(c == nc - 1)
        def _():
            for g in range(SSM_GROUPS):
                hout_ref[g * GW:(g + 1) * GW, :] = h_ref[g].T


def _ssd_direction(xc, dt_raw, bias128, alog128, h0, reverse):
    outs = []
    for (row0, nseq, seqlen, has_init, write_state) in ((0, BATCH, SEQ, False, True),
                                                         (N_CTX, DEC_BATCH, DEC_SEQ, True, False)):
        nc = seqlen // SSM_CHUNK
        base = row0 // SSM_CHUNK

        def chunk_idx(s, c, nc=nc, base=base):
            cc = (nc - 1 - c) if reverse else c
            return (base + s * nc + cc, 0)

        def out_idx(s, c, nc=nc):
            cc = (nc - 1 - c) if reverse else c
            return (s * nc + cc, 0)

        state_spec = pl.BlockSpec((None, SSM_INNER, SSM_STATE), lambda s, c: (s, 0, 0))
        in_specs = [pl.BlockSpec((SSM_CHUNK, SSM_CONV_CH), chunk_idx), pl.BlockSpec((SSM_CHUNK, LANE), chunk_idx),
                    _const_spec((1, LANE)), _const_spec((1, LANE))]
        args = [xc, dt_raw, bias128, alog128]
        if has_init:
            in_specs.append(state_spec)
            args.append(h0)
        out_specs = [pl.BlockSpec((SSM_CHUNK, SSM_INNER), out_idx)]
        out_shape = [jax.ShapeDtypeStruct((nseq * seqlen, SSM_INNER), BF16)]
        if write_state:
            out_specs.append(state_spec)
            out_shape.append(jax.ShapeDtypeStruct((nseq, SSM_INNER, SSM_STATE), F32))
        outs.append(pl.pallas_call(
            functools.partial(_ssd_kernel, reverse=reverse, has_init=has_init, write_state=write_state),
            grid=(nseq, nc),
            in_specs=in_specs,
            out_specs=out_specs,
            out_shape=out_shape,
            scratch_shapes=[pltpu.VMEM((SSM_GROUPS, SSM_STATE, GW), F32)],
            compiler_params=_cparams(("arbitrary", "arbitrary")),
            name="ssd_%s_%s" % ("bwd" if reverse else "fwd", "lat" if has_init else "ctx"),
        )(*args))
    return (outs[0][0], outs[1][0]), outs[0][1]


def _out1_kernel(yfc_ref, yfl_ref, ybc_ref, ybl_ref, xs_ref, z_ref, dskip_ref, ng_ref, wo_ref, x_ref, g1_ref, nf_ref,
                 sc2_ref, sh2_ref, wr_ref, rb_ref, tri_ref, x1_ref, rows_ref, meta_ref, count_ref,
                 carry_ref, hhi_ref, hlo_ref, acc_ref, y_ref):
    is_lat = pl.program_id(0) >= N_CTX_TILES
    dskip = dskip_ref[...]
    gain = ng_ref[...]

    def gate_norm(rows):
        yf = jnp.where(is_lat, yfl_ref[rows, :], yfc_ref[rows, :])
        yb = jnp.where(is_lat, ybl_ref[rows, :], ybc_ref[rows, :])
        y = yf.astype(F32) + yb.astype(F32) + dskip * xs_ref[rows, :].astype(F32)
        y = y * _silu(z_ref[rows, :].astype(F32))
        y_ref[rows, :] = (y * lax.rsqrt(jnp.mean(y * y, axis=-1, keepdims=True) + EPS) * gain).astype(BF16)

    _row_slabs(TM, SLAB // 2, gate_norm)
    acc_ref[...] = jnp.dot(y_ref[...], wo_ref[...], preferred_element_type=F32)
    _residual_router(lambda rows: x_ref[rows, :], acc_ref, g1_ref, nf_ref, sc2_ref, sh2_ref, wr_ref, rb_ref,
                     tri_ref, x1_ref, rows_ref, meta_ref, count_ref, carry_ref, hhi_ref, hlo_ref)


def _out1(yf, yb, xc, z, d_skip, ssm_norm, w_out, x, mods, norm_ffn1, router):
    wr, rb, tri = router
    return pl.pallas_call(
        _out1_kernel,
        grid=(NT // TM,),
        in_specs=_split_row_specs(SSM_INNER) + _split_row_specs(SSM_INNER)
        + [_row_spec(SSM_INNER), _row_spec(SSM_INNER),
           _const_spec((1, SSM_INNER)), _const_spec((1, SSM_INNER)), _const_spec((SSM_INNER, D_MODEL)),
           _row_spec(D_MODEL)] + _epilogue_specs(1),
        out_specs=_EPILOGUE_OUT_SPECS,
        out_shape=_EPILOGUE_OUT_SHAPE,
        scratch_shapes=_EPILOGUE_SCRATCH + [pltpu.VMEM((TM, SSM_INNER), BF16)],
        compiler_params=_cparams(("arbitrary",)),
        name="out_proj_ssm",
    )(yf[0], yf[1], yb[0], yb[1], xc, z, jnp.repeat(d_skip.astype(F32), SSM_HEAD_DIM).reshape(1, SSM_INNER),
      ssm_norm.reshape(1, SSM_INNER), w_out.astype(BF16), x, mods, norm_ffn1.reshape(1, D_MODEL), mods, mods,
      wr, rb, tri)


def _kv_from_lane_major(t):
    return t.reshape(BATCH, 1, N_KV_HEADS, HEAD_DIM, SEQ).transpose(0, 1, 4, 2, 3)


def kernel(x_prompt, x_sample, cache_k, cache_v, state_fwd, state_bwd, c, c_ctx, w_ada, b_ada, norm_mix, norm_ffn,
           w_in_att, q_gain, k_gain, w_fourier, w_out_att, w_in_ssm, conv_w, conv_b, dt_bias_f, dt_bias_b, a_log_f,
           a_log_b, d_skip, ssm_norm, w_out_ssm, w_router, router_bias, w_gate, w_up, w_down, norm_final):
    xp = x_prompt.reshape(N_CTX, D_MODEL)
    xs = x_sample.reshape(N_LAT, D_MODEL)
    cond8 = jnp.zeros((8, D_MODEL), F32).at[0].set(c_ctx).at[1:1 + DEC_BATCH].set(c)
    mods = _ada_mods(cond8, w_ada, b_ada)
    router = _router_operands(w_router, router_bias)
    cos2, sin2 = _rope_tables()

    q, kp, vp, u, new_k, new_v = _in0(xp, xs, mods, norm_mix[0], w_in_att[0], q_gain[0], k_gain[0], cos2, sin2)
    att = _attention(q, kp, vp, cache_k[:, 0], cache_v[:, 0])
    mixed = _fourier(u)
    x1, rows, meta, counts = _out0(att, mixed, xp, xs, mods, w_fourier[0], w_out_att[0], norm_ffn[0], router)
    y_tok = _moe(rows, meta, counts, 0, w_gate, w_up, w_down)

    x2, z, xbc, dt_raw = _in1(x1, y_tok, mods, norm_mix[1], w_in_ssm[0])
    xc = _conv(xbc, conv_w[0], conv_b[0])
    pad = jnp.zeros((LANE - 2 * SSM_HEADS,), F32)
    bias128 = jnp.concatenate([dt_bias_f[0], dt_bias_b[0], pad]).astype(F32).reshape(1, LANE)
    alog128 = jnp.concatenate([a_log_f[0], a_log_b[0], pad]).astype(F32).reshape(1, LANE)
    yf, sf = _ssd_direction(xc, dt_raw, bias128, alog128, state_fwd.reshape(DEC_BATCH, SSM_INNER, SSM_STATE), False)
    yb, sb = _ssd_direction(xc, dt_raw, bias128, alog128, state_bwd.reshape(DEC_BATCH, SSM_INNER, SSM_STATE), True)
    x3, rows, meta, counts = _out1(yf, yb, xc, z, d_skip[0], ssm_norm[0], w_out_ssm[0], x2, mods, norm_ffn[1], router)
    y_tok = _moe(rows, meta, counts, 1, w_gate, w_up, w_down)
    y_prompt, y_sample = _final(x3, y_tok, mods, norm_final)

    state_shape = (BATCH, 1, SSM_HEADS, SSM_HEAD_DIM, SSM_STATE)
    return (y_prompt.reshape(BATCH, SEQ, D_MODEL), y_sample.reshape(DEC_BATCH, DEC_SEQ, D_MODEL),
            _kv_from_lane_major(new_k), _kv_from_lane_major(new_v),
            sf.reshape(state_shape), sb.reshape(state_shape))
```

```python
import functools
import math

import numpy as np
import jax
import jax.numpy as jnp
from jax import lax
from jax.experimental import pallas as pl
from jax.experimental.pallas import tpu as pltpu
from jax.experimental.pallas import tpu_sc as plsc

F32 = jnp.float32
BF16 = jnp.bfloat16
U32 = jnp.uint32
I32 = jnp.int32
HIGHEST = lax.Precision.HIGHEST

D_MODEL = 1024
BATCH = 32
SEQ = 256
DEPTH = 2
DEC_BATCH = 2
DEC_SEQ = 4096
PAST_LEN = 256
GRID_W = 64
EPS = 1e-6
N_HEADS = 8
N_KV_HEADS = 2
HEAD_DIM = 64
ATT_WIDTH = N_HEADS * HEAD_DIM
KV_WIDTH = N_KV_HEADS * HEAD_DIM
ROPE_THETA = 10000.0
N_FGROUPS = 8
FGROUP_DIM = 64
FOURIER_WIDTH = N_FGROUPS * FGROUP_DIM
EVEN_IN = ATT_WIDTH + 2 * KV_WIDTH + FOURIER_WIDTH
SSM_INNER = 2 * D_MODEL
SSM_HEAD_DIM = 64
SSM_HEADS = SSM_INNER // SSM_HEAD_DIM
SSM_GROUPS = 4
SSM_STATE = 128
SSM_CHUNK = 128
SSM_CONV_CH = SSM_INNER + 2 * SSM_GROUPS * SSM_STATE
ODD_IN = SSM_INNER + SSM_CONV_CH + 2 * SSM_HEADS
N_EXPERTS = 16
EXPERTS_PER_GROUP = 4
N_EXPERT_GROUPS = 4
EXPERT_FF = 512

N_CTX = BATCH * SEQ
N_LAT = DEC_BATCH * DEC_SEQ
NT = N_CTX + N_LAT
N_SEG = 1 + DEC_BATCH
LANE = 128
VMEM_LIMIT = 56 * 1024 * 1024

TM = 512
N_CTX_TILES = N_CTX // TM
LAT_TILES_PER_SEQ = DEC_SEQ // TM

PAIR_ORDER = ((0, 1), (0, 2), (0, 3), (1, 3), (1, 2), (3, 2))
N_BUCKETS = N_EXPERT_GROUPS * len(PAIR_ORDER)
BUCKET_A = tuple(g * EXPERTS_PER_GROUP + a for g in range(N_EXPERT_GROUPS) for a, _ in PAIR_ORDER)
BUCKET_B = tuple(g * EXPERTS_PER_GROUP + b for g in range(N_EXPERT_GROUPS) for _, b in PAIR_ORDER)
BUCKET_ROWS = 32
TE = 256
N_ETILES = (NT + N_BUCKETS * (TE - 1) + TE - 1) // TE
P_MAX = N_ETILES * TE
HALF = D_MODEL // 2
ROW_WORDS = HALF + LANE
SC_CORES = 2
SC_SUBCORES = 16
SC_WORKERS = SC_CORES * SC_SUBCORES
SC_CHUNK = 64


def _cparams(sem):
    return pltpu.CompilerParams(dimension_semantics=sem, vmem_limit_bytes=VMEM_LIMIT)


def _seg_of_tile(i, tm):
    nct = N_CTX // tm
    return jnp.where(i < nct, 0, 1 + (i - nct) // (DEC_SEQ // tm))


def _mod_spec(layer, which, tm):
    return pl.BlockSpec((None, None, None, 1, D_MODEL),
                        lambda i, *_: (layer, which, _seg_of_tile(i, tm), 0, 0))


def _row_spec(width, tm=TM):
    return pl.BlockSpec((tm, width), lambda i, *_: (i, 0))


def _const_spec(shape):
    nd = len(shape)
    return pl.BlockSpec(shape, lambda *_: (0,) * nd)


def _silu(x):
    return x * jax.nn.sigmoid(x)


def _ada_kernel(cond_ref, w_ref, b_ref, o_ref):
    c = cond_ref[...]
    o_ref[...] = jnp.dot(_silu(c), w_ref[...], precision=HIGHEST, preferred_element_type=F32) + b_ref[...]


def _ada_mods(cond8, w_ada, b_ada):
    tn = 1536
    out = pl.pallas_call(
        _ada_kernel,
        grid=(DEPTH, 6 * D_MODEL // tn),
        in_specs=[pl.BlockSpec((8, D_MODEL), lambda l, n: (0, 0)),
                  pl.BlockSpec((None, D_MODEL, tn), lambda l, n: (l, 0, n)),
                  pl.BlockSpec((None, 1, tn), lambda l, n: (l, 0, n))],
        out_specs=pl.BlockSpec((None, 8, tn), lambda l, n: (l, 0, n)),
        out_shape=jax.ShapeDtypeStruct((DEPTH, 8, 6 * D_MODEL), F32),
        compiler_params=_cparams(("arbitrary", "arbitrary")),
        name="ada_mod",
    )(cond8, w_ada, b_ada.reshape(DEPTH, 1, 6 * D_MODEL))
    return out.reshape(DEPTH, 8, 6, D_MODEL)[:, :N_SEG].transpose(0, 2, 1, 3)[:, :, :, None, :]


def _rope_tables():
    t = np.arange(DEC_SEQ)
    row = (t // GRID_W).astype(np.float64)
    col = (t % GRID_W).astype(np.float64)
    axis_dim = HEAD_DIM // 2
    freqs = ROPE_THETA ** (-np.arange(0, axis_dim, 2, dtype=np.float64) / axis_dim)
    ang = np.concatenate([row[:, None] * freqs, col[:, None] * freqs], axis=-1)
    cos = np.repeat(np.cos(ang), 2, axis=1)
    sin = np.repeat(np.sin(ang), 2, axis=1)
    sign = np.where(np.arange(HEAD_DIM) % 2 == 0, -1.0, 1.0)
    cos2 = np.tile(cos, (1, 2)).astype(np.float32)
    sin2 = np.tile(sin * sign, (1, 2)).astype(np.float32)
    return jnp.asarray(cos2), jnp.asarray(sin2)


def _dft_cos_sin(n, scale):
    k = np.arange(n)
    ang = 2.0 * np.pi * ((k[:, None] * k[None, :]) % n) / n
    return np.cos(ang) * scale, np.sin(ang) * scale


def _block_diag(m, reps):
    n = m.shape[0]
    out = np.zeros((n * reps, n * reps), m.dtype)
    for r in range(reps):
        out[r * n:(r + 1) * n, r * n:(r + 1) * n] = m
    return out


def _channel_dft():
    c, s = _dft_cos_sin(FGROUP_DIM, FGROUP_DIM ** -0.5)
    return jnp.asarray(np.concatenate([_block_diag(c, N_FGROUPS), _block_diag(s, N_FGROUPS)], axis=1), BF16)


def _group_ones(width):
    return jnp.asarray(_block_diag(np.ones((HEAD_DIM, HEAD_DIM), np.float32), width // HEAD_DIM), BF16)


def _pad_heads(x):
    lane = lax.broadcasted_iota(jnp.int32, x.shape, 1)
    low = lane < HEAD_DIM
    xr = pltpu.roll(x, HEAD_DIM, 1)
    zero = jnp.zeros_like(x)
    return [jnp.where(low, x, zero), jnp.where(low, zero, xr), jnp.where(low, xr, zero), jnp.where(low, zero, x)]


def _ctx_tile(i, *_):
    return (jnp.minimum(i, N_CTX_TILES - 1), 0)


def _lat_tile(i, *_):
    return (jnp.maximum(i - N_CTX_TILES, 0), 0)


def _split_row_specs(width):
    return [pl.BlockSpec((TM, width), _ctx_tile), pl.BlockSpec((TM, width), _lat_tile)]


def _in0_kernel(xp_ref, xs_ref, nm_ref, sc_ref, sh_ref, w_ref, qg_ref, kg_ref, ones_ref, cos_ref, sin_ref, dft_ref,
                q_ref, kp_ref, vp_ref, u_ref, nk_ref, nv_ref, h_ref):
    i = pl.program_id(0)
    is_lat = i >= N_CTX_TILES
    gain = nm_ref[...] * (1.0 + sc_ref[...])
    shift = sh_ref[...]

    def norm_rows(rows):
        x = jnp.where(is_lat, xs_ref[rows, :], xp_ref[rows, :])
        h_ref[rows, :] = (x * lax.rsqrt(jnp.mean(x * x, axis=-1, keepdims=True) + EPS) * gain + shift).astype(BF16)

    _row_slabs(TM, SLAB, norm_rows)
    p = jnp.dot(h_ref[...], w_ref[...], preferred_element_type=F32)
    q = p[:, :ATT_WIDTH]
    k = p[:, ATT_WIDTH:ATT_WIDTH + KV_WIDTH]
    v = p[:, ATT_WIDTH + KV_WIDTH:ATT_WIDTH + 2 * KV_WIDTH]
    f = p[:, ATT_WIDTH + 2 * KV_WIDTH:]
    ones = ones_ref[...]
    qss = jnp.dot((q * q).astype(BF16), ones, preferred_element_type=F32)
    kss = jnp.dot((k * k).astype(BF16), ones[:KV_WIDTH, :KV_WIDTH], preferred_element_type=F32)
    qn = q * lax.rsqrt(qss * (1.0 / HEAD_DIM) + EPS) * qg_ref[...]
    kn = k * lax.rsqrt(kss * (1.0 / HEAD_DIM) + EPS) * kg_ref[...]

    cos = jnp.where(is_lat, cos_ref[...], 1.0)
    sin = jnp.where(is_lat, sin_ref[...], 0.0)
    lane = lax.broadcasted_iota(jnp.int32, (TM, LANE), 1)
    even = (lane % 2) == 0

    def rope(xc):
        swapped = jnp.where(even, pltpu.roll(xc, LANE - 1, 1), pltpu.roll(xc, 1, 1))
        return xc * cos + swapped * sin

    scale = HEAD_DIM ** -0.5 * math.log2(math.e)
    for j in range(ATT_WIDTH // LANE):
        q_ref[:, j * LANE:(j + 1) * LANE] = (rope(qn[:, j * LANE:(j + 1) * LANE]) * scale).astype(BF16)
    for j, c in enumerate(_pad_heads(rope(kn))):
        kp_ref[:, j * LANE:(j + 1) * LANE] = c.astype(BF16)
    for j, c in enumerate(_pad_heads(v)):
        vp_ref[:, j * LANE:(j + 1) * LANE] = c.astype(BF16)
    u_ref[...] = jnp.dot(f.astype(BF16), dft_ref[...], preferred_element_type=F32).astype(BF16)

    @pl.when(jnp.logical_not(is_lat))
    def _():
        for r in range(TM // SEQ):
            nk_ref[r] = kn[r * SEQ:(r + 1) * SEQ, :].T
            nv_ref[r] = v[r * SEQ:(r + 1) * SEQ, :].T


def _in0(xp, xs, mods, norm_mix0, w_in, q_gain, k_gain, cos2, sin2):
    def table_idx(i):
        return (jnp.where(i < N_CTX_TILES, 0, (i - N_CTX_TILES) % LAT_TILES_PER_SEQ), 0)

    def ctx_idx(i):
        return (jnp.minimum(i, N_CTX_TILES - 1), 0, 0)

    seqs = TM // SEQ
    outs = pl.pallas_call(
        _in0_kernel,
        grid=(NT // TM,),
        in_specs=_split_row_specs(D_MODEL) + [_const_spec((1, D_MODEL)), _mod_spec(0, 1, TM), _mod_spec(0, 0, TM),
                  _const_spec((D_MODEL, EVEN_IN)), _const_spec((1, ATT_WIDTH)), _const_spec((1, KV_WIDTH)),
                  _const_spec((ATT_WIDTH, ATT_WIDTH)),
                  pl.BlockSpec((TM, LANE), table_idx), pl.BlockSpec((TM, LANE), table_idx),
                  _const_spec((FOURIER_WIDTH, 2 * FOURIER_WIDTH))],
        out_specs=[_row_spec(ATT_WIDTH), _row_spec(4 * LANE), _row_spec(4 * LANE), _row_spec(2 * FOURIER_WIDTH),
                   pl.BlockSpec((seqs, KV_WIDTH, SEQ), ctx_idx), pl.BlockSpec((seqs, KV_WIDTH, SEQ), ctx_idx)],
        out_shape=[jax.ShapeDtypeStruct((NT, ATT_WIDTH), BF16), jax.ShapeDtypeStruct((NT, 4 * LANE), BF16),
                   jax.ShapeDtypeStruct((NT, 4 * LANE), BF16), jax.ShapeDtypeStruct((NT, 2 * FOURIER_WIDTH), BF16),
                   jax.ShapeDtypeStruct((BATCH, KV_WIDTH, SEQ), F32), jax.ShapeDtypeStruct((BATCH, KV_WIDTH, SEQ), F32)],
        scratch_shapes=[pltpu.VMEM((TM, D_MODEL), BF16)],
        compiler_params=_cparams(("arbitrary",)),
        name="in_proj_att",
    )(xp, xs, norm_mix0.reshape(1, D_MODEL), mods, mods, w_in.astype(BF16),
      jnp.tile(q_gain, N_HEADS).reshape(1, ATT_WIDTH), jnp.tile(k_gain, N_KV_HEADS).reshape(1, KV_WIDTH),
      _group_ones(ATT_WIDTH), cos2, sin2, _channel_dft())
    return outs


def _att_kernel(*refs, has_cache):
    if has_cache:
        q_ref, kp_ref, vp_ref, ck_ref, cv_ref, o_ref = refs
        ckp = [c.astype(BF16) for c in _pad_heads(ck_ref[...])]
        cvp = [c.astype(BF16) for c in _pad_heads(cv_ref[...])]
    else:
        q_ref, kp_ref, vp_ref, o_ref = refs
    nt_dims = (((1,), (1,)), ((), ()))
    for j in range(ATT_WIDTH // LANE):
        qj = q_ref[:, j * LANE:(j + 1) * LANE]
        g = j // 2
        acc = None
        for half in range(2):
            c = 2 * g + half
            kk = kp_ref[:, c * LANE:(c + 1) * LANE]
            vv = vp_ref[:, c * LANE:(c + 1) * LANE]
            s = lax.dot_general(qj, kk, nt_dims, preferred_element_type=F32)
            m = jnp.max(s, axis=-1, keepdims=True)
            if has_cache:
                sc = lax.dot_general(qj, ckp[c], nt_dims, preferred_element_type=F32)
                m = jnp.maximum(m, jnp.max(sc, axis=-1, keepdims=True))
            p = jnp.exp2(s - m)
            d = jnp.sum(p, axis=-1, keepdims=True)
            o = jnp.dot(p.astype(BF16), vv, preferred_element_type=F32)
            if has_cache:
                pc = jnp.exp2(sc - m)
                d = d + jnp.sum(pc, axis=-1, keepdims=True)
                o = o + jnp.dot(pc.astype(BF16), cvp[c], preferred_element_type=F32)
            o = o * (1.0 / d)
            acc = o if acc is None else acc + o
        o_ref[:, j * LANE:(j + 1) * LANE] = acc.astype(BF16)


def _attention(q, kp, vp, cache_k, cache_v):
    att_ctx = pl.pallas_call(
        functools.partial(_att_kernel, has_cache=False),
        grid=(BATCH,),
        in_specs=[pl.BlockSpec((SEQ, ATT_WIDTH), lambda b: (b, 0)),
                  pl.BlockSpec((SEQ, 4 * LANE), lambda b: (b, 0)),
                  pl.BlockSpec((SEQ, 4 * LANE), lambda b: (b, 0))],
        out_specs=pl.BlockSpec((SEQ, ATT_WIDTH), lambda b: (b, 0)),
        out_shape=jax.ShapeDtypeStruct((N_CTX, ATT_WIDTH), BF16),
        compiler_params=_cparams(("arbitrary",)),
        name="attention_ctx",
    )(q, kp, vp)
    tq = 256
    off = N_CTX // DEC_SEQ
    att_lat = pl.pallas_call(
        functools.partial(_att_kernel, has_cache=True),
        grid=(DEC_BATCH, DEC_SEQ // tq),
        in_specs=[pl.BlockSpec((tq, ATT_WIDTH), lambda b, i: (N_CTX // tq + b * (DEC_SEQ // tq) + i, 0)),
                  pl.BlockSpec((DEC_SEQ, 4 * LANE), lambda b, i: (off + b, 0)),
                  pl.BlockSpec((DEC_SEQ, 4 * LANE), lambda b, i: (off + b, 0)),
                  pl.BlockSpec((None, PAST_LEN, KV_WIDTH), lambda b, i: (b, 0, 0)),
                  pl.BlockSpec((None, PAST_LEN, KV_WIDTH), lambda b, i: (b, 0, 0))],
        out_specs=pl.BlockSpec((tq, ATT_WIDTH), lambda b, i: (b * (DEC_SEQ // tq) + i, 0)),
        out_shape=jax.ShapeDtypeStruct((N_LAT, ATT_WIDTH), BF16),
        compiler_params=_cparams(("arbitrary", "arbitrary")),
        name="attention_lat",
    )(q, kp, vp, cache_k.reshape(DEC_BATCH, PAST_LEN, KV_WIDTH), cache_v.reshape(DEC_BATCH, PAST_LEN, KV_WIDTH))
    return att_ctx, att_lat


def _four_ctx_kernel(u_ref, c_ref, s_ref, o_ref):
    uc = u_ref[:, :FOURIER_WIDTH]
    us = u_ref[:, FOURIER_WIDTH:]
    o_ref[...] = (jnp.dot(c_ref[...], uc, preferred_element_type=F32)
                  - jnp.dot(s_ref[...], us, preferred_element_type=F32))


FCH = 8


def _four_lat_a_kernel(u_ref, w1_ref, w2_ref, tc_ref, ts_ref, o_ref):
    w1 = w1_ref[...]
    w2 = w2_ref[...]
    for j in range(FCH):
        uc = u_ref[:, j * 2 * FOURIER_WIDTH:j * 2 * FOURIER_WIDTH + FOURIER_WIDTH]
        us = u_ref[:, j * 2 * FOURIER_WIDTH + FOURIER_WIDTH:(j + 1) * 2 * FOURIER_WIDTH]
        z = jnp.dot(w1, uc, preferred_element_type=F32) + jnp.dot(w2, us, preferred_element_type=F32)
        zr = z[:GRID_W]
        zi = z[GRID_W:]
        tc = jnp.concatenate([tc_ref[j]] * (FOURIER_WIDTH // LANE), axis=1)
        ts = jnp.concatenate([ts_ref[j]] * (FOURIER_WIDTH // LANE), axis=1)
        o_ref[j, :, :FOURIER_WIDTH] = (zr * tc - zi * ts).astype(BF16)
        o_ref[j, :, FOURIER_WIDTH:] = (zr * ts + zi * tc).astype(BF16)


def _four_lat_b_kernel(b_ref, c_ref, s_ref, o_ref):
    c = c_ref[...]
    s = s_ref[...]
    for j in range(FCH):
        br = b_ref[:, j * 2 * FOURIER_WIDTH:j * 2 * FOURIER_WIDTH + FOURIER_WIDTH]
        bi = b_ref[:, j * 2 * FOURIER_WIDTH + FOURIER_WIDTH:(j + 1) * 2 * FOURIER_WIDTH]
        o_ref[:, j, :] = (jnp.dot(c, br, preferred_element_type=F32) - jnp.dot(s, bi, preferred_element_type=F32))


def _fourier(u):
    c256, s256 = _dft_cos_sin(SEQ, SEQ ** -0.5)
    mixed_ctx = pl.pallas_call(
        _four_ctx_kernel,
        grid=(BATCH,),
        in_specs=[pl.BlockSpec((SEQ, 2 * FOURIER_WIDTH), lambda b: (b, 0)),
                  _const_spec((SEQ, SEQ)), _const_spec((SEQ, SEQ))],
        out_specs=pl.BlockSpec((SEQ, FOURIER_WIDTH), lambda b: (b, 0)),
        out_shape=jax.ShapeDtypeStruct((N_CTX, FOURIER_WIDTH), F32),
        compiler_params=_cparams(("arbitrary",)),
        name="fourier_ctx",
    )(u, jnp.asarray(c256, BF16), jnp.asarray(s256, BF16))

    g = GRID_W
    c64, s64 = _dft_cos_sin(g, g ** -0.5)
    w1 = jnp.asarray(np.concatenate([c64, s64], axis=0), BF16)
    w2 = jnp.asarray(np.concatenate([-s64, c64], axis=0), BF16)
    t2 = np.arange(g)[:, None]
    k1 = np.arange(g)[None, :]
    ang = 2.0 * np.pi * (t2 * k1) / (g * g)
    tw_c = jnp.asarray(np.broadcast_to(np.cos(ang)[:, :, None], (g, g, LANE)), F32)
    tw_s = jnp.asarray(np.broadcast_to(np.sin(ang)[:, :, None], (g, g, LANE)), F32)
    width = 2 * FOURIER_WIDTH
    u_lat = u[N_CTX:].reshape(DEC_BATCH, g, g * width)
    stage1 = pl.pallas_call(
        _four_lat_a_kernel,
        grid=(DEC_BATCH, g // FCH),
        in_specs=[pl.BlockSpec((None, g, FCH * width), lambda b, i: (b, 0, i)),
                  _const_spec((2 * g, g)), _const_spec((2 * g, g)),
                  pl.BlockSpec((FCH, g, LANE), lambda b, i: (i, 0, 0)),
                  pl.BlockSpec((FCH, g, LANE), lambda b, i: (i, 0, 0))],
        out_specs=pl.BlockSpec((None, FCH, g, width), lambda b, i: (b, i, 0, 0)),
        out_shape=jax.ShapeDtypeStruct((DEC_BATCH, g, g, width), BF16),
        compiler_params=_cparams(("arbitrary", "arbitrary")),
        name="fourier_lat_rows",
    )(u_lat, w1, w2, tw_c, tw_s)
    stage1 = stage1.reshape(DEC_BATCH, g, g * width)
    mixed_lat = pl.pallas_call(
        _four_lat_b_kernel,
        grid=(DEC_BATCH, g // FCH),
        in_specs=[pl.BlockSpec((None, g, FCH * width), lambda b, i: (b, 0, i)),
                  _const_spec((g, g)), _const_spec((g, g))],
        out_specs=pl.BlockSpec((None, g, FCH, FOURIER_WIDTH), lambda b, i: (b, 0, i, 0)),
        out_shape=jax.ShapeDtypeStruct((DEC_BATCH, g, g, FOURIER_WIDTH), F32),
        compiler_params=_cparams(("arbitrary", "arbitrary")),
        name="fourier_lat_cols",
    )(stage1, jnp.asarray(c64, BF16), jnp.asarray(s64, BF16))
    return mixed_ctx, mixed_lat.reshape(N_LAT, FOURIER_WIDTH)


def _pack_bf16_pairs(x):
    n = x.shape[1] // 2
    lo = pltpu.bitcast(x[:, :n].astype(BF16).astype(F32), U32)
    hi = pltpu.bitcast(x[:, n:].astype(BF16).astype(F32), U32)
    return (hi & jnp.uint32(0xFFFF0000)) | (lo >> 16)


def _unpack_bf16_pairs(w):
    lo = pltpu.bitcast(w << 16, F32)
    hi = pltpu.bitcast(w & jnp.uint32(0xFFFF0000), F32)
    return jnp.concatenate([lo, hi], axis=1)


def _route(logits, rb_ref, tri_ref, carry_ref):
    lt = logits.T
    score = [jax.nn.sigmoid(lt[e:e + 1, :]) for e in range(N_EXPERTS)]
    choice = [score[e] + rb_ref[e:e + 1, :] for e in range(N_EXPERTS)]
    best = jnp.zeros_like(score[0], dtype=jnp.int32)
    best_v = None
    for gi in range(N_EXPERT_GROUPS):
        c = choice[gi * EXPERTS_PER_GROUP:(gi + 1) * EXPERTS_PER_GROUP]
        top2 = None
        for a in range(EXPERTS_PER_GROUP):
            for b in range(a + 1, EXPERTS_PER_GROUP):
                pair = c[a] + c[b]
                top2 = pair if top2 is None else jnp.maximum(top2, pair)
        if best_v is None:
            best_v = top2
        else:
            better = top2 > best_v
            best = jnp.where(better, gi, best)
            best_v = jnp.where(better, top2, best_v)
    sel = []
    picked = []
    for e in range(N_EXPERTS):
        gi = e // EXPERTS_PER_GROUP
        rank = jnp.zeros_like(best)
        for o in range(gi * EXPERTS_PER_GROUP, (gi + 1) * EXPERTS_PER_GROUP):
            if o == e:
                continue
            ahead = (choice[o] > choice[e]) | ((choice[o] == choice[e]) & (o < e))
            rank = rank + ahead.astype(jnp.int32)
        chosen = (best == gi) & (rank < 2)
        sel.append(jnp.where(chosen, 1.0, 0.0))
        picked.append(jnp.where(chosen, score[e], 0.0))
    total = picked[0]
    for e in range(1, N_EXPERTS):
        total = total + picked[e]
    inv = 1.0 / total
    gate = [w * inv for w in picked]

    member = [sel[BUCKET_A[k]] * sel[BUCKET_B[k]] for k in range(N_BUCKETS)]
    bucket = member[1]
    wa = member[0] * gate[BUCKET_A[0]]
    wb = member[0] * gate[BUCKET_B[0]]
    for k in range(1, N_BUCKETS):
        if k > 1:
            bucket = bucket + float(k) * member[k]
        wa = wa + member[k] * gate[BUCKET_A[k]]
        wb = wb + member[k] * gate[BUCKET_B[k]]
    tm = bucket.shape[1]
    onehot = jnp.concatenate(member + [jnp.zeros((BUCKET_ROWS - N_BUCKETS, tm), F32)], axis=0)
    earlier = jnp.dot(onehot.astype(BF16), tri_ref[...], preferred_element_type=F32)
    carry = carry_ref[...]
    rank = jnp.sum(onehot * (earlier + carry[:, 0:1]), axis=0, keepdims=True)
    carry_ref[...] = carry + jnp.sum(onehot, axis=1, keepdims=True)
    return bucket.astype(I32), rank.astype(I32), wa, wb


SLAB = 32


def _row_slabs(n_rows, slab, body):
    for i in range(n_rows // slab):
        body(pl.ds(i * slab, slab))


def _residual_router(get_x, acc_ref, g1_ref, nf_ref, sc2_ref, sh2_ref, wr_ref, rb_ref, tri_ref,
                     x1_ref, rows_ref, meta_ref, count_ref, carry_ref, hhi_ref, hlo_ref):
    @pl.when(pl.program_id(0) == 0)
    def _():
        carry_ref[...] = jnp.zeros_like(carry_ref)

    g1 = g1_ref[...]
    gain = nf_ref[...] * (1.0 + sc2_ref[...])
    shift = sh2_ref[...]

    def slab(rows):
        x1 = get_x(rows) + g1 * acc_ref[rows, :]
        x1_ref[rows, :] = x1
        h2 = x1 * lax.rsqrt(jnp.mean(x1 * x1, axis=-1, keepdims=True) + EPS) * gain + shift
        hi = h2.astype(BF16)
        hi32 = hi.astype(F32)
        hhi_ref[rows, :] = hi
        hlo_ref[rows, :] = (h2 - hi32).astype(BF16)
        bits = pltpu.bitcast(hi32, U32)
        rows_ref[rows, :HALF] = (bits[:, HALF:] & jnp.uint32(0xFFFF0000)) | (bits[:, :HALF] >> 16)

    _row_slabs(acc_ref.shape[0], SLAB, slab)
    both = jnp.dot(hhi_ref[...], wr_ref[...], preferred_element_type=F32)
    logits = (both[:, :LANE] + both[:, LANE:]
              + jnp.dot(hlo_ref[...], wr_ref[:, :LANE], preferred_element_type=F32))
    bucket, rank, wa, wb = _route(logits, rb_ref, tri_ref, carry_ref)
    tm = logits.shape[0]
    gates_t = jnp.concatenate([wa, wb, jnp.zeros((LANE - 2, tm), F32)], axis=0)
    rows_ref[:, HALF:] = pltpu.bitcast(gates_t.T, U32)
    meta_ref[...] = jnp.concatenate([bucket, rank, jnp.zeros((6, tm), I32)], axis=0)
    count_ref[...] = carry_ref[...]


def _out0_kernel(attc_ref, attl_ref, mixc_ref, mixl_ref, wf_ref, woa_ref, wof_ref, xp_ref, xs_ref, g1_ref, nf_ref,
                 sc2_ref, sh2_ref, wr_ref, rb_ref, tri_ref, x1_ref, rows_ref, meta_ref, count_ref,
                 carry_ref, hhi_ref, hlo_ref, acc_ref):
    is_lat = pl.program_id(0) >= N_CTX_TILES
    att = jnp.where(is_lat, attl_ref[...], attc_ref[...])
    mix = jnp.where(is_lat, mixl_ref[...], mixc_ref[...])
    four = jnp.dot(mix.astype(BF16), wf_ref[...], preferred_element_type=F32)
    acc_ref[...] = (jnp.dot(att, woa_ref[...], preferred_element_type=F32)
                    + jnp.dot(four.astype(BF16), wof_ref[...], preferred_element_type=F32))

    def get_x(rows):
        return jnp.where(is_lat, xs_ref[rows, :], xp_ref[rows, :])

    _residual_router(get_x, acc_ref, g1_ref, nf_ref, sc2_ref, sh2_ref, wr_ref, rb_ref, tri_ref,
                     x1_ref, rows_ref, meta_ref, count_ref, carry_ref, hhi_ref, hlo_ref)


def _router_operands(w_router, router_bias):
    wr = jnp.zeros((D_MODEL, LANE), F32).at[:, :N_EXPERTS].set(w_router)
    wr_hi = wr.astype(BF16)
    wr_lo = (wr - wr_hi.astype(F32)).astype(BF16)
    rb = jnp.broadcast_to(router_bias.astype(F32)[:, None], (N_EXPERTS, TM))
    tri = jnp.asarray(np.triu(np.ones((TM, TM), np.float32), 1), BF16)
    return jnp.concatenate([wr_hi, wr_lo], axis=1), rb, tri


_EPILOGUE_OUT_SPECS = [_row_spec(D_MODEL), _row_spec(ROW_WORDS), pl.BlockSpec((8, TM), lambda i: (0, i)),
                       _const_spec((BUCKET_ROWS, LANE))]
_EPILOGUE_OUT_SHAPE = [jax.ShapeDtypeStruct((NT, D_MODEL), F32), jax.ShapeDtypeStruct((NT, ROW_WORDS), U32),
                       jax.ShapeDtypeStruct((8, NT), I32), jax.ShapeDtypeStruct((BUCKET_ROWS, LANE), F32)]
_EPILOGUE_SCRATCH = [pltpu.VMEM((BUCKET_ROWS, LANE), F32), pltpu.VMEM((TM, D_MODEL), BF16),
                     pltpu.VMEM((TM, D_MODEL), BF16), pltpu.VMEM((TM, D_MODEL), F32)]


def _epilogue_specs(layer):
    return [_mod_spec(layer, 2, TM), _const_spec((1, D_MODEL)), _mod_spec(layer, 4, TM),
            _mod_spec(layer, 3, TM), _const_spec((D_MODEL, 2 * LANE)), _const_spec((N_EXPERTS, TM)),
            _const_spec((TM, TM))]


def _out0(att, mixed, xp, xs, mods, w_fourier, w_out, norm_ffn0, router):
    wf = jnp.zeros((FOURIER_WIDTH, FOURIER_WIDTH), F32)
    for gi in range(N_FGROUPS):
        sl = slice(gi * FGROUP_DIM, (gi + 1) * FGROUP_DIM)
        wf = wf.at[sl, sl].set(w_fourier[gi])
    w_out = w_out.astype(BF16)
    wr, rb, tri = router
    return pl.pallas_call(
        _out0_kernel,
        grid=(NT // TM,),
        in_specs=_split_row_specs(ATT_WIDTH) + _split_row_specs(FOURIER_WIDTH)
        + [_const_spec((FOURIER_WIDTH, FOURIER_WIDTH)), _const_spec((ATT_WIDTH, D_MODEL)),
           _const_spec((FOURIER_WIDTH, D_MODEL))]
        + _split_row_specs(D_MODEL) + _epilogue_specs(0),
        out_specs=_EPILOGUE_OUT_SPECS,
        out_shape=_EPILOGUE_OUT_SHAPE,
        scratch_shapes=_EPILOGUE_SCRATCH,
        compiler_params=_cparams(("arbitrary",)),
        name="out_proj_att",
    )(att[0], att[1], mixed[0], mixed[1], wf.astype(BF16), w_out[:ATT_WIDTH], w_out[ATT_WIDTH:], xp, xs, mods,
      norm_ffn0.reshape(1, D_MODEL), mods, mods, wr, rb, tri)


def _dispatch_tables(meta, counts):
    bucket, rank = meta[0], meta[1]
    cnt = counts[:N_BUCKETS, 0].astype(I32)
    padded = (cnt + (TE - 1)) // TE * TE
    ends = jnp.cumsum(padded)
    starts = ends - padded
    kk = jnp.arange(N_BUCKETS, dtype=I32)
    pos = rank + jnp.sum(jnp.where(bucket[None, :] == kk[:, None], starts[:, None], 0), axis=0)
    tile0 = jnp.arange(N_ETILES, dtype=I32) * TE
    used = tile0 < ends[-1]
    tb = jnp.sum((tile0[:, None] >= ends[None, :]).astype(I32), axis=1)
    tb = jnp.where(used, tb, jnp.max(jnp.where(used, tb, 0)))
    pick = tb[:, None] == kk[None, :]

    def per_tile(table):
        return jnp.sum(jnp.where(pick, table[None, :], 0), axis=1)

    nrow = jnp.where(used, jnp.clip(per_tile(starts + cnt) - tile0, 0, TE), 0)
    pos3 = pos.reshape(SC_WORKERS, NT // (SC_WORKERS * SC_CHUNK), SC_CHUNK)
    return pos3, per_tile(jnp.asarray(BUCKET_A, I32)), per_tile(jnp.asarray(BUCKET_B, I32)), nrow


def _sc_permute(src, pos3, n_out, scatter):
    width = src.shape[1]
    _, n_chunks, chunk = pos3.shape
    rows_per_worker = n_chunks * chunk
    mesh = plsc.VectorSubcoreMesh(core_axis_name="c", subcore_axis_name="s")

    @functools.partial(pl.kernel, out_type=jax.ShapeDtypeStruct((n_out, width), src.dtype), mesh=mesh,
                       scratch_types=[pltpu.VMEM((n_chunks, chunk), I32), pltpu.VMEM((2, chunk, width), src.dtype),
                                      pltpu.SemaphoreType.DMA((2,))])
    def permute(src_hbm, pos_hbm, out_hbm, pos_v, buf, sem):
        worker = lax.axis_index("s") * SC_CORES + lax.axis_index("c")
        base = worker * rows_per_worker
        pltpu.sync_copy(pos_hbm.at[worker], pos_v)

        def own(j):
            return pl.ds(base + j * chunk, chunk)

        def load(j):
            rows = src_hbm.at[own(j)] if scatter else src_hbm.at[pos_v.at[j]]
            return pltpu.async_copy(rows, buf.at[j % 2], sem.at[j % 2])

        pending = load(0)
        for j in range(n_chunks):
            following = load(j + 1) if j + 1 < n_chunks else None
            pending.wait()
            pltpu.sync_copy(buf.at[j % 2], out_hbm.at[pos_v.at[j]] if scatter else out_hbm.at[own(j)])
            pending = following

    return permute(src, pos3)


def _experts_kernel(ea_ref, eb_ref, nrow_ref, rows_ref, wga_ref, wua_ref, wda_ref, wgb_ref, wub_ref, wdb_ref, y_ref,
                    cga, cua, cda, cgb, cub, cdb):
    j = pl.program_id(0)
    n = nrow_ref[j]

    @pl.when(n == 0)
    def _():
        y_ref[...] = jnp.zeros_like(y_ref)

    @pl.when(n > 0)
    def _():
        prev = jnp.maximum(j - 1, 0)

        @pl.when(jnp.logical_or(j == 0, ea_ref[j] != ea_ref[prev]))
        def _():
            cga[...] = wga_ref[...].astype(BF16)
            cua[...] = wua_ref[...].astype(BF16)
            cda[...] = wda_ref[...].astype(BF16)

        @pl.when(jnp.logical_or(j == 0, eb_ref[j] != eb_ref[prev]))
        def _():
            cgb[...] = wgb_ref[...].astype(BF16)
            cub[...] = wub_ref[...].astype(BF16)
            cdb[...] = wdb_ref[...].astype(BF16)

        valid = lax.broadcasted_iota(I32, (TE, 1), 0) < n
        h = jnp.where(valid, _unpack_bf16_pairs(rows_ref[:, :HALF]), 0.0).astype(BF16)
        gates = jnp.where(valid, pltpu.bitcast(rows_ref[:, HALF:], F32), 0.0)
        y = None
        for cg, cu, cd, col in ((cga, cua, cda, 0), (cgb, cub, cdb, 1)):
            a = jnp.dot(h, cg[...], preferred_element_type=F32)
            u = jnp.dot(h, cu[...], preferred_element_type=F32)
            hid = _silu(a) * u * gates[:, col:col + 1]
            o = jnp.dot(hid.astype(BF16), cd[...], preferred_element_type=F32)
            y = o if y is None else y + o
        y_ref[...] = _pack_bf16_pairs(y)


def _experts(rows_sorted, ea, eb, nrow, layer, w_gate, w_up, w_down):
    def w_spec(shape, which):
        return pl.BlockSpec((None, None) + shape, lambda j, ea, eb, nr: (layer, (ea, eb)[which][j], 0, 0))

    up_shape, down_shape = (D_MODEL, EXPERT_FF), (EXPERT_FF, D_MODEL)
    grid_spec = pltpu.PrefetchScalarGridSpec(
        num_scalar_prefetch=3,
        grid=(N_ETILES,),
        in_specs=[pl.BlockSpec((TE, ROW_WORDS), lambda j, *_: (j, 0)),
                  w_spec(up_shape, 0), w_spec(up_shape, 0), w_spec(down_shape, 0),
                  w_spec(up_shape, 1), w_spec(up_shape, 1), w_spec(down_shape, 1)],
        out_specs=pl.BlockSpec((TE, HALF), lambda j, *_: (j, 0)),
        scratch_shapes=[pltpu.VMEM(up_shape, BF16), pltpu.VMEM(up_shape, BF16), pltpu.VMEM(down_shape, BF16),
                        pltpu.VMEM(up_shape, BF16), pltpu.VMEM(up_shape, BF16), pltpu.VMEM(down_shape, BF16)])
    return pl.pallas_call(
        _experts_kernel,
        grid_spec=grid_spec,
        out_shape=jax.ShapeDtypeStruct((P_MAX, HALF), U32),
        compiler_params=_cparams(("arbitrary",)),
        name="experts_layer%d" % layer,
    )(ea, eb, nrow, rows_sorted, w_gate, w_up, w_down, w_gate, w_up, w_down)


def _moe(rows, meta, counts, layer, w_gate, w_up, w_down):
    pos3, ea, eb, nrow = _dispatch_tables(meta, counts)
    rows_sorted = _sc_permute(rows, pos3, P_MAX, scatter=True)
    y_sorted = _experts(rows_sorted, ea, eb, nrow, layer, w_gate, w_up, w_down)
    return _sc_permute(y_sorted, pos3, NT, scatter=False)


def _final_kernel(x_ref, y_ref, g2_ref, nfin_ref, yp_ref, ys_ref):
    i = pl.program_id(0)
    x = x_ref[...] + g2_ref[...] * _unpack_bf16_pairs(y_ref[...])
    out = x * lax.rsqrt(jnp.mean(x * x, axis=-1, keepdims=True) + EPS) * nfin_ref[...]

    @pl.when(i < N_CTX_TILES)
    def _():
        yp_ref[...] = out

    @pl.when(i >= N_CTX_TILES)
    def _():
        ys_ref[...] = out


def _final(x1, y_tok, mods, norm_final):
    return pl.pallas_call(
        _final_kernel,
        grid=(NT // TM,),
        in_specs=[_row_spec(D_MODEL), _row_spec(HALF), _mod_spec(DEPTH - 1, 5, TM), _const_spec((1, D_MODEL))],
        out_specs=_split_row_specs(D_MODEL),
        out_shape=[jax.ShapeDtypeStruct((N_CTX, D_MODEL), F32), jax.ShapeDtypeStruct((N_LAT, D_MODEL), F32)],
        compiler_params=_cparams(("arbitrary",)),
        name="final_norm",
    )(x1, y_tok, mods, norm_final.reshape(1, D_MODEL))


def _in1_kernel(x_ref, y_ref, g2_ref, nm_ref, sc_ref, sh_ref, w_ref, x2_ref, z_ref, xbc_ref, dt_ref, h_ref):
    g2 = g2_ref[...]
    gain = nm_ref[...] * (1.0 + sc_ref[...])
    shift = sh_ref[...]

    def norm_rows(rows):
        x = x_ref[rows, :] + g2 * _unpack_bf16_pairs(y_ref[rows, :])
        x2_ref[rows, :] = x
        h_ref[rows, :] = (x * lax.rsqrt(jnp.mean(x * x, axis=-1, keepdims=True) + EPS) * gain + shift).astype(BF16)

    _row_slabs(TM, SLAB, norm_rows)
    h = h_ref[...]
    step = 512
    nt_dims = (((1,), (1,)), ((), ()))

    def proj(c0, width):
        return lax.dot_general(h, w_ref[c0:c0 + width, :], nt_dims, preferred_element_type=F32)

    for c0 in range(0, SSM_INNER, step):
        z_ref[:, c0:c0 + step] = proj(c0, step).astype(BF16)
    for c0 in range(0, SSM_CONV_CH, step):
        xbc_ref[:, c0:c0 + step] = proj(SSM_INNER + c0, step).astype(BF16)
    n_dt = 2 * SSM_HEADS
    dt_ref[:, :n_dt] = proj(SSM_INNER + SSM_CONV_CH, n_dt)
    dt_ref[:, n_dt:] = jnp.zeros((TM, LANE - n_dt), F32)


def _in1(x1, y_tok, mods, norm_mix1, w_in):
    w = jnp.swapaxes(w_in, 0, 1).astype(BF16)
    return pl.pallas_call(
        _in1_kernel,
        grid=(NT // TM,),
        in_specs=[_row_spec(D_MODEL), _row_spec(HALF), _mod_spec(0, 5, TM), _const_spec((1, D_MODEL)),
                  _mod_spec(1, 1, TM), _mod_spec(1, 0, TM), _const_spec((ODD_IN, D_MODEL))],
        out_specs=[_row_spec(D_MODEL), _row_spec(SSM_INNER), _row_spec(SSM_CONV_CH), _row_spec(LANE)],
        out_shape=[jax.ShapeDtypeStruct((NT, D_MODEL), F32), jax.ShapeDtypeStruct((NT, SSM_INNER), BF16),
                   jax.ShapeDtypeStruct((NT, SSM_CONV_CH), BF16), jax.ShapeDtypeStruct((NT, LANE), F32)],
        scratch_shapes=[pltpu.VMEM((TM, D_MODEL), BF16)],
        compiler_params=_cparams(("arbitrary",)),
        name="in_proj_ssm",
    )(x1, y_tok, mods, norm_mix1.reshape(1, D_MODEL), mods, mods, w)


TCV = 256


def _conv_kernel(x_ref, prev_ref, next_ref, w_ref, b_ref, o_ref):
    i = pl.program_id(0)
    nct = N_CTX // TCV
    j = (i - nct) % (DEC_SEQ // TCV)
    first = jnp.logical_or(i < nct, j == 0)
    last = jnp.logical_or(i < nct, j == DEC_SEQ // TCV - 1)
    row = lax.broadcasted_iota(jnp.int32, (TCV, LANE), 0)
    top = row == 0
    bottom = row == TCV - 1
    for c in range(SSM_CONV_CH // LANE):
        cols = slice(c * LANE, (c + 1) * LANE)
        x = x_ref[:, cols].astype(F32)
        prev_row = jnp.where(first, 0.0, prev_ref[7:8, cols].astype(F32))
        next_row = jnp.where(last, 0.0, next_ref[0:1, cols].astype(F32))
        xm1 = jnp.where(top, prev_row, pltpu.roll(x, 1, 0))
        xp1 = jnp.where(bottom, next_row, pltpu.roll(x, TCV - 1, 0))
        y = xm1 * w_ref[0:1, cols] + x * w_ref[1:2, cols] + xp1 * w_ref[2:3, cols] + b_ref[:, cols]
        o_ref[:, cols] = _silu(y).astype(BF16)


def _conv(xbc, conv_w, conv_b):
    r8 = TCV // 8
    nblk8 = NT // 8
    return pl.pallas_call(
        _conv_kernel,
        grid=(NT // TCV,),
        in_specs=[_row_spec(SSM_CONV_CH, TCV),
                  pl.BlockSpec((8, SSM_CONV_CH), lambda i: (jnp.maximum(i * r8 - 1, 0), 0)),
                  pl.BlockSpec((8, SSM_CONV_CH), lambda i: (jnp.minimum((i + 1) * r8, nblk8 - 1), 0)),
                  _const_spec((3, SSM_CONV_CH)), _const_spec((1, SSM_CONV_CH))],
        out_specs=_row_spec(SSM_CONV_CH, TCV),
        out_shape=jax.ShapeDtypeStruct((NT, SSM_CONV_CH), BF16),
        compiler_params=_cparams(("arbitrary",)),
        name="ssm_conv",
    )(xbc, xbc, xbc, conv_w, conv_b.reshape(1, SSM_CONV_CH))


HPG = SSM_HEADS // SSM_GROUPS
GW = HPG * SSM_HEAD_DIM


def _ssd_kernel(*refs, has_init, write_state):
    refs = list(refs)
    xcf_ref, xcb_ref, dtf_ref, dtb_ref, bias_ref, alog_ref = refs[:6]
    refs = refs[6:]
    h0f_ref, h0b_ref = (refs.pop(0), refs.pop(0)) if has_init else (None, None)
    yf_ref, yb_ref = refs.pop(0), refs.pop(0)
    houtf_ref, houtb_ref = (refs.pop(0), refs.pop(0)) if write_state else (None, None)
    hf_ref, hb_ref = refs
    c = pl.program_id(1)

    @pl.when(c == 0)
    def _():
        for h_ref, h0_ref in ((hf_ref, h0f_ref), (hb_ref, h0b_ref)):
            for g in range(SSM_GROUPS):
                if has_init:
                    h_ref[g] = h0_ref[g * GW:(g + 1) * GW, :].T
                else:
                    h_ref[g] = jnp.zeros((SSM_STATE, GW), F32)

    _ssd_chunk(xcf_ref, dtf_ref, bias_ref, alog_ref, yf_ref, hf_ref, False)
    _ssd_chunk(xcb_ref, dtb_ref, bias_ref, alog_ref, yb_ref, hb_ref, True)

    if write_state:
        @pl.when(c == pl.num_programs(1) - 1)
        def _():
            for h_ref, hout_ref in ((hf_ref, houtf_ref), (hb_ref, houtb_ref)):
                for g in range(SSM_GROUPS):
                    hout_ref[g * GW:(g + 1) * GW, :] = h_ref[g].T


def _ssd_chunk(xc_ref, dt_ref, bias_ref, alog_ref, y_ref, h_ref, reverse):
    col0 = SSM_HEADS if reverse else 0
    Q = SSM_CHUNK
    dt = jax.nn.softplus(dt_ref[...] + bias_ref[...])
    a = dt * -jnp.exp(alog_ref[...])
    row = lax.broadcasted_iota(jnp.int32, (Q, Q), 0)
    col = lax.broadcasted_iota(jnp.int32, (Q, Q), 1)
    keep = (col >= row) if reverse else (col <= row)
    tri = keep.astype(F32)
    acs = jnp.dot(tri, a, precision=HIGHEST, preferred_element_type=F32)
    edge = (0 if reverse else Q - 1)
    acs_end = acs[edge:edge + 1, :]
    log2e = math.log2(math.e)
    acs2 = acs * log2e
    src_t = (acs2 - jnp.log2(dt)).T
    out_t = (jnp.exp(acs_end - acs) * dt).T
    chunk_decay = jnp.exp(jnp.broadcast_to(acs_end, (8, LANE)))
    nt_dims = (((1,), (1,)), ((), ()))
    lane = lax.broadcasted_iota(jnp.int32, (Q, LANE), 1)
    low = lane < SSM_HEAD_DIM
    low8 = low[:8]

    def two_heads(v):
        zero = jnp.zeros_like(v)
        return jnp.concatenate([jnp.where(low, v, zero), jnp.where(low, zero, v)], axis=0)

    for g in range(SSM_GROUPS):
        bg = xc_ref[:, SSM_INNER + g * SSM_STATE:SSM_INNER + (g + 1) * SSM_STATE]
        cg = xc_ref[:, SSM_INNER + SSM_GROUPS * SSM_STATE + g * SSM_STATE:
                    SSM_INNER + SSM_GROUPS * SSM_STATE + (g + 1) * SSM_STATE]
        cb = lax.dot_general(cg, bg, nt_dims, preferred_element_type=F32).astype(BF16)
        bg_t = bg.astype(F32).T
        for j in range(HPG // 2):
            pair = slice(g * GW + j * LANE, g * GW + (j + 1) * LANE)
            mats, c_in, b_out, cdec = [], [], [], []
            for hh in (2 * j, 2 * j + 1):
                cidx = col0 + g * HPG + hh
                to_l = jnp.broadcast_to(acs2[:, cidx:cidx + 1], (Q, Q))
                lmat = jnp.where(keep, jnp.exp2(to_l - src_t[cidx:cidx + 1, :]), 0.0)
                mats.append(cb * lmat.astype(BF16))
                c_in.append(cg * jnp.exp2(to_l).astype(BF16))
                b_out.append((bg_t * out_t[cidx:cidx + 1, :]).astype(BF16))
                cdec.append(jnp.broadcast_to(chunk_decay[:, cidx:cidx + 1], (8, LANE)))
            hp = h_ref[g, :, j * LANE:(j + 1) * LANE]
            x2 = two_heads(xc_ref[:, pair])
            rhs = jnp.concatenate([x2, two_heads(hp.astype(BF16))], axis=0)
            y = jnp.dot(jnp.concatenate(mats + c_in, axis=1), rhs, preferred_element_type=F32)
            y_ref[:, pair] = y.astype(BF16)
            st = jnp.dot(jnp.concatenate(b_out, axis=1), x2, preferred_element_type=F32)
            h_ref[g, :, j * LANE:(j + 1) * LANE] = hp * jnp.where(low8, cdec[0], cdec[1])[0:1, :] + st


def _ssd(xc, dt_raw, bias128, alog128, h0_fwd, h0_bwd):
    outs = []
    for (row0, nseq, seqlen, has_init, write_state) in ((0, BATCH, SEQ, False, True),
                                                         (N_CTX, DEC_BATCH, DEC_SEQ, True, False)):
        nc = seqlen // SSM_CHUNK
        base = row0 // SSM_CHUNK

        def fwd_in(s, c, nc=nc, base=base):
            return (base + s * nc + c, 0)

        def bwd_in(s, c, nc=nc, base=base):
            return (base + s * nc + nc - 1 - c, 0)

        def fwd_out(s, c, nc=nc):
            return (s * nc + c, 0)

        def bwd_out(s, c, nc=nc):
            return (s * nc + nc - 1 - c, 0)

        state_spec = pl.BlockSpec((None, SSM_INNER, SSM_STATE), lambda s, c: (s, 0, 0))
        y_shape = jax.ShapeDtypeStruct((nseq * seqlen, SSM_INNER), BF16)
        state_shape = jax.ShapeDtypeStruct((nseq, SSM_INNER, SSM_STATE), F32)
        in_specs = [pl.BlockSpec((SSM_CHUNK, SSM_CONV_CH), fwd_in), pl.BlockSpec((SSM_CHUNK, SSM_CONV_CH), bwd_in),
                    pl.BlockSpec((SSM_CHUNK, LANE), fwd_in), pl.BlockSpec((SSM_CHUNK, LANE), bwd_in),
                    _const_spec((1, LANE)), _const_spec((1, LANE))]
        args = [xc, xc, dt_raw, dt_raw, bias128, alog128]
        if has_init:
            in_specs += [state_spec, state_spec]
            args += [h0_fwd, h0_bwd]
        out_specs = [pl.BlockSpec((SSM_CHUNK, SSM_INNER), fwd_out), pl.BlockSpec((SSM_CHUNK, SSM_INNER), bwd_out)]
        out_shape = [y_shape, y_shape]
        if write_state:
            out_specs += [state_spec, state_spec]
            out_shape += [state_shape, state_shape]
        state_scratch = pltpu.VMEM((SSM_GROUPS, SSM_STATE, GW), F32)
        outs.append(pl.pallas_call(
            functools.partial(_ssd_kernel, has_init=has_init, write_state=write_state),
            grid=(nseq, nc),
            in_specs=in_specs,
            out_specs=out_specs,
            out_shape=out_shape,
            scratch_shapes=[state_scratch, state_scratch],
            compiler_params=_cparams(("arbitrary", "arbitrary")),
            name="ssd_%s" % ("lat" if has_init else "ctx"),
        )(*args))
    ctx, lat = outs
    return (ctx[0], lat[0]), (ctx[1], lat[1]), ctx[2], ctx[3]


def _out1_kernel(yfc_ref, yfl_ref, ybc_ref, ybl_ref, xs_ref, z_ref, dskip_ref, ng_ref, wo_ref, x_ref, g1_ref, nf_ref,
                 sc2_ref, sh2_ref, wr_ref, rb_ref, tri_ref, x1_ref, rows_ref, meta_ref, count_ref,
                 carry_ref, hhi_ref, hlo_ref, acc_ref, y_ref):
    is_lat = pl.program_id(0) >= N_CTX_TILES
    dskip = dskip_ref[...]
    gain = ng_ref[...]

    def gate_norm(rows):
        yf = jnp.where(is_lat, yfl_ref[rows, :], yfc_ref[rows, :])
        yb = jnp.where(is_lat, ybl_ref[rows, :], ybc_ref[rows, :])
        y = yf.astype(F32) + yb.astype(F32) + dskip * xs_ref[rows, :].astype(F32)
        y = y * _silu(z_ref[rows, :].astype(F32))
        y_ref[rows, :] = (y * lax.rsqrt(jnp.mean(y * y, axis=-1, keepdims=True) + EPS) * gain).astype(BF16)

    _row_slabs(TM, SLAB // 2, gate_norm)
    acc_ref[...] = jnp.dot(y_ref[...], wo_ref[...], preferred_element_type=F32)
    _residual_router(lambda rows: x_ref[rows, :], acc_ref, g1_ref, nf_ref, sc2_ref, sh2_ref, wr_ref, rb_ref,
                     tri_ref, x1_ref, rows_ref, meta_ref, count_ref, carry_ref, hhi_ref, hlo_ref)


def _out1(yf, yb, xc, z, d_skip, ssm_norm, w_out, x, mods, norm_ffn1, router):
    wr, rb, tri = router
    return pl.pallas_call(
        _out1_kernel,
        grid=(NT // TM,),
        in_specs=_split_row_specs(SSM_INNER) + _split_row_specs(SSM_INNER)
        + [_row_spec(SSM_INNER), _row_spec(SSM_INNER),
           _const_spec((1, SSM_INNER)), _const_spec((1, SSM_INNER)), _const_spec((SSM_INNER, D_MODEL)),
           _row_spec(D_MODEL)] + _epilogue_specs(1),
        out_specs=_EPILOGUE_OUT_SPECS,
        out_shape=_EPILOGUE_OUT_SHAPE,
        scratch_shapes=_EPILOGUE_SCRATCH + [pltpu.VMEM((TM, SSM_INNER), BF16)],
        compiler_params=_cparams(("arbitrary",)),
        name="out_proj_ssm",
    )(yf[0], yf[1], yb[0], yb[1], xc, z, jnp.repeat(d_skip.astype(F32), SSM_HEAD_DIM).reshape(1, SSM_INNER),
      ssm_norm.reshape(1, SSM_INNER), w_out.astype(BF16), x, mods, norm_ffn1.reshape(1, D_MODEL), mods, mods,
      wr, rb, tri)


def _kv_from_lane_major(t):
    return t.reshape(BATCH, 1, N_KV_HEADS, HEAD_DIM, SEQ).transpose(0, 1, 4, 2, 3)


def kernel(x_prompt, x_sample, cache_k, cache_v, state_fwd, state_bwd, c, c_ctx, w_ada, b_ada, norm_mix, norm_ffn,
           w_in_att, q_gain, k_gain, w_fourier, w_out_att, w_in_ssm, conv_w, conv_b, dt_bias_f, dt_bias_b, a_log_f,
           a_log_b, d_skip, ssm_norm, w_out_ssm, w_router, router_bias, w_gate, w_up, w_down, norm_final):
    xp = x_prompt.reshape(N_CTX, D_MODEL)
    xs = x_sample.reshape(N_LAT, D_MODEL)
    cond8 = jnp.zeros((8, D_MODEL), F32).at[0].set(c_ctx).at[1:1 + DEC_BATCH].set(c)
    mods = _ada_mods(cond8, w_ada, b_ada)
    router = _router_operands(w_router, router_bias)
    cos2, sin2 = _rope_tables()

    q, kp, vp, u, new_k, new_v = _in0(xp, xs, mods, norm_mix[0], w_in_att[0], q_gain[0], k_gain[0], cos2, sin2)
    att = _attention(q, kp, vp, cache_k[:, 0], cache_v[:, 0])
    mixed = _fourier(u)
    x1, rows, meta, counts = _out0(att, mixed, xp, xs, mods, w_fourier[0], w_out_att[0], norm_ffn[0], router)
    y_tok = _moe(rows, meta, counts, 0, w_gate, w_up, w_down)

    x2, z, xbc, dt_raw = _in1(x1, y_tok, mods, norm_mix[1], w_in_ssm[0])
    xc = _conv(xbc, conv_w[0], conv_b[0])
    pad = jnp.zeros((LANE - 2 * SSM_HEADS,), F32)
    bias128 = jnp.concatenate([dt_bias_f[0], dt_bias_b[0], pad]).astype(F32).reshape(1, LANE)
    alog128 = jnp.concatenate([a_log_f[0], a_log_b[0], pad]).astype(F32).reshape(1, LANE)
    yf, yb, sf, sb = _ssd(xc, dt_raw, bias128, alog128, state_fwd.reshape(DEC_BATCH, SSM_INNER, SSM_STATE),
                          state_bwd.reshape(DEC_BATCH, SSM_INNER, SSM_STATE))
    x3, rows, meta, counts = _out1(yf, yb, xc, z, d_skip[0], ssm_norm[0], w_out_ssm[0], x2, mods, norm_ffn[1], router)
    y_tok = _moe(rows, meta, counts, 1, w_gate, w_up, w_down)
    y_prompt, y_sample = _final(x3, y_tok, mods, norm_final)

    state_shape = (BATCH, 1, SSM_HEADS, SSM_HEAD_DIM, SSM_STATE)
    return (y_prompt.reshape(BATCH, SEQ, D_MODEL), y_sample.reshape(DEC_BATCH, DEC_SEQ, D_MODEL),
            _kv_from_lane_major(new_k), _kv_from_lane_major(new_v),
            sf.reshape(state_shape), sb.reshape(state_shape))
```

```python
import functools
import math

import numpy as np
import jax
import jax.numpy as jnp
from jax import lax
from jax.experimental import pallas as pl
from jax.experimental.pallas import tpu as pltpu
from jax.experimental.pallas import tpu_sc as plsc

F32 = jnp.float32
BF16 = jnp.bfloat16
U32 = jnp.uint32
I32 = jnp.int32
HIGHEST = lax.Precision.HIGHEST

D_MODEL = 1024
BATCH = 32
SEQ = 256
DEPTH = 2
DEC_BATCH = 2
DEC_SEQ = 4096
PAST_LEN = 256
GRID_W = 64
EPS = 1e-6
N_HEADS = 8
N_KV_HEADS = 2
HEAD_DIM = 64
ATT_WIDTH = N_HEADS * HEAD_DIM
KV_WIDTH = N_KV_HEADS * HEAD_DIM
ROPE_THETA = 10000.0
N_FGROUPS = 8
FGROUP_DIM = 64
FOURIER_WIDTH = N_FGROUPS * FGROUP_DIM
EVEN_IN = ATT_WIDTH + 2 * KV_WIDTH + FOURIER_WIDTH
SSM_INNER = 2 * D_MODEL
SSM_HEAD_DIM = 64
SSM_HEADS = SSM_INNER // SSM_HEAD_DIM
SSM_GROUPS = 4
SSM_STATE = 128
SSM_CHUNK = 128
SSM_CONV_CH = SSM_INNER + 2 * SSM_GROUPS * SSM_STATE
ODD_IN = SSM_INNER + SSM_CONV_CH + 2 * SSM_HEADS
N_EXPERTS = 16
EXPERTS_PER_GROUP = 4
N_EXPERT_GROUPS = 4
EXPERT_FF = 512

N_CTX = BATCH * SEQ
N_LAT = DEC_BATCH * DEC_SEQ
NT = N_CTX + N_LAT
N_SEG = 1 + DEC_BATCH
LANE = 128
VMEM_LIMIT = 56 * 1024 * 1024

TM = 512
N_CTX_TILES = N_CTX // TM
LAT_TILES_PER_SEQ = DEC_SEQ // TM

PAIR_ORDER = ((0, 1), (0, 2), (0, 3), (1, 3), (1, 2), (3, 2))
N_BUCKETS = N_EXPERT_GROUPS * len(PAIR_ORDER)
BUCKET_A = tuple(g * EXPERTS_PER_GROUP + a for g in range(N_EXPERT_GROUPS) for a, _ in PAIR_ORDER)
BUCKET_B = tuple(g * EXPERTS_PER_GROUP + b for g in range(N_EXPERT_GROUPS) for _, b in PAIR_ORDER)
BUCKET_ROWS = 32
TE = 256
N_ETILES = (NT + N_BUCKETS * (TE - 1) + TE - 1) // TE
P_MAX = N_ETILES * TE
HALF = D_MODEL // 2
ROW_WORDS = HALF + LANE
SC_CORES = 2
SC_SUBCORES = 16
SC_WORKERS = SC_CORES * SC_SUBCORES
SC_CHUNK = 64


def _cparams(sem):
    return pltpu.CompilerParams(dimension_semantics=sem, vmem_limit_bytes=VMEM_LIMIT)


def _seg_of_tile(i, tm):
    nct = N_CTX // tm
    return jnp.where(i < nct, 0, 1 + (i - nct) // (DEC_SEQ // tm))


def _mod_spec(layer, which, tm):
    return pl.BlockSpec((None, None, None, 1, D_MODEL),
                        lambda i, *_: (layer, which, _seg_of_tile(i, tm), 0, 0))


def _row_spec(width, tm=TM):
    return pl.BlockSpec((tm, width), lambda i, *_: (i, 0))


def _const_spec(shape):
    nd = len(shape)
    return pl.BlockSpec(shape, lambda *_: (0,) * nd)


def _silu(x):
    return x * jax.nn.sigmoid(x)


def _ada_kernel(cond_ref, w_ref, b_ref, o_ref):
    c = cond_ref[...]
    o_ref[...] = jnp.dot(_silu(c), w_ref[...], precision=HIGHEST, preferred_element_type=F32) + b_ref[...]


def _ada_mods(cond8, w_ada, b_ada):
    tn = 1536
    out = pl.pallas_call(
        _ada_kernel,
        grid=(DEPTH, 6 * D_MODEL // tn),
        in_specs=[pl.BlockSpec((8, D_MODEL), lambda l, n: (0, 0)),
                  pl.BlockSpec((None, D_MODEL, tn), lambda l, n: (l, 0, n)),
                  pl.BlockSpec((None, 1, tn), lambda l, n: (l, 0, n))],
        out_specs=pl.BlockSpec((None, 8, tn), lambda l, n: (l, 0, n)),
        out_shape=jax.ShapeDtypeStruct((DEPTH, 8, 6 * D_MODEL), F32),
        compiler_params=_cparams(("arbitrary", "arbitrary")),
        name="ada_mod",
    )(cond8, w_ada, b_ada.reshape(DEPTH, 1, 6 * D_MODEL))
    return out.reshape(DEPTH, 8, 6, D_MODEL)[:, :N_SEG].transpose(0, 2, 1, 3)[:, :, :, None, :]


def _rope_tables():
    t = np.arange(DEC_SEQ)
    row = (t // GRID_W).astype(np.float64)
    col = (t % GRID_W).astype(np.float64)
    axis_dim = HEAD_DIM // 2
    freqs = ROPE_THETA ** (-np.arange(0, axis_dim, 2, dtype=np.float64) / axis_dim)
    ang = np.concatenate([row[:, None] * freqs, col[:, None] * freqs], axis=-1)
    cos = np.repeat(np.cos(ang), 2, axis=1)
    sin = np.repeat(np.sin(ang), 2, axis=1)
    sign = np.where(np.arange(HEAD_DIM) % 2 == 0, -1.0, 1.0)
    cos2 = np.tile(cos, (1, 2)).astype(np.float32)
    sin2 = np.tile(sin * sign, (1, 2)).astype(np.float32)
    return jnp.asarray(cos2), jnp.asarray(sin2)


def _dft_cos_sin(n, scale):
    k = np.arange(n)
    ang = 2.0 * np.pi * ((k[:, None] * k[None, :]) % n) / n
    return np.cos(ang) * scale, np.sin(ang) * scale


def _block_diag(m, reps):
    n = m.shape[0]
    out = np.zeros((n * reps, n * reps), m.dtype)
    for r in range(reps):
        out[r * n:(r + 1) * n, r * n:(r + 1) * n] = m
    return out


def _channel_dft():
    c, s = _dft_cos_sin(FGROUP_DIM, FGROUP_DIM ** -0.5)
    return jnp.asarray(np.concatenate([_block_diag(c, N_FGROUPS), _block_diag(s, N_FGROUPS)], axis=1), BF16)


def _group_ones(width):
    return jnp.asarray(_block_diag(np.ones((HEAD_DIM, HEAD_DIM), np.float32), width // HEAD_DIM), BF16)


def _pad_heads(x):
    lane = lax.broadcasted_iota(jnp.int32, x.shape, 1)
    low = lane < HEAD_DIM
    xr = pltpu.roll(x, HEAD_DIM, 1)
    zero = jnp.zeros_like(x)
    return [jnp.where(low, x, zero), jnp.where(low, zero, xr), jnp.where(low, xr, zero), jnp.where(low, zero, x)]


def _ctx_tile(i, *_):
    return (jnp.minimum(i, N_CTX_TILES - 1), 0)


def _lat_tile(i, *_):
    return (jnp.maximum(i - N_CTX_TILES, 0), 0)


def _split_row_specs(width):
    return [pl.BlockSpec((TM, width), _ctx_tile), pl.BlockSpec((TM, width), _lat_tile)]


def _in0_kernel(xp_ref, xs_ref, nm_ref, sc_ref, sh_ref, w_ref, qg_ref, kg_ref, ones_ref, cos_ref, sin_ref, dft_ref,
                q_ref, kp_ref, vp_ref, u_ref, nk_ref, nv_ref, h_ref):
    i = pl.program_id(0)
    is_lat = i >= N_CTX_TILES
    gain = nm_ref[...] * (1.0 + sc_ref[...])
    shift = sh_ref[...]

    def norm_rows(rows):
        x = jnp.where(is_lat, xs_ref[rows, :], xp_ref[rows, :])
        h_ref[rows, :] = (x * lax.rsqrt(jnp.mean(x * x, axis=-1, keepdims=True) + EPS) * gain + shift).astype(BF16)

    _row_slabs(TM, SLAB, norm_rows)
    p = jnp.dot(h_ref[...], w_ref[...], preferred_element_type=F32)
    q = p[:, :ATT_WIDTH]
    k = p[:, ATT_WIDTH:ATT_WIDTH + KV_WIDTH]
    v = p[:, ATT_WIDTH + KV_WIDTH:ATT_WIDTH + 2 * KV_WIDTH]
    f = p[:, ATT_WIDTH + 2 * KV_WIDTH:]
    ones = ones_ref[...]
    qss = jnp.dot((q * q).astype(BF16), ones, preferred_element_type=F32)
    kss = jnp.dot((k * k).astype(BF16), ones[:KV_WIDTH, :KV_WIDTH], preferred_element_type=F32)
    qn = q * lax.rsqrt(qss * (1.0 / HEAD_DIM) + EPS) * qg_ref[...]
    kn = k * lax.rsqrt(kss * (1.0 / HEAD_DIM) + EPS) * kg_ref[...]

    cos = jnp.where(is_lat, cos_ref[...], 1.0)
    sin = jnp.where(is_lat, sin_ref[...], 0.0)
    lane = lax.broadcasted_iota(jnp.int32, (TM, LANE), 1)
    even = (lane % 2) == 0

    def rope(xc):
        swapped = jnp.where(even, pltpu.roll(xc, LANE - 1, 1), pltpu.roll(xc, 1, 1))
        return xc * cos + swapped * sin

    scale = HEAD_DIM ** -0.5 * math.log2(math.e)
    for j in range(ATT_WIDTH // LANE):
        q_ref[:, j * LANE:(j + 1) * LANE] = (rope(qn[:, j * LANE:(j + 1) * LANE]) * scale).astype(BF16)
    for j, c in enumerate(_pad_heads(rope(kn))):
        kp_ref[:, j * LANE:(j + 1) * LANE] = c.astype(BF16)
    for j, c in enumerate(_pad_heads(v)):
        vp_ref[:, j * LANE:(j + 1) * LANE] = c.astype(BF16)
    u_ref[...] = jnp.dot(f.astype(BF16), dft_ref[...], preferred_element_type=F32).astype(BF16)

    @pl.when(jnp.logical_not(is_lat))
    def _():
        for r in range(TM // SEQ):
            nk_ref[r] = kn[r * SEQ:(r + 1) * SEQ, :].T
            nv_ref[r] = v[r * SEQ:(r + 1) * SEQ, :].T


def _in0(xp, xs, mods, norm_mix0, w_in, q_gain, k_gain, cos2, sin2):
    def table_idx(i):
        return (jnp.where(i < N_CTX_TILES, 0, (i - N_CTX_TILES) % LAT_TILES_PER_SEQ), 0)

    def ctx_idx(i):
        return (jnp.minimum(i, N_CTX_TILES - 1), 0, 0)

    seqs = TM // SEQ
    outs = pl.pallas_call(
        _in0_kernel,
        grid=(NT // TM,),
        in_specs=_split_row_specs(D_MODEL) + [_const_spec((1, D_MODEL)), _mod_spec(0, 1, TM), _mod_spec(0, 0, TM),
                  _const_spec((D_MODEL, EVEN_IN)), _const_spec((1, ATT_WIDTH)), _const_spec((1, KV_WIDTH)),
                  _const_spec((ATT_WIDTH, ATT_WIDTH)),
                  pl.BlockSpec((TM, LANE), table_idx), pl.BlockSpec((TM, LANE), table_idx),
                  _const_spec((FOURIER_WIDTH, 2 * FOURIER_WIDTH))],
        out_specs=[_row_spec(ATT_WIDTH), _row_spec(4 * LANE), _row_spec(4 * LANE), _row_spec(2 * FOURIER_WIDTH),
                   pl.BlockSpec((seqs, KV_WIDTH, SEQ), ctx_idx), pl.BlockSpec((seqs, KV_WIDTH, SEQ), ctx_idx)],
        out_shape=[jax.ShapeDtypeStruct((NT, ATT_WIDTH), BF16), jax.ShapeDtypeStruct((NT, 4 * LANE), BF16),
                   jax.ShapeDtypeStruct((NT, 4 * LANE), BF16), jax.ShapeDtypeStruct((NT, 2 * FOURIER_WIDTH), BF16),
                   jax.ShapeDtypeStruct((BATCH, KV_WIDTH, SEQ), F32), jax.ShapeDtypeStruct((BATCH, KV_WIDTH, SEQ), F32)],
        scratch_shapes=[pltpu.VMEM((TM, D_MODEL), BF16)],
        compiler_params=_cparams(("arbitrary",)),
        name="in_proj_att",
    )(xp, xs, norm_mix0.reshape(1, D_MODEL), mods, mods, w_in.astype(BF16),
      jnp.tile(q_gain, N_HEADS).reshape(1, ATT_WIDTH), jnp.tile(k_gain, N_KV_HEADS).reshape(1, KV_WIDTH),
      _group_ones(ATT_WIDTH), cos2, sin2, _channel_dft())
    return outs


def _att_kernel(*refs, has_cache):
    if has_cache:
        q_ref, kp_ref, vp_ref, ck_ref, cv_ref, o_ref = refs
        ckp = [c.astype(BF16) for c in _pad_heads(ck_ref[...])]
        cvp = [c.astype(BF16) for c in _pad_heads(cv_ref[...])]
    else:
        q_ref, kp_ref, vp_ref, o_ref = refs
    nt_dims = (((1,), (1,)), ((), ()))
    for j in range(ATT_WIDTH // LANE):
        qj = q_ref[:, j * LANE:(j + 1) * LANE]
        g = j // 2
        acc = None
        for half in range(2):
            c = 2 * g + half
            kk = kp_ref[:, c * LANE:(c + 1) * LANE]
            vv = vp_ref[:, c * LANE:(c + 1) * LANE]
            s = lax.dot_general(qj, kk, nt_dims, preferred_element_type=F32)
            m = jnp.max(s, axis=-1, keepdims=True)
            if has_cache:
                sc = lax.dot_general(qj, ckp[c], nt_dims, preferred_element_type=F32)
                m = jnp.maximum(m, jnp.max(sc, axis=-1, keepdims=True))
            p = jnp.exp2(s - m)
            d = jnp.sum(p, axis=-1, keepdims=True)
            o = jnp.dot(p.astype(BF16), vv, preferred_element_type=F32)
            if has_cache:
                pc = jnp.exp2(sc - m)
                d = d + jnp.sum(pc, axis=-1, keepdims=True)
                o = o + jnp.dot(pc.astype(BF16), cvp[c], preferred_element_type=F32)
            o = o * (1.0 / d)
            acc = o if acc is None else acc + o
        o_ref[:, j * LANE:(j + 1) * LANE] = acc.astype(BF16)


def _attention(q, kp, vp, cache_k, cache_v):
    att_ctx = pl.pallas_call(
        functools.partial(_att_kernel, has_cache=False),
        grid=(BATCH,),
        in_specs=[pl.BlockSpec((SEQ, ATT_WIDTH), lambda b: (b, 0)),
                  pl.BlockSpec((SEQ, 4 * LANE), lambda b: (b, 0)),
                  pl.BlockSpec((SEQ, 4 * LANE), lambda b: (b, 0))],
        out_specs=pl.BlockSpec((SEQ, ATT_WIDTH), lambda b: (b, 0)),
        out_shape=jax.ShapeDtypeStruct((N_CTX, ATT_WIDTH), BF16),
        compiler_params=_cparams(("arbitrary",)),
        name="attention_ctx",
    )(q, kp, vp)
    tq = 256
    off = N_CTX // DEC_SEQ
    att_lat = pl.pallas_call(
        functools.partial(_att_kernel, has_cache=True),
        grid=(DEC_BATCH, DEC_SEQ // tq),
        in_specs=[pl.BlockSpec((tq, ATT_WIDTH), lambda b, i: (N_CTX // tq + b * (DEC_SEQ // tq) + i, 0)),
                  pl.BlockSpec((DEC_SEQ, 4 * LANE), lambda b, i: (off + b, 0)),
                  pl.BlockSpec((DEC_SEQ, 4 * LANE), lambda b, i: (off + b, 0)),
                  pl.BlockSpec((None, PAST_LEN, KV_WIDTH), lambda b, i: (b, 0, 0)),
                  pl.BlockSpec((None, PAST_LEN, KV_WIDTH), lambda b, i: (b, 0, 0))],
        out_specs=pl.BlockSpec((tq, ATT_WIDTH), lambda b, i: (b * (DEC_SEQ // tq) + i, 0)),
        out_shape=jax.ShapeDtypeStruct((N_LAT, ATT_WIDTH), BF16),
        compiler_params=_cparams(("arbitrary", "arbitrary")),
        name="attention_lat",
    )(q, kp, vp, cache_k.reshape(DEC_BATCH, PAST_LEN, KV_WIDTH), cache_v.reshape(DEC_BATCH, PAST_LEN, KV_WIDTH))
    return att_ctx, att_lat


def _four_ctx_kernel(u_ref, c_ref, s_ref, o_ref):
    uc = u_ref[:, :FOURIER_WIDTH]
    us = u_ref[:, FOURIER_WIDTH:]
    o_ref[...] = (jnp.dot(c_ref[...], uc, preferred_element_type=F32)
                  - jnp.dot(s_ref[...], us, preferred_element_type=F32))


FCH = 8


def _four_lat_a_kernel(u_ref, w1_ref, w2_ref, tc_ref, ts_ref, o_ref):
    w1 = w1_ref[...]
    w2 = w2_ref[...]
    for j in range(FCH):
        uc = u_ref[:, j * 2 * FOURIER_WIDTH:j * 2 * FOURIER_WIDTH + FOURIER_WIDTH]
        us = u_ref[:, j * 2 * FOURIER_WIDTH + FOURIER_WIDTH:(j + 1) * 2 * FOURIER_WIDTH]
        z = jnp.dot(w1, uc, preferred_element_type=F32) + jnp.dot(w2, us, preferred_element_type=F32)
        zr = z[:GRID_W]
        zi = z[GRID_W:]
        tc = jnp.concatenate([tc_ref[j]] * (FOURIER_WIDTH // LANE), axis=1)
        ts = jnp.concatenate([ts_ref[j]] * (FOURIER_WIDTH // LANE), axis=1)
        o_ref[j, :, :FOURIER_WIDTH] = (zr * tc - zi * ts).astype(BF16)
        o_ref[j, :, FOURIER_WIDTH:] = (zr * ts + zi * tc).astype(BF16)


def _four_lat_b_kernel(b_ref, c_ref, s_ref, o_ref):
    c = c_ref[...]
    s = s_ref[...]
    for j in range(FCH):
        br = b_ref[:, j * 2 * FOURIER_WIDTH:j * 2 * FOURIER_WIDTH + FOURIER_WIDTH]
        bi = b_ref[:, j * 2 * FOURIER_WIDTH + FOURIER_WIDTH:(j + 1) * 2 * FOURIER_WIDTH]
        o_ref[:, j, :] = (jnp.dot(c, br, preferred_element_type=F32) - jnp.dot(s, bi, preferred_element_type=F32))


def _fourier(u):
    c256, s256 = _dft_cos_sin(SEQ, SEQ ** -0.5)
    mixed_ctx = pl.pallas_call(
        _four_ctx_kernel,
        grid=(BATCH,),
        in_specs=[pl.BlockSpec((SEQ, 2 * FOURIER_WIDTH), lambda b: (b, 0)),
                  _const_spec((SEQ, SEQ)), _const_spec((SEQ, SEQ))],
        out_specs=pl.BlockSpec((SEQ, FOURIER_WIDTH), lambda b: (b, 0)),
        out_shape=jax.ShapeDtypeStruct((N_CTX, FOURIER_WIDTH), F32),
        compiler_params=_cparams(("arbitrary",)),
        name="fourier_ctx",
    )(u, jnp.asarray(c256, BF16), jnp.asarray(s256, BF16))

    g = GRID_W
    c64, s64 = _dft_cos_sin(g, g ** -0.5)
    w1 = jnp.asarray(np.concatenate([c64, s64], axis=0), BF16)
    w2 = jnp.asarray(np.concatenate([-s64, c64], axis=0), BF16)
    t2 = np.arange(g)[:, None]
    k1 = np.arange(g)[None, :]
    ang = 2.0 * np.pi * (t2 * k1) / (g * g)
    tw_c = jnp.asarray(np.broadcast_to(np.cos(ang)[:, :, None], (g, g, LANE)), F32)
    tw_s = jnp.asarray(np.broadcast_to(np.sin(ang)[:, :, None], (g, g, LANE)), F32)
    width = 2 * FOURIER_WIDTH
    u_lat = u[N_CTX:].reshape(DEC_BATCH, g, g * width)
    stage1 = pl.pallas_call(
        _four_lat_a_kernel,
        grid=(DEC_BATCH, g // FCH),
        in_specs=[pl.BlockSpec((None, g, FCH * width), lambda b, i: (b, 0, i)),
                  _const_spec((2 * g, g)), _const_spec((2 * g, g)),
                  pl.BlockSpec((FCH, g, LANE), lambda b, i: (i, 0, 0)),
                  pl.BlockSpec((FCH, g, LANE), lambda b, i: (i, 0, 0))],
        out_specs=pl.BlockSpec((None, FCH, g, width), lambda b, i: (b, i, 0, 0)),
        out_shape=jax.ShapeDtypeStruct((DEC_BATCH, g, g, width), BF16),
        compiler_params=_cparams(("arbitrary", "arbitrary")),
        name="fourier_lat_rows",
    )(u_lat, w1, w2, tw_c, tw_s)
    stage1 = stage1.reshape(DEC_BATCH, g, g * width)
    mixed_lat = pl.pallas_call(
        _four_lat_b_kernel,
        grid=(DEC_BATCH, g // FCH),
        in_specs=[pl.BlockSpec((None, g, FCH * width), lambda b, i: (b, 0, i)),
                  _const_spec((g, g)), _const_spec((g, g))],
        out_specs=pl.BlockSpec((None, g, FCH, FOURIER_WIDTH), lambda b, i: (b, 0, i, 0)),
        out_shape=jax.ShapeDtypeStruct((DEC_BATCH, g, g, FOURIER_WIDTH), F32),
        compiler_params=_cparams(("arbitrary", "arbitrary")),
        name="fourier_lat_cols",
    )(stage1, jnp.asarray(c64, BF16), jnp.asarray(s64, BF16))
    return mixed_ctx, mixed_lat.reshape(N_LAT, FOURIER_WIDTH)


def _pack_bf16_pairs(x):
    n = x.shape[1] // 2
    lo = pltpu.bitcast(x[:, :n].astype(BF16).astype(F32), U32)
    hi = pltpu.bitcast(x[:, n:].astype(BF16).astype(F32), U32)
    return (hi & jnp.uint32(0xFFFF0000)) | (lo >> 16)


def _unpack_bf16_pairs(w):
    lo = pltpu.bitcast(w << 16, F32)
    hi = pltpu.bitcast(w & jnp.uint32(0xFFFF0000), F32)
    return jnp.concatenate([lo, hi], axis=1)


def _route(logits, rb_ref, tri_ref, carry_ref):
    lt = logits.T
    score = [jax.nn.sigmoid(lt[e:e + 1, :]) for e in range(N_EXPERTS)]
    choice = [score[e] + rb_ref[e:e + 1, :] for e in range(N_EXPERTS)]
    best = jnp.zeros_like(score[0], dtype=jnp.int32)
    best_v = None
    for gi in range(N_EXPERT_GROUPS):
        c = choice[gi * EXPERTS_PER_GROUP:(gi + 1) * EXPERTS_PER_GROUP]
        top2 = None
        for a in range(EXPERTS_PER_GROUP):
            for b in range(a + 1, EXPERTS_PER_GROUP):
                pair = c[a] + c[b]
                top2 = pair if top2 is None else jnp.maximum(top2, pair)
        if best_v is None:
            best_v = top2
        else:
            better = top2 > best_v
            best = jnp.where(better, gi, best)
            best_v = jnp.where(better, top2, best_v)
    sel = []
    picked = []
    for e in range(N_EXPERTS):
        gi = e // EXPERTS_PER_GROUP
        rank = jnp.zeros_like(best)
        for o in range(gi * EXPERTS_PER_GROUP, (gi + 1) * EXPERTS_PER_GROUP):
            if o == e:
                continue
            ahead = (choice[o] > choice[e]) | ((choice[o] == choice[e]) & (o < e))
            rank = rank + ahead.astype(jnp.int32)
        chosen = (best == gi) & (rank < 2)
        sel.append(jnp.where(chosen, 1.0, 0.0))
        picked.append(jnp.where(chosen, score[e], 0.0))
    total = picked[0]
    for e in range(1, N_EXPERTS):
        total = total + picked[e]
    inv = 1.0 / total
    gate = [w * inv for w in picked]

    member = [sel[BUCKET_A[k]] * sel[BUCKET_B[k]] for k in range(N_BUCKETS)]
    bucket = member[1]
    wa = member[0] * gate[BUCKET_A[0]]
    wb = member[0] * gate[BUCKET_B[0]]
    for k in range(1, N_BUCKETS):
        if k > 1:
            bucket = bucket + float(k) * member[k]
        wa = wa + member[k] * gate[BUCKET_A[k]]
        wb = wb + member[k] * gate[BUCKET_B[k]]
    tm = bucket.shape[1]
    onehot = jnp.concatenate(member + [jnp.zeros((BUCKET_ROWS - N_BUCKETS, tm), F32)], axis=0)
    earlier = jnp.dot(onehot.astype(BF16), tri_ref[...], preferred_element_type=F32)
    carry = carry_ref[...]
    rank = jnp.sum(onehot * (earlier + carry[:, 0:1]), axis=0, keepdims=True)
    carry_ref[...] = carry + jnp.sum(onehot, axis=1, keepdims=True)
    return bucket.astype(I32), rank.astype(I32), wa, wb


SLAB = 32


def _row_slabs(n_rows, slab, body):
    for i in range(n_rows // slab):
        body(pl.ds(i * slab, slab))


def _residual_router(get_x, acc_ref, g1_ref, nf_ref, sc2_ref, sh2_ref, wr_ref, rb_ref, tri_ref,
                     x1_ref, rows_ref, meta_ref, count_ref, carry_ref, hhi_ref, hlo_ref):
    @pl.when(pl.program_id(0) == 0)
    def _():
        carry_ref[...] = jnp.zeros_like(carry_ref)

    g1 = g1_ref[...]
    gain = nf_ref[...] * (1.0 + sc2_ref[...])
    shift = sh2_ref[...]

    def slab(rows):
        x1 = get_x(rows) + g1 * acc_ref[rows, :]
        x1_ref[rows, :] = x1
        h2 = x1 * lax.rsqrt(jnp.mean(x1 * x1, axis=-1, keepdims=True) + EPS) * gain + shift
        hi = h2.astype(BF16)
        hi32 = hi.astype(F32)
        hhi_ref[rows, :] = hi
        hlo_ref[rows, :] = (h2 - hi32).astype(BF16)
        bits = pltpu.bitcast(hi32, U32)
        rows_ref[rows, :HALF] = (bits[:, HALF:] & jnp.uint32(0xFFFF0000)) | (bits[:, :HALF] >> 16)

    _row_slabs(acc_ref.shape[0], SLAB, slab)
    both = jnp.dot(hhi_ref[...], wr_ref[...], preferred_element_type=F32)
    logits = (both[:, :LANE] + both[:, LANE:]
              + jnp.dot(hlo_ref[...], wr_ref[:, :LANE], preferred_element_type=F32))
    bucket, rank, wa, wb = _route(logits, rb_ref, tri_ref, carry_ref)
    tm = logits.shape[0]
    gates_t = jnp.concatenate([wa, wb, jnp.zeros((LANE - 2, tm), F32)], axis=0)
    rows_ref[:, HALF:] = pltpu.bitcast(gates_t.T, U32)
    meta_ref[...] = jnp.concatenate([bucket, rank, jnp.zeros((6, tm), I32)], axis=0)
    count_ref[...] = carry_ref[...]


def _out0_kernel(attc_ref, attl_ref, mixc_ref, mixl_ref, wf_ref, woa_ref, wof_ref, xp_ref, xs_ref, g1_ref, nf_ref,
                 sc2_ref, sh2_ref, wr_ref, rb_ref, tri_ref, x1_ref, rows_ref, meta_ref, count_ref,
                 carry_ref, hhi_ref, hlo_ref, acc_ref):
    is_lat = pl.program_id(0) >= N_CTX_TILES
    att = jnp.where(is_lat, attl_ref[...], attc_ref[...])
    mix = jnp.where(is_lat, mixl_ref[...], mixc_ref[...])
    four = jnp.dot(mix.astype(BF16), wf_ref[...], preferred_element_type=F32)
    acc_ref[...] = (jnp.dot(att, woa_ref[...], preferred_element_type=F32)
                    + jnp.dot(four.astype(BF16), wof_ref[...], preferred_element_type=F32))

    def get_x(rows):
        return jnp.where(is_lat, xs_ref[rows, :], xp_ref[rows, :])

    _residual_router(get_x, acc_ref, g1_ref, nf_ref, sc2_ref, sh2_ref, wr_ref, rb_ref, tri_ref,
                     x1_ref, rows_ref, meta_ref, count_ref, carry_ref, hhi_ref, hlo_ref)


def _router_operands(w_router, router_bias):
    wr = jnp.zeros((D_MODEL, LANE), F32).at[:, :N_EXPERTS].set(w_router)
    wr_hi = wr.astype(BF16)
    wr_lo = (wr - wr_hi.astype(F32)).astype(BF16)
    rb = jnp.broadcast_to(router_bias.astype(F32)[:, None], (N_EXPERTS, TM))
    tri = jnp.asarray(np.triu(np.ones((TM, TM), np.float32), 1), BF16)
    return jnp.concatenate([wr_hi, wr_lo], axis=1), rb, tri


_EPILOGUE_OUT_SPECS = [_row_spec(D_MODEL), _row_spec(ROW_WORDS), pl.BlockSpec((8, TM), lambda i: (0, i)),
                       _const_spec((BUCKET_ROWS, LANE))]
_EPILOGUE_OUT_SHAPE = [jax.ShapeDtypeStruct((NT, D_MODEL), F32), jax.ShapeDtypeStruct((NT, ROW_WORDS), U32),
                       jax.ShapeDtypeStruct((8, NT), I32), jax.ShapeDtypeStruct((BUCKET_ROWS, LANE), F32)]
_EPILOGUE_SCRATCH = [pltpu.VMEM((BUCKET_ROWS, LANE), F32), pltpu.VMEM((TM, D_MODEL), BF16),
                     pltpu.VMEM((TM, D_MODEL), BF16), pltpu.VMEM((TM, D_MODEL), F32)]


def _epilogue_specs(layer):
    return [_mod_spec(layer, 2, TM), _const_spec((1, D_MODEL)), _mod_spec(layer, 4, TM),
            _mod_spec(layer, 3, TM), _const_spec((D_MODEL, 2 * LANE)), _const_spec((N_EXPERTS, TM)),
            _const_spec((TM, TM))]


def _out0(att, mixed, xp, xs, mods, w_fourier, w_out, norm_ffn0, router):
    wf = jnp.zeros((FOURIER_WIDTH, FOURIER_WIDTH), F32)
    for gi in range(N_FGROUPS):
        sl = slice(gi * FGROUP_DIM, (gi + 1) * FGROUP_DIM)
        wf = wf.at[sl, sl].set(w_fourier[gi])
    w_out = w_out.astype(BF16)
    wr, rb, tri = router
    return pl.pallas_call(
        _out0_kernel,
        grid=(NT // TM,),
        in_specs=_split_row_specs(ATT_WIDTH) + _split_row_specs(FOURIER_WIDTH)
        + [_const_spec((FOURIER_WIDTH, FOURIER_WIDTH)), _const_spec((ATT_WIDTH, D_MODEL)),
           _const_spec((FOURIER_WIDTH, D_MODEL))]
        + _split_row_specs(D_MODEL) + _epilogue_specs(0),
        out_specs=_EPILOGUE_OUT_SPECS,
        out_shape=_EPILOGUE_OUT_SHAPE,
        scratch_shapes=_EPILOGUE_SCRATCH,
        compiler_params=_cparams(("arbitrary",)),
        name="out_proj_att",
    )(att[0], att[1], mixed[0], mixed[1], wf.astype(BF16), w_out[:ATT_WIDTH], w_out[ATT_WIDTH:], xp, xs, mods,
      norm_ffn0.reshape(1, D_MODEL), mods, mods, wr, rb, tri)


def _dispatch_tables(meta, counts):
    bucket, rank = meta[0], meta[1]
    cnt = counts[:N_BUCKETS, 0].astype(I32)
    padded = (cnt + (TE - 1)) // TE * TE
    ends = jnp.cumsum(padded)
    starts = ends - padded
    kk = jnp.arange(N_BUCKETS, dtype=I32)
    pos = rank + jnp.sum(jnp.where(bucket[None, :] == kk[:, None], starts[:, None], 0), axis=0)
    tile0 = jnp.arange(N_ETILES, dtype=I32) * TE
    used = tile0 < ends[-1]
    tb = jnp.sum((tile0[:, None] >= ends[None, :]).astype(I32), axis=1)
    tb = jnp.where(used, tb, jnp.max(jnp.where(used, tb, 0)))
    pick = tb[:, None] == kk[None, :]

    def per_tile(table):
        return jnp.sum(jnp.where(pick, table[None, :], 0), axis=1)

    nrow = jnp.where(used, jnp.clip(per_tile(starts + cnt) - tile0, 0, TE), 0)
    pos3 = pos.reshape(SC_WORKERS, NT // (SC_WORKERS * SC_CHUNK), SC_CHUNK)
    return pos3, per_tile(jnp.asarray(BUCKET_A, I32)), per_tile(jnp.asarray(BUCKET_B, I32)), nrow


def _sc_permute(src, pos3, n_out, scatter):
    width = src.shape[1]
    _, n_chunks, chunk = pos3.shape
    rows_per_worker = n_chunks * chunk
    mesh = plsc.VectorSubcoreMesh(core_axis_name="c", subcore_axis_name="s")

    @functools.partial(pl.kernel, out_type=jax.ShapeDtypeStruct((n_out, width), src.dtype), mesh=mesh,
                       scratch_types=[pltpu.VMEM((n_chunks, chunk), I32), pltpu.VMEM((2, chunk, width), src.dtype),
                                      pltpu.SemaphoreType.DMA((2,))])
    def permute(src_hbm, pos_hbm, out_hbm, pos_v, buf, sem):
        worker = lax.axis_index("s") * SC_CORES + lax.axis_index("c")
        base = worker * rows_per_worker
        pltpu.sync_copy(pos_hbm.at[worker], pos_v)

        def own(j):
            return pl.ds(base + j * chunk, chunk)

        def load(j):
            rows = src_hbm.at[own(j)] if scatter else src_hbm.at[pos_v.at[j]]
            return pltpu.async_copy(rows, buf.at[j % 2], sem.at[j % 2])

        pending = load(0)
        for j in range(n_chunks):
            following = load(j + 1) if j + 1 < n_chunks else None
            pending.wait()
            pltpu.sync_copy(buf.at[j % 2], out_hbm.at[pos_v.at[j]] if scatter else out_hbm.at[own(j)])
            pending = following

    return permute(src, pos3)


def _experts_kernel(ea_ref, eb_ref, nrow_ref, rows_ref, wga_ref, wua_ref, wda_ref, wgb_ref, wub_ref, wdb_ref, y_ref,
                    cga, cua, cda, cgb, cub, cdb):
    j = pl.program_id(0)
    n = nrow_ref[j]

    @pl.when(n == 0)
    def _():
        y_ref[...] = jnp.zeros_like(y_ref)

    @pl.when(n > 0)
    def _():
        prev = jnp.maximum(j - 1, 0)

        @pl.when(jnp.logical_or(j == 0, ea_ref[j] != ea_ref[prev]))
        def _():
            cga[...] = wga_ref[...].astype(BF16)
            cua[...] = wua_ref[...].astype(BF16)
            cda[...] = wda_ref[...].astype(BF16)

        @pl.when(jnp.logical_or(j == 0, eb_ref[j] != eb_ref[prev]))
        def _():
            cgb[...] = wgb_ref[...].astype(BF16)
            cub[...] = wub_ref[...].astype(BF16)
            cdb[...] = wdb_ref[...].astype(BF16)

        valid = lax.broadcasted_iota(I32, (TE, 1), 0) < n
        h = jnp.where(valid, _unpack_bf16_pairs(rows_ref[:, :HALF]), 0.0).astype(BF16)
        gates = jnp.where(valid, pltpu.bitcast(rows_ref[:, HALF:], F32), 0.0)
        y = None
        for cg, cu, cd, col in ((cga, cua, cda, 0), (cgb, cub, cdb, 1)):
            a = jnp.dot(h, cg[...], preferred_element_type=F32)
            u = jnp.dot(h, cu[...], preferred_element_type=F32)
            hid = _silu(a) * u * gates[:, col:col + 1]
            o = jnp.dot(hid.astype(BF16), cd[...], preferred_element_type=F32)
            y = o if y is None else y + o
        y_ref[...] = _pack_bf16_pairs(y)


def _experts(rows_sorted, ea, eb, nrow, layer, w_gate, w_up, w_down):
    def w_spec(shape, which):
        return pl.BlockSpec((None, None) + shape, lambda j, ea, eb, nr: (layer, (ea, eb)[which][j], 0, 0))

    up_shape, down_shape = (D_MODEL, EXPERT_FF), (EXPERT_FF, D_MODEL)
    grid_spec = pltpu.PrefetchScalarGridSpec(
        num_scalar_prefetch=3,
        grid=(N_ETILES,),
        in_specs=[pl.BlockSpec((TE, ROW_WORDS), lambda j, *_: (j, 0)),
                  w_spec(up_shape, 0), w_spec(up_shape, 0), w_spec(down_shape, 0),
                  w_spec(up_shape, 1), w_spec(up_shape, 1), w_spec(down_shape, 1)],
        out_specs=pl.BlockSpec((TE, HALF), lambda j, *_: (j, 0)),
        scratch_shapes=[pltpu.VMEM(up_shape, BF16), pltpu.VMEM(up_shape, BF16), pltpu.VMEM(down_shape, BF16),
                        pltpu.VMEM(up_shape, BF16), pltpu.VMEM(up_shape, BF16), pltpu.VMEM(down_shape, BF16)])
    return pl.pallas_call(
        _experts_kernel,
        grid_spec=grid_spec,
        out_shape=jax.ShapeDtypeStruct((P_MAX, HALF), U32),
        compiler_params=_cparams(("arbitrary",)),
        name="experts_layer%d" % layer,
    )(ea, eb, nrow, rows_sorted, w_gate, w_up, w_down, w_gate, w_up, w_down)


def _moe(rows, meta, counts, layer, w_gate, w_up, w_down):
    pos3, ea, eb, nrow = _dispatch_tables(meta, counts)
    rows_sorted = _sc_permute(rows, pos3, P_MAX, scatter=True)
    y_sorted = _experts(rows_sorted, ea, eb, nrow, layer, w_gate, w_up, w_down)
    return _sc_permute(y_sorted, pos3, NT, scatter=False)


def _final_kernel(x_ref, y_ref, g2_ref, nfin_ref, yp_ref, ys_ref):
    i = pl.program_id(0)
    x = x_ref[...] + g2_ref[...] * _unpack_bf16_pairs(y_ref[...])
    out = x * lax.rsqrt(jnp.mean(x * x, axis=-1, keepdims=True) + EPS) * nfin_ref[...]

    @pl.when(i < N_CTX_TILES)
    def _():
        yp_ref[...] = out

    @pl.when(i >= N_CTX_TILES)
    def _():
        ys_ref[...] = out


def _final(x1, y_tok, mods, norm_final):
    return pl.pallas_call(
        _final_kernel,
        grid=(NT // TM,),
        in_specs=[_row_spec(D_MODEL), _row_spec(HALF), _mod_spec(DEPTH - 1, 5, TM), _const_spec((1, D_MODEL))],
        out_specs=_split_row_specs(D_MODEL),
        out_shape=[jax.ShapeDtypeStruct((N_CTX, D_MODEL), F32), jax.ShapeDtypeStruct((N_LAT, D_MODEL), F32)],
        compiler_params=_cparams(("arbitrary",)),
        name="final_norm",
    )(x1, y_tok, mods, norm_final.reshape(1, D_MODEL))


HALO = 16


def _in1_kernel(x_ref, xa_ref, xb_ref, y_ref, ya_ref, yb_ref, g2_ref, nm_ref, sc_ref, sh_ref, w_ref, cw_ref, cb_ref,
                x2_ref, z_ref, xc_ref, dt_ref, h_ref):
    i = pl.program_id(0)
    g2 = g2_ref[...]
    gain = nm_ref[...] * (1.0 + sc_ref[...])
    shift = sh_ref[...]

    def normed(x_rows, y_rows):
        x = x_rows + g2 * _unpack_bf16_pairs(y_rows)
        return x, (x * lax.rsqrt(jnp.mean(x * x, axis=-1, keepdims=True) + EPS) * gain + shift).astype(BF16)

    def norm_rows(rows):
        x, h = normed(x_ref[rows, :], y_ref[rows, :])
        x2_ref[rows, :] = x
        h_ref[pl.ds(rows.start + HALO, rows.size), :] = h

    _row_slabs(TM, SLAB, norm_rows)
    h_ref[0:HALO, :] = normed(xa_ref[...], ya_ref[...])[1]
    h_ref[HALO + TM:, :] = normed(xb_ref[...], yb_ref[...])[1]

    step = 512
    nt_dims = (((1,), (1,)), ((), ()))

    def proj(h, c0, width):
        return lax.dot_general(h, w_ref[c0:c0 + width, :], nt_dims, preferred_element_type=F32)

    h = h_ref[HALO:HALO + TM, :]
    for c0 in range(0, SSM_INNER, step):
        z_ref[:, c0:c0 + step] = proj(h, c0, step).astype(BF16)
    n_dt = 2 * SSM_HEADS
    dt_ref[:, :n_dt] = proj(h, SSM_INNER + SSM_CONV_CH, n_dt)
    dt_ref[:, n_dt:] = jnp.zeros((TM, LANE - n_dt), F32)

    row = lax.broadcasted_iota(jnp.int32, (TM, LANE), 0)
    is_ctx = i < N_CTX_TILES
    j = (i - N_CTX_TILES) % LAT_TILES_PER_SEQ
    in_seq = row & (SEQ - 1)
    lat_first = (j == 0).astype(I32)
    lat_last = (j == LAT_TILES_PER_SEQ - 1).astype(I32)
    starts = jnp.where(is_ctx, (in_seq == 0).astype(I32), (row == 0).astype(I32) * lat_first) != 0
    ends = jnp.where(is_ctx, (in_seq == SEQ - 1).astype(I32), (row == TM - 1).astype(I32) * lat_last) != 0
    h_ext = h_ref[...]
    n_ext = TM + 2 * HALO
    for c0 in range(0, SSM_CONV_CH, step):
        xe = proj(h_ext, SSM_INNER + c0, step)
        for t in range(step // LANE):
            cols = slice(c0 + t * LANE, c0 + (t + 1) * LANE)
            x = xe[:, t * LANE:(t + 1) * LANE]
            above = jnp.where(starts, 0.0, pltpu.roll(x, 1, 0)[HALO:HALO + TM])
            below = jnp.where(ends, 0.0, pltpu.roll(x, n_ext - 1, 0)[HALO:HALO + TM])
            y = (above * cw_ref[0:1, cols] + x[HALO:HALO + TM] * cw_ref[1:2, cols] + below * cw_ref[2:3, cols]
                 + cb_ref[:, cols])
            xc_ref[:, cols] = _silu(y).astype(BF16)


def _in1(x1, y_tok, mods, norm_mix1, w_in, conv_w, conv_b):
    w = jnp.swapaxes(w_in, 0, 1).astype(BF16)
    per_tile = TM // HALO
    last = NT // HALO - 1

    def above(i):
        return (jnp.maximum(i * per_tile - 1, 0), 0)

    def below(i):
        return (jnp.minimum((i + 1) * per_tile, last), 0)

    return pl.pallas_call(
        _in1_kernel,
        grid=(NT // TM,),
        in_specs=[_row_spec(D_MODEL), pl.BlockSpec((HALO, D_MODEL), above), pl.BlockSpec((HALO, D_MODEL), below),
                  _row_spec(HALF), pl.BlockSpec((HALO, HALF), above), pl.BlockSpec((HALO, HALF), below),
                  _mod_spec(0, 5, TM), _const_spec((1, D_MODEL)),
                  _mod_spec(1, 1, TM), _mod_spec(1, 0, TM), _const_spec((ODD_IN, D_MODEL)),
                  _const_spec((3, SSM_CONV_CH)), _const_spec((1, SSM_CONV_CH))],
        out_specs=[_row_spec(D_MODEL), _row_spec(SSM_INNER), _row_spec(SSM_CONV_CH), _row_spec(LANE)],
        out_shape=[jax.ShapeDtypeStruct((NT, D_MODEL), F32), jax.ShapeDtypeStruct((NT, SSM_INNER), BF16),
                   jax.ShapeDtypeStruct((NT, SSM_CONV_CH), BF16), jax.ShapeDtypeStruct((NT, LANE), F32)],
        scratch_shapes=[pltpu.VMEM((TM + 2 * HALO, D_MODEL), BF16)],
        compiler_params=_cparams(("arbitrary",)),
        name="in_proj_ssm",
    )(x1, x1, x1, y_tok, y_tok, y_tok, mods, norm_mix1.reshape(1, D_MODEL), mods, mods, w, conv_w,
      conv_b.reshape(1, SSM_CONV_CH))


HPG = SSM_HEADS // SSM_GROUPS
GW = HPG * SSM_HEAD_DIM


def _ssd_kernel(*refs, has_init, write_state):
    refs = list(refs)
    xcf_ref, xcb_ref, dtf_ref, dtb_ref, bias_ref, alog_ref = refs[:6]
    refs = refs[6:]
    h0f_ref, h0b_ref = (refs.pop(0), refs.pop(0)) if has_init else (None, None)
    yf_ref, yb_ref = refs.pop(0), refs.pop(0)
    houtf_ref, houtb_ref = (refs.pop(0), refs.pop(0)) if write_state else (None, None)
    hf_ref, hb_ref = refs
    c = pl.program_id(1)

    @pl.when(c == 0)
    def _():
        for h_ref, h0_ref in ((hf_ref, h0f_ref), (hb_ref, h0b_ref)):
            for g in range(SSM_GROUPS):
                if has_init:
                    h_ref[g] = h0_ref[g * GW:(g + 1) * GW, :].T
                else:
                    h_ref[g] = jnp.zeros((SSM_STATE, GW), F32)

    _ssd_chunk(xcf_ref, dtf_ref, bias_ref, alog_ref, yf_ref, hf_ref, False)
    _ssd_chunk(xcb_ref, dtb_ref, bias_ref, alog_ref, yb_ref, hb_ref, True)

    if write_state:
        @pl.when(c == pl.num_programs(1) - 1)
        def _():
            for h_ref, hout_ref in ((hf_ref, houtf_ref), (hb_ref, houtb_ref)):
                for g in range(SSM_GROUPS):
                    hout_ref[g * GW:(g + 1) * GW, :] = h_ref[g].T


def _ssd_chunk(xc_ref, dt_ref, bias_ref, alog_ref, y_ref, h_ref, reverse):
    col0 = SSM_HEADS if reverse else 0
    Q = SSM_CHUNK
    dt = jax.nn.softplus(dt_ref[...] + bias_ref[...])
    a = dt * -jnp.exp(alog_ref[...])
    row = lax.broadcasted_iota(jnp.int32, (Q, Q), 0)
    col = lax.broadcasted_iota(jnp.int32, (Q, Q), 1)
    keep = (col >= row) if reverse else (col <= row)
    tri = keep.astype(F32)
    acs = jnp.dot(tri, a, precision=HIGHEST, preferred_element_type=F32)
    edge = (0 if reverse else Q - 1)
    acs_end = acs[edge:edge + 1, :]
    log2e = math.log2(math.e)
    acs2 = acs * log2e
    src_t = (acs2 - jnp.log2(dt)).T
    out_t = (jnp.exp(acs_end - acs) * dt).T
    chunk_decay = jnp.exp(jnp.broadcast_to(acs_end, (8, LANE)))
    nt_dims = (((1,), (1,)), ((), ()))
    lane = lax.broadcasted_iota(jnp.int32, (Q, LANE), 1)
    low = lane < SSM_HEAD_DIM
    low8 = low[:8]

    def two_heads(v):
        zero = jnp.zeros_like(v)
        return jnp.concatenate([jnp.where(low, v, zero), jnp.where(low, zero, v)], axis=0)

    for g in range(SSM_GROUPS):
        bg = xc_ref[:, SSM_INNER + g * SSM_STATE:SSM_INNER + (g + 1) * SSM_STATE]
        cg = xc_ref[:, SSM_INNER + SSM_GROUPS * SSM_STATE + g * SSM_STATE:
                    SSM_INNER + SSM_GROUPS * SSM_STATE + (g + 1) * SSM_STATE]
        cb = lax.dot_general(cg, bg, nt_dims, preferred_element_type=F32).astype(BF16)
        bg_t = bg.astype(F32).T
        for j in range(HPG // 2):
            pair = slice(g * GW + j * LANE, g * GW + (j + 1) * LANE)
            mats, c_in, b_out, cdec = [], [], [], []
            for hh in (2 * j, 2 * j + 1):
                cidx = col0 + g * HPG + hh
                to_l = jnp.broadcast_to(acs2[:, cidx:cidx + 1], (Q, Q))
                lmat = jnp.where(keep, jnp.exp2(to_l - src_t[cidx:cidx + 1, :]), 0.0)
                mats.append(cb * lmat.astype(BF16))
                c_in.append(cg * jnp.exp2(to_l).astype(BF16))
                b_out.append((bg_t * out_t[cidx:cidx + 1, :]).astype(BF16))
                cdec.append(jnp.broadcast_to(chunk_decay[:, cidx:cidx + 1], (8, LANE)))
            hp = h_ref[g, :, j * LANE:(j + 1) * LANE]
            x2 = two_heads(xc_ref[:, pair])
            rhs = jnp.concatenate([x2, two_heads(hp.astype(BF16))], axis=0)
            y = jnp.dot(jnp.concatenate(mats + c_in, axis=1), rhs, preferred_element_type=F32)
            y_ref[:, pair] = y.astype(BF16)
            st = jnp.dot(jnp.concatenate(b_out, axis=1), x2, preferred_element_type=F32)
            h_ref[g, :, j * LANE:(j + 1) * LANE] = hp * jnp.where(low8, cdec[0], cdec[1])[0:1, :] + st


def _ssd(xc, dt_raw, bias128, alog128, h0_fwd, h0_bwd):
    outs = []
    for (row0, nseq, seqlen, has_init, write_state) in ((0, BATCH, SEQ, False, True),
                                                         (N_CTX, DEC_BATCH, DEC_SEQ, True, False)):
        nc = seqlen // SSM_CHUNK
        base = row0 // SSM_CHUNK

        def fwd_in(s, c, nc=nc, base=base):
            return (base + s * nc + c, 0)

        def bwd_in(s, c, nc=nc, base=base):
            return (base + s * nc + nc - 1 - c, 0)

        def fwd_out(s, c, nc=nc):
            return (s * nc + c, 0)

        def bwd_out(s, c, nc=nc):
            return (s * nc + nc - 1 - c, 0)

        state_spec = pl.BlockSpec((None, SSM_INNER, SSM_STATE), lambda s, c: (s, 0, 0))
        y_shape = jax.ShapeDtypeStruct((nseq * seqlen, SSM_INNER), BF16)
        state_shape = jax.ShapeDtypeStruct((nseq, SSM_INNER, SSM_STATE), F32)
        in_specs = [pl.BlockSpec((SSM_CHUNK, SSM_CONV_CH), fwd_in), pl.BlockSpec((SSM_CHUNK, SSM_CONV_CH), bwd_in),
                    pl.BlockSpec((SSM_CHUNK, LANE), fwd_in), pl.BlockSpec((SSM_CHUNK, LANE), bwd_in),
                    _const_spec((1, LANE)), _const_spec((1, LANE))]
        args = [xc, xc, dt_raw, dt_raw, bias128, alog128]
        if has_init:
            in_specs += [state_spec, state_spec]
            args += [h0_fwd, h0_bwd]
        out_specs = [pl.BlockSpec((SSM_CHUNK, SSM_INNER), fwd_out), pl.BlockSpec((SSM_CHUNK, SSM_INNER), bwd_out)]
        out_shape = [y_shape, y_shape]
        if write_state:
            out_specs += [state_spec, state_spec]
            out_shape += [state_shape, state_shape]
        state_scratch = pltpu.VMEM((SSM_GROUPS, SSM_STATE, GW), F32)
        outs.append(pl.pallas_call(
            functools.partial(_ssd_kernel, has_init=has_init, write_state=write_state),
            grid=(nseq, nc),
            in_specs=in_specs,
            out_specs=out_specs,
            out_shape=out_shape,
            scratch_shapes=[state_scratch, state_scratch],
            compiler_params=_cparams(("arbitrary", "arbitrary")),
            name="ssd_%s" % ("lat" if has_init else "ctx"),
        )(*args))
    ctx, lat = outs
    return (ctx[0], lat[0]), (ctx[1], lat[1]), ctx[2], ctx[3]


def _out1_kernel(yfc_ref, yfl_ref, ybc_ref, ybl_ref, xs_ref, z_ref, dskip_ref, ng_ref, wo_ref, x_ref, g1_ref, nf_ref,
                 sc2_ref, sh2_ref, wr_ref, rb_ref, tri_ref, x1_ref, rows_ref, meta_ref, count_ref,
                 carry_ref, hhi_ref, hlo_ref, acc_ref, y_ref):
    is_lat = pl.program_id(0) >= N_CTX_TILES
    dskip = dskip_ref[...]
    gain = ng_ref[...]

    def gate_norm(rows):
        yf = jnp.where(is_lat, yfl_ref[rows, :], yfc_ref[rows, :])
        yb = jnp.where(is_lat, ybl_ref[rows, :], ybc_ref[rows, :])
        y = yf.astype(F32) + yb.astype(F32) + dskip * xs_ref[rows, :].astype(F32)
        y = y * _silu(z_ref[rows, :].astype(F32))
        y_ref[rows, :] = (y * lax.rsqrt(jnp.mean(y * y, axis=-1, keepdims=True) + EPS) * gain).astype(BF16)

    _row_slabs(TM, SLAB // 2, gate_norm)
    acc_ref[...] = jnp.dot(y_ref[...], wo_ref[...], preferred_element_type=F32)
    _residual_router(lambda rows: x_ref[rows, :], acc_ref, g1_ref, nf_ref, sc2_ref, sh2_ref, wr_ref, rb_ref,
                     tri_ref, x1_ref, rows_ref, meta_ref, count_ref, carry_ref, hhi_ref, hlo_ref)


def _out1(yf, yb, xc, z, d_skip, ssm_norm, w_out, x, mods, norm_ffn1, router):
    wr, rb, tri = router
    return pl.pallas_call(
        _out1_kernel,
        grid=(NT // TM,),
        in_specs=_split_row_specs(SSM_INNER) + _split_row_specs(SSM_INNER)
        + [_row_spec(SSM_INNER), _row_spec(SSM_INNER),
           _const_spec((1, SSM_INNER)), _const_spec((1, SSM_INNER)), _const_spec((SSM_INNER, D_MODEL)),
           _row_spec(D_MODEL)] + _epilogue_specs(1),
        out_specs=_EPILOGUE_OUT_SPECS,
        out_shape=_EPILOGUE_OUT_SHAPE,
        scratch_shapes=_EPILOGUE_SCRATCH + [pltpu.VMEM((TM, SSM_INNER), BF16)],
        compiler_params=_cparams(("arbitrary",)),
        name="out_proj_ssm",
    )(yf[0], yf[1], yb[0], yb[1], xc, z, jnp.repeat(d_skip.astype(F32), SSM_HEAD_DIM).reshape(1, SSM_INNER),
      ssm_norm.reshape(1, SSM_INNER), w_out.astype(BF16), x, mods, norm_ffn1.reshape(1, D_MODEL), mods, mods,
      wr, rb, tri)


def _kv_from_lane_major(t):
    return t.reshape(BATCH, 1, N_KV_HEADS, HEAD_DIM, SEQ).transpose(0, 1, 4, 2, 3)


def kernel(x_prompt, x_sample, cache_k, cache_v, state_fwd, state_bwd, c, c_ctx, w_ada, b_ada, norm_mix, norm_ffn,
           w_in_att, q_gain, k_gain, w_fourier, w_out_att, w_in_ssm, conv_w, conv_b, dt_bias_f, dt_bias_b, a_log_f,
           a_log_b, d_skip, ssm_norm, w_out_ssm, w_router, router_bias, w_gate, w_up, w_down, norm_final):
    xp = x_prompt.reshape(N_CTX, D_MODEL)
    xs = x_sample.reshape(N_LAT, D_MODEL)
    cond8 = jnp.zeros((8, D_MODEL), F32).at[0].set(c_ctx).at[1:1 + DEC_BATCH].set(c)
    mods = _ada_mods(cond8, w_ada, b_ada)
    router = _router_operands(w_router, router_bias)
    cos2, sin2 = _rope_tables()

    q, kp, vp, u, new_k, new_v = _in0(xp, xs, mods, norm_mix[0], w_in_att[0], q_gain[0], k_gain[0], cos2, sin2)
    att = _attention(q, kp, vp, cache_k[:, 0], cache_v[:, 0])
    mixed = _fourier(u)
    x1, rows, meta, counts = _out0(att, mixed, xp, xs, mods, w_fourier[0], w_out_att[0], norm_ffn[0], router)
    y_tok = _moe(rows, meta, counts, 0, w_gate, w_up, w_down)

    x2, z, xc, dt_raw = _in1(x1, y_tok, mods, norm_mix[1], w_in_ssm[0], conv_w[0], conv_b[0])
    pad = jnp.zeros((LANE - 2 * SSM_HEADS,), F32)
    bias128 = jnp.concatenate([dt_bias_f[0], dt_bias_b[0], pad]).astype(F32).reshape(1, LANE)
    alog128 = jnp.concatenate([a_log_f[0], a_log_b[0], pad]).astype(F32).reshape(1, LANE)
    yf, yb, sf, sb = _ssd(xc, dt_raw, bias128, alog128, state_fwd.reshape(DEC_BATCH, SSM_INNER, SSM_STATE),
                          state_bwd.reshape(DEC_BATCH, SSM_INNER, SSM_STATE))
    x3, rows, meta, counts = _out1(yf, yb, xc, z, d_skip[0], ssm_norm[0], w_out_ssm[0], x2, mods, norm_ffn[1], router)
    y_tok = _moe(rows, meta, counts, 1, w_gate, w_up, w_down)
    y_prompt, y_sample = _final(x3, y_tok, mods, norm_final)

    state_shape = (BATCH, 1, SSM_HEADS, SSM_HEAD_DIM, SSM_STATE)
    return (y_prompt.reshape(BATCH, SEQ, D_MODEL), y_sample.reshape(DEC_BATCH, DEC_SEQ, D_MODEL),
            _kv_from_lane_major(new_k), _kv_from_lane_major(new_v),
            sf.reshape(state_shape), sb.reshape(state_shape))
```

```python
import functools
import math

import numpy as np
import jax
import jax.numpy as jnp
from jax import lax
from jax.experimental import pallas as pl
from jax.experimental.pallas import tpu as pltpu
from jax.experimental.pallas import tpu_sc as plsc

F32 = jnp.float32
BF16 = jnp.bfloat16
U32 = jnp.uint32
I32 = jnp.int32
HIGHEST = lax.Precision.HIGHEST

D_MODEL = 1024
BATCH = 32
SEQ = 256
DEPTH = 2
DEC_BATCH = 2
DEC_SEQ = 4096
PAST_LEN = 256
GRID_W = 64
EPS = 1e-6
N_HEADS = 8
N_KV_HEADS = 2
HEAD_DIM = 64
ATT_WIDTH = N_HEADS * HEAD_DIM
KV_WIDTH = N_KV_HEADS * HEAD_DIM
ROPE_THETA = 10000.0
N_FGROUPS = 8
FGROUP_DIM = 64
FOURIER_WIDTH = N_FGROUPS * FGROUP_DIM
EVEN_IN = ATT_WIDTH + 2 * KV_WIDTH + FOURIER_WIDTH
SSM_INNER = 2 * D_MODEL
SSM_HEAD_DIM = 64
SSM_HEADS = SSM_INNER // SSM_HEAD_DIM
SSM_GROUPS = 4
SSM_STATE = 128
SSM_CHUNK = 128
SSM_CONV_CH = SSM_INNER + 2 * SSM_GROUPS * SSM_STATE
ODD_IN = SSM_INNER + SSM_CONV_CH + 2 * SSM_HEADS
N_EXPERTS = 16
EXPERTS_PER_GROUP = 4
N_EXPERT_GROUPS = 4
EXPERT_FF = 512

N_CTX = BATCH * SEQ
N_LAT = DEC_BATCH * DEC_SEQ
NT = N_CTX + N_LAT
N_SEG = 1 + DEC_BATCH
LANE = 128
VMEM_LIMIT = 56 * 1024 * 1024

TM = 512
N_CTX_TILES = N_CTX // TM
LAT_TILES_PER_SEQ = DEC_SEQ // TM

PAIR_ORDER = ((0, 1), (0, 2), (0, 3), (1, 3), (1, 2), (3, 2))
N_BUCKETS = N_EXPERT_GROUPS * len(PAIR_ORDER)
BUCKET_A = tuple(g * EXPERTS_PER_GROUP + a for g in range(N_EXPERT_GROUPS) for a, _ in PAIR_ORDER)
BUCKET_B = tuple(g * EXPERTS_PER_GROUP + b for g in range(N_EXPERT_GROUPS) for _, b in PAIR_ORDER)
BUCKET_ROWS = 32
TE = 256
N_ETILES = (NT + N_BUCKETS * (TE - 1) + TE - 1) // TE
P_MAX = N_ETILES * TE
HALF = D_MODEL // 2
ROW_WORDS = HALF + LANE
SC_CORES = 2
SC_SUBCORES = 16
SC_WORKERS = SC_CORES * SC_SUBCORES
SC_CHUNK = 64


def _cparams(sem):
    return pltpu.CompilerParams(dimension_semantics=sem, vmem_limit_bytes=VMEM_LIMIT)


def _seg_of_tile(i, tm):
    nct = N_CTX // tm
    return jnp.where(i < nct, 0, 1 + (i - nct) // (DEC_SEQ // tm))


def _mod_spec(layer, which, tm):
    return pl.BlockSpec((None, None, None, 1, D_MODEL),
                        lambda i, *_: (layer, which, _seg_of_tile(i, tm), 0, 0))


def _row_spec(width, tm=TM):
    return pl.BlockSpec((tm, width), lambda i, *_: (i, 0))


def _const_spec(shape):
    nd = len(shape)
    return pl.BlockSpec(shape, lambda *_: (0,) * nd)


def _silu(x):
    return x * jax.nn.sigmoid(x)


def _ada_kernel(cond_ref, w_ref, b_ref, o_ref):
    s = _silu(cond_ref[...])
    w = w_ref[...]
    s_hi = s.astype(BF16)
    w_hi = w.astype(BF16)
    s_lo = (s - s_hi.astype(F32)).astype(BF16)
    w_lo = (w - w_hi.astype(F32)).astype(BF16)
    o_ref[...] = (jnp.dot(s_hi, w_hi, preferred_element_type=F32) + jnp.dot(s_hi, w_lo, preferred_element_type=F32)
                  + jnp.dot(s_lo, w_hi, preferred_element_type=F32) + b_ref[...])


def _ada_mods(cond8, w_ada, b_ada):
    tn = 1536
    out = pl.pallas_call(
        _ada_kernel,
        grid=(DEPTH, 6 * D_MODEL // tn),
        in_specs=[pl.BlockSpec((8, D_MODEL), lambda l, n: (0, 0)),
                  pl.BlockSpec((None, D_MODEL, tn), lambda l, n: (l, 0, n)),
                  pl.BlockSpec((None, 1, tn), lambda l, n: (l, 0, n))],
        out_specs=pl.BlockSpec((None, 8, tn), lambda l, n: (l, 0, n)),
        out_shape=jax.ShapeDtypeStruct((DEPTH, 8, 6 * D_MODEL), F32),
        compiler_params=_cparams(("arbitrary", "arbitrary")),
        name="ada_mod",
    )(cond8, w_ada, b_ada.reshape(DEPTH, 1, 6 * D_MODEL))
    return out.reshape(DEPTH, 8, 6, D_MODEL)[:, :N_SEG].transpose(0, 2, 1, 3)[:, :, :, None, :]


def _rope_tables():
    t = np.arange(DEC_SEQ)
    row = (t // GRID_W).astype(np.float64)
    col = (t % GRID_W).astype(np.float64)
    axis_dim = HEAD_DIM // 2
    freqs = ROPE_THETA ** (-np.arange(0, axis_dim, 2, dtype=np.float64) / axis_dim)
    ang = np.concatenate([row[:, None] * freqs, col[:, None] * freqs], axis=-1)
    cos = np.repeat(np.cos(ang), 2, axis=1)
    sin = np.repeat(np.sin(ang), 2, axis=1)
    sign = np.where(np.arange(HEAD_DIM) % 2 == 0, -1.0, 1.0)
    cos2 = np.tile(cos, (1, 2)).astype(np.float32)
    sin2 = np.tile(sin * sign, (1, 2)).astype(np.float32)
    return jnp.asarray(cos2), jnp.asarray(sin2)


def _dft_cos_sin(n, scale):
    k = np.arange(n)
    ang = 2.0 * np.pi * ((k[:, None] * k[None, :]) % n) / n
    return np.cos(ang) * scale, np.sin(ang) * scale


def _block_diag(m, reps):
    n = m.shape[0]
    out = np.zeros((n * reps, n * reps), m.dtype)
    for r in range(reps):
        out[r * n:(r + 1) * n, r * n:(r + 1) * n] = m
    return out


def _channel_dft():
    c, s = _dft_cos_sin(FGROUP_DIM, FGROUP_DIM ** -0.5)
    return jnp.asarray(np.concatenate([_block_diag(c, N_FGROUPS), _block_diag(s, N_FGROUPS)], axis=1), BF16)


def _group_ones(width):
    return jnp.asarray(_block_diag(np.ones((HEAD_DIM, HEAD_DIM), np.float32), width // HEAD_DIM), BF16)


def _pad_heads(x):
    lane = lax.broadcasted_iota(jnp.int32, x.shape, 1)
    low = lane < HEAD_DIM
    xr = pltpu.roll(x, HEAD_DIM, 1)
    zero = jnp.zeros_like(x)
    return [jnp.where(low, x, zero), jnp.where(low, zero, xr), jnp.where(low, xr, zero), jnp.where(low, zero, x)]


def _ctx_tile(i, *_):
    return (jnp.minimum(i, N_CTX_TILES - 1), 0)


def _lat_tile(i, *_):
    return (jnp.maximum(i - N_CTX_TILES, 0), 0)


def _split_row_specs(width):
    return [pl.BlockSpec((TM, width), _ctx_tile), pl.BlockSpec((TM, width), _lat_tile)]


def _in0_kernel(xp_ref, xs_ref, nm_ref, sc_ref, sh_ref, w_ref, qg_ref, kg_ref, ones_ref, cos_ref, sin_ref, dft_ref,
                q_ref, kp_ref, vp_ref, uc_ref, ul_ref, nk_ref, nv_ref, h_ref):
    i = pl.program_id(0)
    is_lat = i >= N_CTX_TILES
    gain = nm_ref[...] * (1.0 + sc_ref[...])
    shift = sh_ref[...]

    def norm_rows(rows):
        x = jnp.where(is_lat, xs_ref[rows, :], xp_ref[rows, :])
        h_ref[rows, :] = (x * lax.rsqrt(jnp.mean(x * x, axis=-1, keepdims=True) + EPS) * gain + shift).astype(BF16)

    _row_slabs(TM, SLAB, norm_rows)
    p = jnp.dot(h_ref[...], w_ref[...], preferred_element_type=F32)
    q = p[:, :ATT_WIDTH]
    k = p[:, ATT_WIDTH:ATT_WIDTH + KV_WIDTH]
    v = p[:, ATT_WIDTH + KV_WIDTH:ATT_WIDTH + 2 * KV_WIDTH]
    f = p[:, ATT_WIDTH + 2 * KV_WIDTH:]
    ones = ones_ref[...]
    qss = jnp.dot((q * q).astype(BF16), ones, preferred_element_type=F32)
    kss = jnp.dot((k * k).astype(BF16), ones[:KV_WIDTH, :KV_WIDTH], preferred_element_type=F32)
    qn = q * lax.rsqrt(qss * (1.0 / HEAD_DIM) + EPS) * qg_ref[...]
    kn = k * lax.rsqrt(kss * (1.0 / HEAD_DIM) + EPS) * kg_ref[...]

    cos = jnp.where(is_lat, cos_ref[...], 1.0)
    sin = jnp.where(is_lat, sin_ref[...], 0.0)
    lane = lax.broadcasted_iota(jnp.int32, (TM, LANE), 1)
    even = (lane % 2) == 0

    def rope(xc):
        swapped = jnp.where(even, pltpu.roll(xc, LANE - 1, 1), pltpu.roll(xc, 1, 1))
        return xc * cos + swapped * sin

    scale = HEAD_DIM ** -0.5 * math.log2(math.e)
    for j in range(ATT_WIDTH // LANE):
        q_ref[:, j * LANE:(j + 1) * LANE] = (rope(qn[:, j * LANE:(j + 1) * LANE]) * scale).astype(BF16)
    for j, c in enumerate(_pad_heads(rope(kn))):
        kp_ref[:, j * LANE:(j + 1) * LANE] = c.astype(BF16)
    for j, c in enumerate(_pad_heads(v)):
        vp_ref[:, j * LANE:(j + 1) * LANE] = c.astype(BF16)
    u = jnp.dot(f.astype(BF16), dft_ref[...], preferred_element_type=F32).astype(BF16)

    @pl.when(is_lat)
    def _():
        ul_ref[...] = u

    @pl.when(jnp.logical_not(is_lat))
    def _():
        uc_ref[...] = u
        for r in range(TM // SEQ):
            nk_ref[r] = kn[r * SEQ:(r + 1) * SEQ, :].T
            nv_ref[r] = v[r * SEQ:(r + 1) * SEQ, :].T


def _in0(xp, xs, mods, norm_mix0, w_in, q_gain, k_gain, cos2, sin2):
    def table_idx(i):
        return (jnp.where(i < N_CTX_TILES, 0, (i - N_CTX_TILES) % LAT_TILES_PER_SEQ), 0)

    def ctx_idx(i):
        return (jnp.minimum(i, N_CTX_TILES - 1), 0, 0)

    seqs = TM // SEQ
    outs = pl.pallas_call(
        _in0_kernel,
        grid=(NT // TM,),
        in_specs=_split_row_specs(D_MODEL) + [_const_spec((1, D_MODEL)), _mod_spec(0, 1, TM), _mod_spec(0, 0, TM),
                  _const_spec((D_MODEL, EVEN_IN)), _const_spec((1, ATT_WIDTH)), _const_spec((1, KV_WIDTH)),
                  _const_spec((ATT_WIDTH, ATT_WIDTH)),
                  pl.BlockSpec((TM, LANE), table_idx), pl.BlockSpec((TM, LANE), table_idx),
                  _const_spec((FOURIER_WIDTH, 2 * FOURIER_WIDTH))],
        out_specs=[_row_spec(ATT_WIDTH), _row_spec(4 * LANE), _row_spec(4 * LANE)]
        + _split_row_specs(2 * FOURIER_WIDTH)
        + [pl.BlockSpec((seqs, KV_WIDTH, SEQ), ctx_idx), pl.BlockSpec((seqs, KV_WIDTH, SEQ), ctx_idx)],
        out_shape=[jax.ShapeDtypeStruct((NT, ATT_WIDTH), BF16), jax.ShapeDtypeStruct((NT, 4 * LANE), BF16),
                   jax.ShapeDtypeStruct((NT, 4 * LANE), BF16),
                   jax.ShapeDtypeStruct((N_CTX, 2 * FOURIER_WIDTH), BF16),
                   jax.ShapeDtypeStruct((N_LAT, 2 * FOURIER_WIDTH), BF16),
                   jax.ShapeDtypeStruct((BATCH, KV_WIDTH, SEQ), F32), jax.ShapeDtypeStruct((BATCH, KV_WIDTH, SEQ), F32)],
        scratch_shapes=[pltpu.VMEM((TM, D_MODEL), BF16)],
        compiler_params=_cparams(("arbitrary",)),
        name="in_proj_att",
    )(xp, xs, norm_mix0.reshape(1, D_MODEL), mods, mods, w_in.astype(BF16),
      jnp.tile(q_gain, N_HEADS).reshape(1, ATT_WIDTH), jnp.tile(k_gain, N_KV_HEADS).reshape(1, KV_WIDTH),
      _group_ones(ATT_WIDTH), cos2, sin2, _channel_dft())
    return outs


def _att_kernel(*refs, has_cache):
    if has_cache:
        q_ref, kp_ref, vp_ref, ck_ref, cv_ref, o_ref = refs
        ckp = [c.astype(BF16) for c in _pad_heads(ck_ref[...])]
        cvp = [c.astype(BF16) for c in _pad_heads(cv_ref[...])]
    else:
        q_ref, kp_ref, vp_ref, o_ref = refs
    nt_dims = (((1,), (1,)), ((), ()))
    for j in range(ATT_WIDTH // LANE):
        qj = q_ref[:, j * LANE:(j + 1) * LANE]
        g = j // 2
        acc = None
        for half in range(2):
            c = 2 * g + half
            kk = kp_ref[:, c * LANE:(c + 1) * LANE]
            vv = vp_ref[:, c * LANE:(c + 1) * LANE]
            s = lax.dot_general(qj, kk, nt_dims, preferred_element_type=F32)
            m = jnp.max(s, axis=-1, keepdims=True)
            if has_cache:
                sc = lax.dot_general(qj, ckp[c], nt_dims, preferred_element_type=F32)
                m = jnp.maximum(m, jnp.max(sc, axis=-1, keepdims=True))
            p = jnp.exp2(s - m)
            d = jnp.sum(p, axis=-1, keepdims=True)
            o = jnp.dot(p.astype(BF16), vv, preferred_element_type=F32)
            if has_cache:
                pc = jnp.exp2(sc - m)
                d = d + jnp.sum(pc, axis=-1, keepdims=True)
                o = o + jnp.dot(pc.astype(BF16), cvp[c], preferred_element_type=F32)
            o = o * (1.0 / d)
            acc = o if acc is None else acc + o
        o_ref[:, j * LANE:(j + 1) * LANE] = acc.astype(BF16)


def _attention(q, kp, vp, cache_k, cache_v):
    att_ctx = pl.pallas_call(
        functools.partial(_att_kernel, has_cache=False),
        grid=(BATCH,),
        in_specs=[pl.BlockSpec((SEQ, ATT_WIDTH), lambda b: (b, 0)),
                  pl.BlockSpec((SEQ, 4 * LANE), lambda b: (b, 0)),
                  pl.BlockSpec((SEQ, 4 * LANE), lambda b: (b, 0))],
        out_specs=pl.BlockSpec((SEQ, ATT_WIDTH), lambda b: (b, 0)),
        out_shape=jax.ShapeDtypeStruct((N_CTX, ATT_WIDTH), BF16),
        compiler_params=_cparams(("arbitrary",)),
        name="attention_ctx",
    )(q, kp, vp)
    tq = 256
    off = N_CTX // DEC_SEQ
    att_lat = pl.pallas_call(
        functools.partial(_att_kernel, has_cache=True),
        grid=(DEC_BATCH, DEC_SEQ // tq),
        in_specs=[pl.BlockSpec((tq, ATT_WIDTH), lambda b, i: (N_CTX // tq + b * (DEC_SEQ // tq) + i, 0)),
                  pl.BlockSpec((DEC_SEQ, 4 * LANE), lambda b, i: (off + b, 0)),
                  pl.BlockSpec((DEC_SEQ, 4 * LANE), lambda b, i: (off + b, 0)),
                  pl.BlockSpec((None, PAST_LEN, KV_WIDTH), lambda b, i: (b, 0, 0)),
                  pl.BlockSpec((None, PAST_LEN, KV_WIDTH), lambda b, i: (b, 0, 0))],
        out_specs=pl.BlockSpec((tq, ATT_WIDTH), lambda b, i: (b * (DEC_SEQ // tq) + i, 0)),
        out_shape=jax.ShapeDtypeStruct((N_LAT, ATT_WIDTH), BF16),
        compiler_params=_cparams(("arbitrary", "arbitrary")),
        name="attention_lat",
    )(q, kp, vp, cache_k.reshape(DEC_BATCH, PAST_LEN, KV_WIDTH), cache_v.reshape(DEC_BATCH, PAST_LEN, KV_WIDTH))
    return att_ctx, att_lat


def _four_ctx_kernel(u_ref, c_ref, s_ref, o_ref):
    uc = u_ref[:, :FOURIER_WIDTH]
    us = u_ref[:, FOURIER_WIDTH:]
    o_ref[...] = (jnp.dot(c_ref[...], uc, preferred_element_type=F32)
                  - jnp.dot(s_ref[...], us, preferred_element_type=F32))


FCH = 8


def _four_lat_a_kernel(u_ref, w1_ref, w2_ref, tc_ref, ts_ref, o_ref):
    w1 = w1_ref[...]
    w2 = w2_ref[...]
    for j in range(FCH):
        uc = u_ref[:, j * 2 * FOURIER_WIDTH:j * 2 * FOURIER_WIDTH + FOURIER_WIDTH]
        us = u_ref[:, j * 2 * FOURIER_WIDTH + FOURIER_WIDTH:(j + 1) * 2 * FOURIER_WIDTH]
        z = jnp.dot(w1, uc, preferred_element_type=F32) + jnp.dot(w2, us, preferred_element_type=F32)
        zr = z[:GRID_W]
        zi = z[GRID_W:]
        tc = jnp.concatenate([tc_ref[j]] * (FOURIER_WIDTH // LANE), axis=1)
        ts = jnp.concatenate([ts_ref[j]] * (FOURIER_WIDTH // LANE), axis=1)
        o_ref[j, :, :FOURIER_WIDTH] = (zr * tc - zi * ts).astype(BF16)
        o_ref[j, :, FOURIER_WIDTH:] = (zr * ts + zi * tc).astype(BF16)


def _four_lat_b_kernel(b_ref, c_ref, s_ref, o_ref):
    c = c_ref[...]
    s = s_ref[...]
    for j in range(FCH):
        br = b_ref[:, j * 2 * FOURIER_WIDTH:j * 2 * FOURIER_WIDTH + FOURIER_WIDTH]
        bi = b_ref[:, j * 2 * FOURIER_WIDTH + FOURIER_WIDTH:(j + 1) * 2 * FOURIER_WIDTH]
        o_ref[:, j, :] = (jnp.dot(c, br, preferred_element_type=F32) - jnp.dot(s, bi, preferred_element_type=F32))


def _fourier(u_ctx, u_lat):
    c256, s256 = _dft_cos_sin(SEQ, SEQ ** -0.5)
    mixed_ctx = pl.pallas_call(
        _four_ctx_kernel,
        grid=(BATCH,),
        in_specs=[pl.BlockSpec((SEQ, 2 * FOURIER_WIDTH), lambda b: (b, 0)),
                  _const_spec((SEQ, SEQ)), _const_spec((SEQ, SEQ))],
        out_specs=pl.BlockSpec((SEQ, FOURIER_WIDTH), lambda b: (b, 0)),
        out_shape=jax.ShapeDtypeStruct((N_CTX, FOURIER_WIDTH), F32),
        compiler_params=_cparams(("arbitrary",)),
        name="fourier_ctx",
    )(u_ctx, jnp.asarray(c256, BF16), jnp.asarray(s256, BF16))

    g = GRID_W
    c64, s64 = _dft_cos_sin(g, g ** -0.5)
    w1 = jnp.asarray(np.concatenate([c64, s64], axis=0), BF16)
    w2 = jnp.asarray(np.concatenate([-s64, c64], axis=0), BF16)
    t2 = np.arange(g)[:, None]
    k1 = np.arange(g)[None, :]
    ang = 2.0 * np.pi * (t2 * k1) / (g * g)
    tw_c = jnp.asarray(np.broadcast_to(np.cos(ang)[:, :, None], (g, g, LANE)), F32)
    tw_s = jnp.asarray(np.broadcast_to(np.sin(ang)[:, :, None], (g, g, LANE)), F32)
    width = 2 * FOURIER_WIDTH
    u_lat = u_lat.reshape(DEC_BATCH, g, g * width)
    stage1 = pl.pallas_call(
        _four_lat_a_kernel,
        grid=(DEC_BATCH, g // FCH),
        in_specs=[pl.BlockSpec((None, g, FCH * width), lambda b, i: (b, 0, i)),
                  _const_spec((2 * g, g)), _const_spec((2 * g, g)),
                  pl.BlockSpec((FCH, g, LANE), lambda b, i: (i, 0, 0)),
                  pl.BlockSpec((FCH, g, LANE), lambda b, i: (i, 0, 0))],
        out_specs=pl.BlockSpec((None, FCH, g, width), lambda b, i: (b, i, 0, 0)),
        out_shape=jax.ShapeDtypeStruct((DEC_BATCH, g, g, width), BF16),
        compiler_params=_cparams(("arbitrary", "arbitrary")),
        name="fourier_lat_rows",
    )(u_lat, w1, w2, tw_c, tw_s)
    stage1 = stage1.reshape(DEC_BATCH, g, g * width)
    mixed_lat = pl.pallas_call(
        _four_lat_b_kernel,
        grid=(DEC_BATCH, g // FCH),
        in_specs=[pl.BlockSpec((None, g, FCH * width), lambda b, i: (b, 0, i)),
                  _const_spec((g, g)), _const_spec((g, g))],
        out_specs=pl.BlockSpec((None, g, FCH, FOURIER_WIDTH), lambda b, i: (b, 0, i, 0)),
        out_shape=jax.ShapeDtypeStruct((DEC_BATCH, g, g, FOURIER_WIDTH), F32),
        compiler_params=_cparams(("arbitrary", "arbitrary")),
        name="fourier_lat_cols",
    )(stage1, jnp.asarray(c64, BF16), jnp.asarray(s64, BF16))
    return mixed_ctx, mixed_lat.reshape(N_LAT, FOURIER_WIDTH)


def _pack_bf16_pairs(x):
    n = x.shape[1] // 2
    lo = pltpu.bitcast(x[:, :n].astype(BF16).astype(F32), U32)
    hi = pltpu.bitcast(x[:, n:].astype(BF16).astype(F32), U32)
    return (hi & jnp.uint32(0xFFFF0000)) | (lo >> 16)


def _unpack_bf16_pairs(w):
    lo = pltpu.bitcast(w << 16, F32)
    hi = pltpu.bitcast(w & jnp.uint32(0xFFFF0000), F32)
    return jnp.concatenate([lo, hi], axis=1)


def _route(logits, rb_ref, tri_ref, carry_ref):
    lt = logits.T
    score = [jax.nn.sigmoid(lt[e:e + 1, :]) for e in range(N_EXPERTS)]
    choice = [score[e] + rb_ref[e:e + 1, :] for e in range(N_EXPERTS)]
    best = jnp.zeros_like(score[0], dtype=jnp.int32)
    best_v = None
    for gi in range(N_EXPERT_GROUPS):
        c = choice[gi * EXPERTS_PER_GROUP:(gi + 1) * EXPERTS_PER_GROUP]
        top2 = None
        for a in range(EXPERTS_PER_GROUP):
            for b in range(a + 1, EXPERTS_PER_GROUP):
                pair = c[a] + c[b]
                top2 = pair if top2 is None else jnp.maximum(top2, pair)
        if best_v is None:
            best_v = top2
        else:
            better = top2 > best_v
            best = jnp.where(better, gi, best)
            best_v = jnp.where(better, top2, best_v)
    sel = []
    picked = []
    for e in range(N_EXPERTS):
        gi = e // EXPERTS_PER_GROUP
        rank = jnp.zeros_like(best)
        for o in range(gi * EXPERTS_PER_GROUP, (gi + 1) * EXPERTS_PER_GROUP):
            if o == e:
                continue
            ahead = (choice[o] > choice[e]) | ((choice[o] == choice[e]) & (o < e))
            rank = rank + ahead.astype(jnp.int32)
        chosen = (best == gi) & (rank < 2)
        sel.append(jnp.where(chosen, 1.0, 0.0))
        picked.append(jnp.where(chosen, score[e], 0.0))
    total = picked[0]
    for e in range(1, N_EXPERTS):
        total = total + picked[e]
    inv = 1.0 / total
    gate = [w * inv for w in picked]

    member = [sel[BUCKET_A[k]] * sel[BUCKET_B[k]] for k in range(N_BUCKETS)]
    bucket = member[1]
    wa = member[0] * gate[BUCKET_A[0]]
    wb = member[0] * gate[BUCKET_B[0]]
    for k in range(1, N_BUCKETS):
        if k > 1:
            bucket = bucket + float(k) * member[k]
        wa = wa + member[k] * gate[BUCKET_A[k]]
        wb = wb + member[k] * gate[BUCKET_B[k]]
    tm = bucket.shape[1]
    onehot = jnp.concatenate(member + [jnp.zeros((BUCKET_ROWS - N_BUCKETS, tm), F32)], axis=0)
    earlier = jnp.dot(onehot.astype(BF16), tri_ref[...], preferred_element_type=F32)
    carry = carry_ref[...]
    rank = jnp.sum(onehot * (earlier + carry[:, 0:1]), axis=0, keepdims=True)
    carry_ref[...] = carry + jnp.sum(onehot, axis=1, keepdims=True)
    return bucket.astype(I32), rank.astype(I32), wa, wb


SLAB = 32


def _row_slabs(n_rows, slab, body):
    for i in range(n_rows // slab):
        body(pl.ds(i * slab, slab))


def _residual_router(get_x, acc_ref, g1_ref, nf_ref, sc2_ref, sh2_ref, wr_ref, rb_ref, tri_ref,
                     x1_ref, rows_ref, meta_ref, count_ref, carry_ref, hhi_ref, hlo_ref):
    @pl.when(pl.program_id(0) == 0)
    def _():
        carry_ref[...] = jnp.zeros_like(carry_ref)

    g1 = g1_ref[...]
    gain = nf_ref[...] * (1.0 + sc2_ref[...])
    shift = sh2_ref[...]

    def slab(rows):
        x1 = get_x(rows) + g1 * acc_ref[rows, :]
        x1_ref[rows, :] = x1
        h2 = x1 * lax.rsqrt(jnp.mean(x1 * x1, axis=-1, keepdims=True) + EPS) * gain + shift
        hi = h2.astype(BF16)
        hi32 = hi.astype(F32)
        hhi_ref[rows, :] = hi
        hlo_ref[rows, :] = (h2 - hi32).astype(BF16)
        bits = pltpu.bitcast(hi32, U32)
        rows_ref[rows, :HALF] = (bits[:, HALF:] & jnp.uint32(0xFFFF0000)) | (bits[:, :HALF] >> 16)

    _row_slabs(acc_ref.shape[0], SLAB, slab)
    both = jnp.dot(hhi_ref[...], wr_ref[...], preferred_element_type=F32)
    logits = (both[:, :LANE] + both[:, LANE:]
              + jnp.dot(hlo_ref[...], wr_ref[:, :LANE], preferred_element_type=F32))
    bucket, rank, wa, wb = _route(logits, rb_ref, tri_ref, carry_ref)
    tm = logits.shape[0]
    gates_t = jnp.concatenate([wa, wb, jnp.zeros((LANE - 2, tm), F32)], axis=0)
    rows_ref[:, HALF:] = pltpu.bitcast(gates_t.T, U32)
    meta_ref[...] = jnp.concatenate([bucket, rank, jnp.zeros((6, tm), I32)], axis=0)
    count_ref[...] = carry_ref[...]


def _out0_kernel(attc_ref, attl_ref, mixc_ref, mixl_ref, wf_ref, woa_ref, wof_ref, xp_ref, xs_ref, g1_ref, nf_ref,
                 sc2_ref, sh2_ref, wr_ref, rb_ref, tri_ref, x1_ref, rows_ref, meta_ref, count_ref,
                 carry_ref, hhi_ref, hlo_ref, acc_ref):
    is_lat = pl.program_id(0) >= N_CTX_TILES
    att = jnp.where(is_lat, attl_ref[...], attc_ref[...])
    mix = jnp.where(is_lat, mixl_ref[...], mixc_ref[...])
    four = jnp.dot(mix.astype(BF16), wf_ref[...], preferred_element_type=F32)
    acc_ref[...] = (jnp.dot(att, woa_ref[...], preferred_element_type=F32)
                    + jnp.dot(four.astype(BF16), wof_ref[...], preferred_element_type=F32))

    def get_x(rows):
        return jnp.where(is_lat, xs_ref[rows, :], xp_ref[rows, :])

    _residual_router(get_x, acc_ref, g1_ref, nf_ref, sc2_ref, sh2_ref, wr_ref, rb_ref, tri_ref,
                     x1_ref, rows_ref, meta_ref, count_ref, carry_ref, hhi_ref, hlo_ref)


def _router_operands(w_router, router_bias):
    wr = jnp.zeros((D_MODEL, LANE), F32).at[:, :N_EXPERTS].set(w_router)
    wr_hi = wr.astype(BF16)
    wr_lo = (wr - wr_hi.astype(F32)).astype(BF16)
    rb = jnp.broadcast_to(router_bias.astype(F32)[:, None], (N_EXPERTS, TM))
    tri = jnp.asarray(np.triu(np.ones((TM, TM), np.float32), 1), BF16)
    return jnp.concatenate([wr_hi, wr_lo], axis=1), rb, tri


_EPILOGUE_OUT_SPECS = [_row_spec(D_MODEL), _row_spec(ROW_WORDS), pl.BlockSpec((8, TM), lambda i: (0, i)),
                       _const_spec((BUCKET_ROWS, LANE))]
_EPILOGUE_OUT_SHAPE = [jax.ShapeDtypeStruct((NT, D_MODEL), F32), jax.ShapeDtypeStruct((NT, ROW_WORDS), U32),
                       jax.ShapeDtypeStruct((8, NT), I32), jax.ShapeDtypeStruct((BUCKET_ROWS, LANE), F32)]
_EPILOGUE_SCRATCH = [pltpu.VMEM((BUCKET_ROWS, LANE), F32), pltpu.VMEM((TM, D_MODEL), BF16),
                     pltpu.VMEM((TM, D_MODEL), BF16), pltpu.VMEM((TM, D_MODEL), F32)]


def _epilogue_specs(layer):
    return [_mod_spec(layer, 2, TM), _const_spec((1, D_MODEL)), _mod_spec(layer, 4, TM),
            _mod_spec(layer, 3, TM), _const_spec((D_MODEL, 2 * LANE)), _const_spec((N_EXPERTS, TM)),
            _const_spec((TM, TM))]


def _out0(att, mixed, xp, xs, mods, w_fourier, w_out, norm_ffn0, router):
    wf = jnp.zeros((FOURIER_WIDTH, FOURIER_WIDTH), F32)
    for gi in range(N_FGROUPS):
        sl = slice(gi * FGROUP_DIM, (gi + 1) * FGROUP_DIM)
        wf = wf.at[sl, sl].set(w_fourier[gi])
    w_out = w_out.astype(BF16)
    wr, rb, tri = router
    return pl.pallas_call(
        _out0_kernel,
        grid=(NT // TM,),
        in_specs=_split_row_specs(ATT_WIDTH) + _split_row_specs(FOURIER_WIDTH)
        + [_const_spec((FOURIER_WIDTH, FOURIER_WIDTH)), _const_spec((ATT_WIDTH, D_MODEL)),
           _const_spec((FOURIER_WIDTH, D_MODEL))]
        + _split_row_specs(D_MODEL) + _epilogue_specs(0),
        out_specs=_EPILOGUE_OUT_SPECS,
        out_shape=_EPILOGUE_OUT_SHAPE,
        scratch_shapes=_EPILOGUE_SCRATCH,
        compiler_params=_cparams(("arbitrary",)),
        name="out_proj_att",
    )(att[0], att[1], mixed[0], mixed[1], wf.astype(BF16), w_out[:ATT_WIDTH], w_out[ATT_WIDTH:], xp, xs, mods,
      norm_ffn0.reshape(1, D_MODEL), mods, mods, wr, rb, tri)


def _dispatch_tables(meta, counts):
    bucket, rank = meta[0], meta[1]
    cnt = counts[:N_BUCKETS, 0].astype(I32)
    padded = (cnt + (TE - 1)) // TE * TE
    ends = jnp.cumsum(padded)
    starts = ends - padded
    kk = jnp.arange(N_BUCKETS, dtype=I32)
    pos = rank + jnp.sum(jnp.where(bucket[None, :] == kk[:, None], starts[:, None], 0), axis=0)
    tile0 = jnp.arange(N_ETILES, dtype=I32) * TE
    used = tile0 < ends[-1]
    tb = jnp.sum((tile0[:, None] >= ends[None, :]).astype(I32), axis=1)
    tb = jnp.where(used, tb, jnp.max(jnp.where(used, tb, 0)))
    pick = tb[:, None] == kk[None, :]

    def per_tile(table):
        return jnp.sum(jnp.where(pick, table[None, :], 0), axis=1)

    nrow = jnp.where(used, jnp.clip(per_tile(starts + cnt) - tile0, 0, TE), 0)
    pos3 = pos.reshape(SC_WORKERS, NT // (SC_WORKERS * SC_CHUNK), SC_CHUNK)
    return pos3, per_tile(jnp.asarray(BUCKET_A, I32)), per_tile(jnp.asarray(BUCKET_B, I32)), nrow


def _sc_permute(src, pos3, n_out, scatter):
    width = src.shape[1]
    _, n_chunks, chunk = pos3.shape
    rows_per_worker = n_chunks * chunk
    mesh = plsc.VectorSubcoreMesh(core_axis_name="c", subcore_axis_name="s")

    @functools.partial(pl.kernel, out_type=jax.ShapeDtypeStruct((n_out, width), src.dtype), mesh=mesh,
                       scratch_types=[pltpu.VMEM((n_chunks, chunk), I32), pltpu.VMEM((2, chunk, width), src.dtype),
                                      pltpu.SemaphoreType.DMA((2,))])
    def permute(src_hbm, pos_hbm, out_hbm, pos_v, buf, sem):
        worker = lax.axis_index("s") * SC_CORES + lax.axis_index("c")
        base = worker * rows_per_worker
        pltpu.sync_copy(pos_hbm.at[worker], pos_v)

        def own(j):
            return pl.ds(base + j * chunk, chunk)

        def load(j):
            rows = src_hbm.at[own(j)] if scatter else src_hbm.at[pos_v.at[j]]
            return pltpu.async_copy(rows, buf.at[j % 2], sem.at[j % 2])

        pending = load(0)
        for j in range(n_chunks):
            following = load(j + 1) if j + 1 < n_chunks else None
            pending.wait()
            pltpu.sync_copy(buf.at[j % 2], out_hbm.at[pos_v.at[j]] if scatter else out_hbm.at[own(j)])
            pending = following

    return permute(src, pos3)


def _experts_kernel(ea_ref, eb_ref, nrow_ref, rows_ref, wga_ref, wua_ref, wda_ref, wgb_ref, wub_ref, wdb_ref, y_ref,
                    cga, cua, cda, cgb, cub, cdb):
    j = pl.program_id(0)
    n = nrow_ref[j]

    @pl.when(n == 0)
    def _():
        y_ref[...] = jnp.zeros_like(y_ref)

    @pl.when(n > 0)
    def _():
        prev = jnp.maximum(j - 1, 0)

        @pl.when(jnp.logical_or(j == 0, ea_ref[j] != ea_ref[prev]))
        def _():
            cga[...] = wga_ref[...].astype(BF16)
            cua[...] = wua_ref[...].astype(BF16)
            cda[...] = wda_ref[...].astype(BF16)

        @pl.when(jnp.logical_or(j == 0, eb_ref[j] != eb_ref[prev]))
        def _():
            cgb[...] = wgb_ref[...].astype(BF16)
            cub[...] = wub_ref[...].astype(BF16)
            cdb[...] = wdb_ref[...].astype(BF16)

        valid = lax.broadcasted_iota(I32, (TE, 1), 0) < n
        h = jnp.where(valid, _unpack_bf16_pairs(rows_ref[:, :HALF]), 0.0).astype(BF16)
        gates = jnp.where(valid, pltpu.bitcast(rows_ref[:, HALF:], F32), 0.0)
        y = None
        for cg, cu, cd, col in ((cga, cua, cda, 0), (cgb, cub, cdb, 1)):
            a = jnp.dot(h, cg[...], preferred_element_type=F32)
            u = jnp.dot(h, cu[...], preferred_element_type=F32)
            hid = _silu(a) * u * gates[:, col:col + 1]
            o = jnp.dot(hid.astype(BF16), cd[...], preferred_element_type=F32)
            y = o if y is None else y + o
        y_ref[...] = _pack_bf16_pairs(y)


def _experts(rows_sorted, ea, eb, nrow, layer, w_gate, w_up, w_down):
    def w_spec(shape, which):
        return pl.BlockSpec((None, None) + shape, lambda j, ea, eb, nr: (layer, (ea, eb)[which][j], 0, 0))

    up_shape, down_shape = (D_MODEL, EXPERT_FF), (EXPERT_FF, D_MODEL)
    grid_spec = pltpu.PrefetchScalarGridSpec(
        num_scalar_prefetch=3,
        grid=(N_ETILES,),
        in_specs=[pl.BlockSpec((TE, ROW_WORDS), lambda j, *_: (j, 0)),
                  w_spec(up_shape, 0), w_spec(up_shape, 0), w_spec(down_shape, 0),
                  w_spec(up_shape, 1), w_spec(up_shape, 1), w_spec(down_shape, 1)],
        out_specs=pl.BlockSpec((TE, HALF), lambda j, *_: (j, 0)),
        scratch_shapes=[pltpu.VMEM(up_shape, BF16), pltpu.VMEM(up_shape, BF16), pltpu.VMEM(down_shape, BF16),
                        pltpu.VMEM(up_shape, BF16), pltpu.VMEM(up_shape, BF16), pltpu.VMEM(down_shape, BF16)])
    return pl.pallas_call(
        _experts_kernel,
        grid_spec=grid_spec,
        out_shape=jax.ShapeDtypeStruct((P_MAX, HALF), U32),
        compiler_params=_cparams(("arbitrary",)),
        name="experts_layer%d" % layer,
    )(ea, eb, nrow, rows_sorted, w_gate, w_up, w_down, w_gate, w_up, w_down)


def _moe(rows, meta, counts, layer, w_gate, w_up, w_down):
    pos3, ea, eb, nrow = _dispatch_tables(meta, counts)
    rows_sorted = _sc_permute(rows, pos3, P_MAX, scatter=True)
    y_sorted = _experts(rows_sorted, ea, eb, nrow, layer, w_gate, w_up, w_down)
    return _sc_permute(y_sorted, pos3, NT, scatter=False)


def _final_kernel(x_ref, y_ref, g2_ref, nfin_ref, yp_ref, ys_ref):
    i = pl.program_id(0)
    x = x_ref[...] + g2_ref[...] * _unpack_bf16_pairs(y_ref[...])
    out = x * lax.rsqrt(jnp.mean(x * x, axis=-1, keepdims=True) + EPS) * nfin_ref[...]

    @pl.when(i < N_CTX_TILES)
    def _():
        yp_ref[...] = out

    @pl.when(i >= N_CTX_TILES)
    def _():
        ys_ref[...] = out


def _final(x1, y_tok, mods, norm_final):
    return pl.pallas_call(
        _final_kernel,
        grid=(NT // TM,),
        in_specs=[_row_spec(D_MODEL), _row_spec(HALF), _mod_spec(DEPTH - 1, 5, TM), _const_spec((1, D_MODEL))],
        out_specs=_split_row_specs(D_MODEL),
        out_shape=[jax.ShapeDtypeStruct((N_CTX, D_MODEL), F32), jax.ShapeDtypeStruct((N_LAT, D_MODEL), F32)],
        compiler_params=_cparams(("arbitrary",)),
        name="final_norm",
    )(x1, y_tok, mods, norm_final.reshape(1, D_MODEL))


HALO = 16


def _in1_kernel(x_ref, xa_ref, xb_ref, y_ref, ya_ref, yb_ref, g2_ref, nm_ref, sc_ref, sh_ref, w_ref, cw_ref, cb_ref,
                x2_ref, z_ref, xc_ref, dt_ref, h_ref):
    i = pl.program_id(0)
    g2 = g2_ref[...]
    gain = nm_ref[...] * (1.0 + sc_ref[...])
    shift = sh_ref[...]

    def normed(x_rows, y_rows):
        x = x_rows + g2 * _unpack_bf16_pairs(y_rows)
        return x, (x * lax.rsqrt(jnp.mean(x * x, axis=-1, keepdims=True) + EPS) * gain + shift).astype(BF16)

    def norm_rows(rows):
        x, h = normed(x_ref[rows, :], y_ref[rows, :])
        x2_ref[rows, :] = x
        h_ref[pl.ds(rows.start + HALO, rows.size), :] = h

    _row_slabs(TM, SLAB, norm_rows)
    h_ref[0:HALO, :] = normed(xa_ref[...], ya_ref[...])[1]
    h_ref[HALO + TM:, :] = normed(xb_ref[...], yb_ref[...])[1]

    step = 512
    nt_dims = (((1,), (1,)), ((), ()))

    def proj(h, c0, width):
        return lax.dot_general(h, w_ref[c0:c0 + width, :], nt_dims, preferred_element_type=F32)

    h = h_ref[HALO:HALO + TM, :]
    for c0 in range(0, SSM_INNER, step):
        z_ref[:, c0:c0 + step] = proj(h, c0, step).astype(BF16)
    n_dt = 2 * SSM_HEADS
    dt_ref[:, :n_dt] = proj(h, SSM_INNER + SSM_CONV_CH, n_dt)
    dt_ref[:, n_dt:] = jnp.zeros((TM, LANE - n_dt), F32)

    row = lax.broadcasted_iota(jnp.int32, (TM, LANE), 0)
    is_ctx = i < N_CTX_TILES
    j = (i - N_CTX_TILES) % LAT_TILES_PER_SEQ
    in_seq = row & (SEQ - 1)
    lat_first = (j == 0).astype(I32)
    lat_last = (j == LAT_TILES_PER_SEQ - 1).astype(I32)
    starts = jnp.where(is_ctx, (in_seq == 0).astype(I32), (row == 0).astype(I32) * lat_first) != 0
    ends = jnp.where(is_ctx, (in_seq == SEQ - 1).astype(I32), (row == TM - 1).astype(I32) * lat_last) != 0
    h_ext = h_ref[...]
    n_ext = TM + 2 * HALO
    for c0 in range(0, SSM_CONV_CH, step):
        xe = proj(h_ext, SSM_INNER + c0, step)
        for t in range(step // LANE):
            cols = slice(c0 + t * LANE, c0 + (t + 1) * LANE)
            x = xe[:, t * LANE:(t + 1) * LANE]
            above = jnp.where(starts, 0.0, pltpu.roll(x, 1, 0)[HALO:HALO + TM])
            below = jnp.where(ends, 0.0, pltpu.roll(x, n_ext - 1, 0)[HALO:HALO + TM])
            y = (above * cw_ref[0:1, cols] + x[HALO:HALO + TM] * cw_ref[1:2, cols] + below * cw_ref[2:3, cols]
                 + cb_ref[:, cols])
            xc_ref[:, cols] = _silu(y).astype(BF16)


def _in1(x1, y_tok, mods, norm_mix1, w_in, conv_w, conv_b):
    w = jnp.swapaxes(w_in, 0, 1).astype(BF16)
    per_tile = TM // HALO
    last = NT // HALO - 1

    def above(i):
        return (jnp.maximum(i * per_tile - 1, 0), 0)

    def below(i):
        return (jnp.minimum((i + 1) * per_tile, last), 0)

    return pl.pallas_call(
        _in1_kernel,
        grid=(NT // TM,),
        in_specs=[_row_spec(D_MODEL), pl.BlockSpec((HALO, D_MODEL), above), pl.BlockSpec((HALO, D_MODEL), below),
                  _row_spec(HALF), pl.BlockSpec((HALO, HALF), above), pl.BlockSpec((HALO, HALF), below),
                  _mod_spec(0, 5, TM), _const_spec((1, D_MODEL)),
                  _mod_spec(1, 1, TM), _mod_spec(1, 0, TM), _const_spec((ODD_IN, D_MODEL)),
                  _const_spec((3, SSM_CONV_CH)), _const_spec((1, SSM_CONV_CH))],
        out_specs=[_row_spec(D_MODEL), _row_spec(SSM_INNER), _row_spec(SSM_CONV_CH), _row_spec(LANE)],
        out_shape=[jax.ShapeDtypeStruct((NT, D_MODEL), F32), jax.ShapeDtypeStruct((NT, SSM_INNER), BF16),
                   jax.ShapeDtypeStruct((NT, SSM_CONV_CH), BF16), jax.ShapeDtypeStruct((NT, LANE), F32)],
        scratch_shapes=[pltpu.VMEM((TM + 2 * HALO, D_MODEL), BF16)],
        compiler_params=_cparams(("arbitrary",)),
        name="in_proj_ssm",
    )(x1, x1, x1, y_tok, y_tok, y_tok, mods, norm_mix1.reshape(1, D_MODEL), mods, mods, w, conv_w,
      conv_b.reshape(1, SSM_CONV_CH))


HPG = SSM_HEADS // SSM_GROUPS
GW = HPG * SSM_HEAD_DIM


def _ssd_kernel(*refs, has_init, write_state):
    refs = list(refs)
    xcf_ref, xcb_ref, dtf_ref, dtb_ref, bias_ref, alog_ref = refs[:6]
    refs = refs[6:]
    h0f_ref, h0b_ref = (refs.pop(0), refs.pop(0)) if has_init else (None, None)
    yf_ref, yb_ref = refs.pop(0), refs.pop(0)
    houtf_ref, houtb_ref = (refs.pop(0), refs.pop(0)) if write_state else (None, None)
    hf_ref, hb_ref = refs
    c = pl.program_id(1)

    @pl.when(c == 0)
    def _():
        for h_ref, h0_ref in ((hf_ref, h0f_ref), (hb_ref, h0b_ref)):
            for g in range(SSM_GROUPS):
                if has_init:
                    h_ref[g] = h0_ref[g * GW:(g + 1) * GW, :].T
                else:
                    h_ref[g] = jnp.zeros((SSM_STATE, GW), F32)

    _ssd_chunk(xcf_ref, dtf_ref, bias_ref, alog_ref, yf_ref, hf_ref, False)
    _ssd_chunk(xcb_ref, dtb_ref, bias_ref, alog_ref, yb_ref, hb_ref, True)

    if write_state:
        @pl.when(c == pl.num_programs(1) - 1)
        def _():
            for h_ref, hout_ref in ((hf_ref, houtf_ref), (hb_ref, houtb_ref)):
                for g in range(SSM_GROUPS):
                    hout_ref[g * GW:(g + 1) * GW, :] = h_ref[g].T


def _ssd_chunk(xc_ref, dt_ref, bias_ref, alog_ref, y_ref, h_ref, reverse):
    col0 = SSM_HEADS if reverse else 0
    Q = SSM_CHUNK
    dt = jax.nn.softplus(dt_ref[...] + bias_ref[...])
    a = dt * -jnp.exp(alog_ref[...])
    row = lax.broadcasted_iota(jnp.int32, (Q, Q), 0)
    col = lax.broadcasted_iota(jnp.int32, (Q, Q), 1)
    keep = (col >= row) if reverse else (col <= row)
    tri = keep.astype(F32)
    acs = jnp.dot(tri, a, precision=HIGHEST, preferred_element_type=F32)
    edge = (0 if reverse else Q - 1)
    acs_end = acs[edge:edge + 1, :]
    log2e = math.log2(math.e)
    acs2 = acs * log2e
    src_t = (acs2 - jnp.log2(dt)).T
    out_t = (jnp.exp(acs_end - acs) * dt).T
    chunk_decay = jnp.exp(jnp.broadcast_to(acs_end, (8, LANE)))
    nt_dims = (((1,), (1,)), ((), ()))
    lane = lax.broadcasted_iota(jnp.int32, (Q, LANE), 1)
    low = lane < SSM_HEAD_DIM
    low8 = low[:8]

    def two_heads(v):
        zero = jnp.zeros_like(v)
        return jnp.concatenate([jnp.where(low, v, zero), jnp.where(low, zero, v)], axis=0)

    for g in range(SSM_GROUPS):
        bg = xc_ref[:, SSM_INNER + g * SSM_STATE:SSM_INNER + (g + 1) * SSM_STATE]
        cg = xc_ref[:, SSM_INNER + SSM_GROUPS * SSM_STATE + g * SSM_STATE:
                    SSM_INNER + SSM_GROUPS * SSM_STATE + (g + 1) * SSM_STATE]
        cb = lax.dot_general(cg, bg, nt_dims, preferred_element_type=F32).astype(BF16)
        bg_t = bg.astype(F32).T
        for j in range(HPG // 2):
            pair = slice(g * GW + j * LANE, g * GW + (j + 1) * LANE)
            mats, c_in, b_out, cdec = [], [], [], []
            for hh in (2 * j, 2 * j + 1):
                cidx = col0 + g * HPG + hh
                to_l = jnp.broadcast_to(acs2[:, cidx:cidx + 1], (Q, Q))
                lmat = jnp.where(keep, jnp.exp2(to_l - src_t[cidx:cidx + 1, :]), 0.0)
                mats.append(cb * lmat.astype(BF16))
                c_in.append(cg * jnp.exp2(to_l).astype(BF16))
                b_out.append((bg_t * out_t[cidx:cidx + 1, :]).astype(BF16))
                cdec.append(jnp.broadcast_to(chunk_decay[:, cidx:cidx + 1], (8, LANE)))
            hp = h_ref[g, :, j * LANE:(j + 1) * LANE]
            x2 = two_heads(xc_ref[:, pair])
            rhs = jnp.concatenate([x2, two_heads(hp.astype(BF16))], axis=0)
            y = jnp.dot(jnp.concatenate(mats + c_in, axis=1), rhs, preferred_element_type=F32)
            y_ref[:, pair] = y.astype(BF16)
            st = jnp.dot(jnp.concatenate(b_out, axis=1), x2, preferred_element_type=F32)
            h_ref[g, :, j * LANE:(j + 1) * LANE] = hp * jnp.where(low8, cdec[0], cdec[1])[0:1, :] + st


def _ssd(xc, dt_raw, bias128, alog128, h0_fwd, h0_bwd):
    outs = []
    for (row0, nseq, seqlen, has_init, write_state) in ((0, BATCH, SEQ, False, True),
                                                         (N_CTX, DEC_BATCH, DEC_SEQ, True, False)):
        nc = seqlen // SSM_CHUNK
        base = row0 // SSM_CHUNK

        def fwd_in(s, c, nc=nc, base=base):
            return (base + s * nc + c, 0)

        def bwd_in(s, c, nc=nc, base=base):
            return (base + s * nc + nc - 1 - c, 0)

        def fwd_out(s, c, nc=nc):
            return (s * nc + c, 0)

        def bwd_out(s, c, nc=nc):
            return (s * nc + nc - 1 - c, 0)

        state_spec = pl.BlockSpec((None, SSM_INNER, SSM_STATE), lambda s, c: (s, 0, 0))
        y_shape = jax.ShapeDtypeStruct((nseq * seqlen, SSM_INNER), BF16)
        state_shape = jax.ShapeDtypeStruct((nseq, SSM_INNER, SSM_STATE), F32)
        in_specs = [pl.BlockSpec((SSM_CHUNK, SSM_CONV_CH), fwd_in), pl.BlockSpec((SSM_CHUNK, SSM_CONV_CH), bwd_in),
                    pl.BlockSpec((SSM_CHUNK, LANE), fwd_in), pl.BlockSpec((SSM_CHUNK, LANE), bwd_in),
                    _const_spec((1, LANE)), _const_spec((1, LANE))]
        args = [xc, xc, dt_raw, dt_raw, bias128, alog128]
        if has_init:
            in_specs += [state_spec, state_spec]
            args += [h0_fwd, h0_bwd]
        out_specs = [pl.BlockSpec((SSM_CHUNK, SSM_INNER), fwd_out), pl.BlockSpec((SSM_CHUNK, SSM_INNER), bwd_out)]
        out_shape = [y_shape, y_shape]
        if write_state:
            out_specs += [state_spec, state_spec]
            out_shape += [state_shape, state_shape]
        state_scratch = pltpu.VMEM((SSM_GROUPS, SSM_STATE, GW), F32)
        outs.append(pl.pallas_call(
            functools.partial(_ssd_kernel, has_init=has_init, write_state=write_state),
            grid=(nseq, nc),
            in_specs=in_specs,
            out_specs=out_specs,
            out_shape=out_shape,
            scratch_shapes=[state_scratch, state_scratch],
            compiler_params=_cparams(("arbitrary", "arbitrary")),
            name="ssd_%s" % ("lat" if has_init else "ctx"),
        )(*args))
    ctx, lat = outs
    return (ctx[0], lat[0]), (ctx[1], lat[1]), ctx[2], ctx[3]


def _out1_kernel(yfc_ref, yfl_ref, ybc_ref, ybl_ref, xs_ref, z_ref, dskip_ref, ng_ref, wo_ref, x_ref, g1_ref, nf_ref,
                 sc2_ref, sh2_ref, wr_ref, rb_ref, tri_ref, x1_ref, rows_ref, meta_ref, count_ref,
                 carry_ref, hhi_ref, hlo_ref, acc_ref, y_ref):
    is_lat = pl.program_id(0) >= N_CTX_TILES
    dskip = dskip_ref[...]
    gain = ng_ref[...]

    def gate_norm(rows):
        yf = jnp.where(is_lat, yfl_ref[rows, :], yfc_ref[rows, :])
        yb = jnp.where(is_lat, ybl_ref[rows, :], ybc_ref[rows, :])
        y = (yf + yb).astype(F32) + dskip * xs_ref[rows, :].astype(F32)
        y = y * _silu(z_ref[rows, :].astype(F32))
        y_ref[rows, :] = (y * lax.rsqrt(jnp.mean(y * y, axis=-1, keepdims=True) + EPS) * gain).astype(BF16)

    _row_slabs(TM, SLAB // 2, gate_norm)
    acc_ref[...] = jnp.dot(y_ref[...], wo_ref[...], preferred_element_type=F32)
    _residual_router(lambda rows: x_ref[rows, :], acc_ref, g1_ref, nf_ref, sc2_ref, sh2_ref, wr_ref, rb_ref,
                     tri_ref, x1_ref, rows_ref, meta_ref, count_ref, carry_ref, hhi_ref, hlo_ref)


def _out1(yf, yb, xc, z, d_skip, ssm_norm, w_out, x, mods, norm_ffn1, router):
    wr, rb, tri = router
    return pl.pallas_call(
        _out1_kernel,
        grid=(NT // TM,),
        in_specs=_split_row_specs(SSM_INNER) + _split_row_specs(SSM_INNER)
        + [_row_spec(SSM_INNER), _row_spec(SSM_INNER),
           _const_spec((1, SSM_INNER)), _const_spec((1, SSM_INNER)), _const_spec((SSM_INNER, D_MODEL)),
           _row_spec(D_MODEL)] + _epilogue_specs(1),
        out_specs=_EPILOGUE_OUT_SPECS,
        out_shape=_EPILOGUE_OUT_SHAPE,
        scratch_shapes=_EPILOGUE_SCRATCH + [pltpu.VMEM((TM, SSM_INNER), BF16)],
        compiler_params=_cparams(("arbitrary",)),
        name="out_proj_ssm",
    )(yf[0], yf[1], yb[0], yb[1], xc, z, jnp.repeat(d_skip.astype(F32), SSM_HEAD_DIM).reshape(1, SSM_INNER),
      ssm_norm.reshape(1, SSM_INNER), w_out.astype(BF16), x, mods, norm_ffn1.reshape(1, D_MODEL), mods, mods,
      wr, rb, tri)


def _kv_from_lane_major(t):
    return t.reshape(BATCH, 1, N_KV_HEADS, HEAD_DIM, SEQ).transpose(0, 1, 4, 2, 3)


def kernel(x_prompt, x_sample, cache_k, cache_v, state_fwd, state_bwd, c, c_ctx, w_ada, b_ada, norm_mix, norm_ffn,
           w_in_att, q_gain, k_gain, w_fourier, w_out_att, w_in_ssm, conv_w, conv_b, dt_bias_f, dt_bias_b, a_log_f,
           a_log_b, d_skip, ssm_norm, w_out_ssm, w_router, router_bias, w_gate, w_up, w_down, norm_final):
    xp = x_prompt.reshape(N_CTX, D_MODEL)
    xs = x_sample.reshape(N_LAT, D_MODEL)
    cond8 = jnp.zeros((8, D_MODEL), F32).at[0].set(c_ctx).at[1:1 + DEC_BATCH].set(c)
    mods = _ada_mods(cond8, w_ada, b_ada)
    router = _router_operands(w_router, router_bias)
    cos2, sin2 = _rope_tables()

    q, kp, vp, u_ctx, u_lat, new_k, new_v = _in0(xp, xs, mods, norm_mix[0], w_in_att[0], q_gain[0], k_gain[0],
                                                 cos2, sin2)
    att = _attention(q, kp, vp, cache_k[:, 0], cache_v[:, 0])
    mixed = _fourier(u_ctx, u_lat)
    x1, rows, meta, counts = _out0(att, mixed, xp, xs, mods, w_fourier[0], w_out_att[0], norm_ffn[0], router)
    y_tok = _moe(rows, meta, counts, 0, w_gate, w_up, w_down)

    x2, z, xc, dt_raw = _in1(x1, y_tok, mods, norm_mix[1], w_in_ssm[0], conv_w[0], conv_b[0])
    pad = jnp.zeros((LANE - 2 * SSM_HEADS,), F32)
    bias128 = jnp.concatenate([dt_bias_f[0], dt_bias_b[0], pad]).astype(F32).reshape(1, LANE)
    alog128 = jnp.concatenate([a_log_f[0], a_log_b[0], pad]).astype(F32).reshape(1, LANE)
    yf, yb, sf, sb = _ssd(xc, dt_raw, bias128, alog128, state_fwd.reshape(DEC_BATCH, SSM_INNER, SSM_STATE),
                          state_bwd.reshape(DEC_BATCH, SSM_INNER, SSM_STATE))
    x3, rows, meta, counts = _out1(yf, yb, xc, z, d_skip[0], ssm_norm[0], w_out_ssm[0], x2, mods, norm_ffn[1], router)
    y_tok = _moe(rows, meta, counts, 1, w_gate, w_up, w_down)
    y_prompt, y_sample = _final(x3, y_tok, mods, norm_final)

    state_shape = (BATCH, 1, SSM_HEADS, SSM_HEAD_DIM, SSM_STATE)
    return (y_prompt.reshape(BATCH, SEQ, D_MODEL), y_sample.reshape(DEC_BATCH, DEC_SEQ, D_MODEL),
            _kv_from_lane_major(new_k), _kv_from_lane_major(new_v),
            sf.reshape(state_shape), sb.reshape(state_shape))
```

```python
import functools
import math

import numpy as np
import jax
import jax.numpy as jnp
from jax import lax
from jax.experimental import pallas as pl
from jax.experimental.pallas import tpu as pltpu
from jax.experimental.pallas import tpu_sc as plsc

F32 = jnp.float32
BF16 = jnp.bfloat16
U32 = jnp.uint32
I32 = jnp.int32
HIGHEST = lax.Precision.HIGHEST

D_MODEL = 1024
BATCH = 32
SEQ = 256
DEPTH = 2
DEC_BATCH = 2
DEC_SEQ = 4096
PAST_LEN = 256
GRID_W = 64
EPS = 1e-6
N_HEADS = 8
N_KV_HEADS = 2
HEAD_DIM = 64
ATT_WIDTH = N_HEADS * HEAD_DIM
KV_WIDTH = N_KV_HEADS * HEAD_DIM
ROPE_THETA = 10000.0
N_FGROUPS = 8
FGROUP_DIM = 64
FOURIER_WIDTH = N_FGROUPS * FGROUP_DIM
EVEN_IN = ATT_WIDTH + 2 * KV_WIDTH + FOURIER_WIDTH
SSM_INNER = 2 * D_MODEL
SSM_HEAD_DIM = 64
SSM_HEADS = SSM_INNER // SSM_HEAD_DIM
SSM_GROUPS = 4
SSM_STATE = 128
SSM_CHUNK = 128
SSM_CONV_CH = SSM_INNER + 2 * SSM_GROUPS * SSM_STATE
ODD_IN = SSM_INNER + SSM_CONV_CH + 2 * SSM_HEADS
N_EXPERTS = 16
EXPERTS_PER_GROUP = 4
N_EXPERT_GROUPS = 4
EXPERT_FF = 512

N_CTX = BATCH * SEQ
N_LAT = DEC_BATCH * DEC_SEQ
NT = N_CTX + N_LAT
N_SEG = 1 + DEC_BATCH
LANE = 128
VMEM_LIMIT = 56 * 1024 * 1024

TM = 512
N_CTX_TILES = N_CTX // TM
LAT_TILES_PER_SEQ = DEC_SEQ // TM

PAIR_ORDER = ((0, 1), (0, 2), (0, 3), (1, 3), (1, 2), (3, 2))
N_BUCKETS = N_EXPERT_GROUPS * len(PAIR_ORDER)
BUCKET_A = tuple(g * EXPERTS_PER_GROUP + a for g in range(N_EXPERT_GROUPS) for a, _ in PAIR_ORDER)
BUCKET_B = tuple(g * EXPERTS_PER_GROUP + b for g in range(N_EXPERT_GROUPS) for _, b in PAIR_ORDER)
BUCKET_ROWS = 32
TE = 256
N_ETILES = (NT + N_BUCKETS * (TE - 1) + TE - 1) // TE
P_MAX = N_ETILES * TE
HALF = D_MODEL // 2
ROW_WORDS = HALF + LANE
SC_CORES = 2
SC_SUBCORES = 16
SC_WORKERS = SC_CORES * SC_SUBCORES
SC_CHUNK = 64


def _cparams(sem):
    return pltpu.CompilerParams(dimension_semantics=sem, vmem_limit_bytes=VMEM_LIMIT)


def _seg_of_tile(i, tm):
    nct = N_CTX // tm
    return jnp.where(i < nct, 0, 1 + (i - nct) // (DEC_SEQ // tm))


def _mod_spec(layer, which, tm):
    return pl.BlockSpec((None, None, None, 1, D_MODEL),
                        lambda i, *_: (layer, which, _seg_of_tile(i, tm), 0, 0))


def _row_spec(width, tm=TM):
    return pl.BlockSpec((tm, width), lambda i, *_: (i, 0))


def _const_spec(shape):
    nd = len(shape)
    return pl.BlockSpec(shape, lambda *_: (0,) * nd)


def _silu(x):
    return x * jax.nn.sigmoid(x)


def _ada_kernel(cond_ref, w_ref, b_ref, o_ref):
    s = _silu(cond_ref[...])
    w = w_ref[...]
    s_hi = s.astype(BF16)
    w_hi = w.astype(BF16)
    s_lo = (s - s_hi.astype(F32)).astype(BF16)
    w_lo = (w - w_hi.astype(F32)).astype(BF16)
    o_ref[...] = (jnp.dot(s_hi, w_hi, preferred_element_type=F32) + jnp.dot(s_hi, w_lo, preferred_element_type=F32)
                  + jnp.dot(s_lo, w_hi, preferred_element_type=F32) + b_ref[...])


def _ada_mods(cond8, w_ada, b_ada):
    tn = 1536
    out = pl.pallas_call(
        _ada_kernel,
        grid=(DEPTH, 6 * D_MODEL // tn),
        in_specs=[pl.BlockSpec((8, D_MODEL), lambda l, n: (0, 0)),
                  pl.BlockSpec((None, D_MODEL, tn), lambda l, n: (l, 0, n)),
                  pl.BlockSpec((None, 1, tn), lambda l, n: (l, 0, n))],
        out_specs=pl.BlockSpec((None, 8, tn), lambda l, n: (l, 0, n)),
        out_shape=jax.ShapeDtypeStruct((DEPTH, 8, 6 * D_MODEL), F32),
        compiler_params=_cparams(("arbitrary", "arbitrary")),
        name="ada_mod",
    )(cond8, w_ada, b_ada.reshape(DEPTH, 1, 6 * D_MODEL))
    return out.reshape(DEPTH, 8, 6, D_MODEL)[:, :N_SEG].transpose(0, 2, 1, 3)[:, :, :, None, :]


def _rope_tables():
    t = np.arange(DEC_SEQ)
    row = (t // GRID_W).astype(np.float64)
    col = (t % GRID_W).astype(np.float64)
    axis_dim = HEAD_DIM // 2
    freqs = ROPE_THETA ** (-np.arange(0, axis_dim, 2, dtype=np.float64) / axis_dim)
    ang = np.concatenate([row[:, None] * freqs, col[:, None] * freqs], axis=-1)
    cos = np.repeat(np.cos(ang), 2, axis=1)
    sin = np.repeat(np.sin(ang), 2, axis=1)
    sign = np.where(np.arange(HEAD_DIM) % 2 == 0, -1.0, 1.0)
    cos2 = np.tile(cos, (1, 2)).astype(np.float32)
    sin2 = np.tile(sin * sign, (1, 2)).astype(np.float32)
    return jnp.asarray(cos2), jnp.asarray(sin2)


def _dft_cos_sin(n, scale):
    k = np.arange(n)
    ang = 2.0 * np.pi * ((k[:, None] * k[None, :]) % n) / n
    return np.cos(ang) * scale, np.sin(ang) * scale


def _block_diag(m, reps):
    n = m.shape[0]
    out = np.zeros((n * reps, n * reps), m.dtype)
    for r in range(reps):
        out[r * n:(r + 1) * n, r * n:(r + 1) * n] = m
    return out


def _channel_dft():
    c, s = _dft_cos_sin(FGROUP_DIM, FGROUP_DIM ** -0.5)
    return jnp.asarray(np.concatenate([_block_diag(c, N_FGROUPS), _block_diag(s, N_FGROUPS)], axis=1), BF16)


def _group_ones(width):
    return jnp.asarray(_block_diag(np.ones((HEAD_DIM, HEAD_DIM), np.float32), width // HEAD_DIM), BF16)


def _pad_heads(x):
    lane = lax.broadcasted_iota(jnp.int32, x.shape, 1)
    low = lane < HEAD_DIM
    xr = pltpu.roll(x, HEAD_DIM, 1)
    zero = jnp.zeros_like(x)
    return [jnp.where(low, x, zero), jnp.where(low, zero, xr), jnp.where(low, xr, zero), jnp.where(low, zero, x)]


def _ctx_tile(i, *_):
    return (jnp.minimum(i, N_CTX_TILES - 1), 0)


def _lat_tile(i, *_):
    return (jnp.maximum(i - N_CTX_TILES, 0), 0)


def _split_row_specs(width):
    return [pl.BlockSpec((TM, width), _ctx_tile), pl.BlockSpec((TM, width), _lat_tile)]


def _in0_kernel(xp_ref, xs_ref, nm_ref, sc_ref, sh_ref, w_ref, qg_ref, kg_ref, ones_ref, cos_ref, sin_ref, dft_ref,
                q_ref, kp_ref, vp_ref, uc_ref, ul_ref, nk_ref, nv_ref, h_ref):
    i = pl.program_id(0)
    is_lat = i >= N_CTX_TILES
    gain = nm_ref[...] * (1.0 + sc_ref[...])
    shift = sh_ref[...]

    def norm_rows(rows):
        x = jnp.where(is_lat, xs_ref[rows, :], xp_ref[rows, :])
        h_ref[rows, :] = (x * lax.rsqrt(jnp.mean(x * x, axis=-1, keepdims=True) + EPS) * gain + shift).astype(BF16)

    _row_slabs(TM, SLAB, norm_rows)
    p = jnp.dot(h_ref[...], w_ref[...], preferred_element_type=F32)
    q = p[:, :ATT_WIDTH]
    k = p[:, ATT_WIDTH:ATT_WIDTH + KV_WIDTH]
    v = p[:, ATT_WIDTH + KV_WIDTH:ATT_WIDTH + 2 * KV_WIDTH]
    f = p[:, ATT_WIDTH + 2 * KV_WIDTH:]
    ones = ones_ref[...]
    qss = jnp.dot((q * q).astype(BF16), ones, preferred_element_type=F32)
    kss = jnp.dot((k * k).astype(BF16), ones[:KV_WIDTH, :KV_WIDTH], preferred_element_type=F32)
    qn = q * lax.rsqrt(qss * (1.0 / HEAD_DIM) + EPS) * qg_ref[...]
    kn = k * lax.rsqrt(kss * (1.0 / HEAD_DIM) + EPS) * kg_ref[...]

    cos = jnp.where(is_lat, cos_ref[...], 1.0)
    sin = jnp.where(is_lat, sin_ref[...], 0.0)
    lane = lax.broadcasted_iota(jnp.int32, (TM, LANE), 1)
    even = (lane % 2) == 0

    def rope(xc):
        swapped = jnp.where(even, pltpu.roll(xc, LANE - 1, 1), pltpu.roll(xc, 1, 1))
        return xc * cos + swapped * sin

    scale = HEAD_DIM ** -0.5 * math.log2(math.e)
    for j in range(ATT_WIDTH // LANE):
        q_ref[:, j * LANE:(j + 1) * LANE] = (rope(qn[:, j * LANE:(j + 1) * LANE]) * scale).astype(BF16)
    for j, c in enumerate(_pad_heads(rope(kn))):
        kp_ref[:, j * LANE:(j + 1) * LANE] = c.astype(BF16)
    for j, c in enumerate(_pad_heads(v)):
        vp_ref[:, j * LANE:(j + 1) * LANE] = c.astype(BF16)
    u = jnp.dot(f.astype(BF16), dft_ref[...], preferred_element_type=F32).astype(BF16)

    @pl.when(is_lat)
    def _():
        ul_ref[...] = u

    @pl.when(jnp.logical_not(is_lat))
    def _():
        uc_ref[...] = u
        for r in range(TM // SEQ):
            nk_ref[r] = kn[r * SEQ:(r + 1) * SEQ, :].T
            nv_ref[r] = v[r * SEQ:(r + 1) * SEQ, :].T


def _in0(xp, xs, mods, norm_mix0, w_in, q_gain, k_gain, cos2, sin2):
    def table_idx(i):
        return (jnp.where(i < N_CTX_TILES, 0, (i - N_CTX_TILES) % LAT_TILES_PER_SEQ), 0)

    def ctx_idx(i):
        return (jnp.minimum(i, N_CTX_TILES - 1), 0, 0)

    seqs = TM // SEQ
    outs = pl.pallas_call(
        _in0_kernel,
        grid=(NT // TM,),
        in_specs=_split_row_specs(D_MODEL) + [_const_spec((1, D_MODEL)), _mod_spec(0, 1, TM), _mod_spec(0, 0, TM),
                  _const_spec((D_MODEL, EVEN_IN)), _const_spec((1, ATT_WIDTH)), _const_spec((1, KV_WIDTH)),
                  _const_spec((ATT_WIDTH, ATT_WIDTH)),
                  pl.BlockSpec((TM, LANE), table_idx), pl.BlockSpec((TM, LANE), table_idx),
                  _const_spec((FOURIER_WIDTH, 2 * FOURIER_WIDTH))],
        out_specs=[_row_spec(ATT_WIDTH), _row_spec(4 * LANE), _row_spec(4 * LANE)]
        + _split_row_specs(2 * FOURIER_WIDTH)
        + [pl.BlockSpec((seqs, KV_WIDTH, SEQ), ctx_idx), pl.BlockSpec((seqs, KV_WIDTH, SEQ), ctx_idx)],
        out_shape=[jax.ShapeDtypeStruct((NT, ATT_WIDTH), BF16), jax.ShapeDtypeStruct((NT, 4 * LANE), BF16),
                   jax.ShapeDtypeStruct((NT, 4 * LANE), BF16),
                   jax.ShapeDtypeStruct((N_CTX, 2 * FOURIER_WIDTH), BF16),
                   jax.ShapeDtypeStruct((N_LAT, 2 * FOURIER_WIDTH), BF16),
                   jax.ShapeDtypeStruct((BATCH, KV_WIDTH, SEQ), F32), jax.ShapeDtypeStruct((BATCH, KV_WIDTH, SEQ), F32)],
        scratch_shapes=[pltpu.VMEM((TM, D_MODEL), BF16)],
        compiler_params=_cparams(("arbitrary",)),
        name="in_proj_att",
    )(xp, xs, norm_mix0.reshape(1, D_MODEL), mods, mods, w_in.astype(BF16),
      jnp.tile(q_gain, N_HEADS).reshape(1, ATT_WIDTH), jnp.tile(k_gain, N_KV_HEADS).reshape(1, KV_WIDTH),
      _group_ones(ATT_WIDTH), cos2, sin2, _channel_dft())
    return outs


def _att_kernel(*refs, has_cache):
    if has_cache:
        q_ref, kp_ref, vp_ref, ck_ref, cv_ref, o_ref = refs
        ckp = [c.astype(BF16) for c in _pad_heads(ck_ref[...])]
        cvp = [c.astype(BF16) for c in _pad_heads(cv_ref[...])]
    else:
        q_ref, kp_ref, vp_ref, o_ref = refs
    nt_dims = (((1,), (1,)), ((), ()))
    for j in range(ATT_WIDTH // LANE):
        qj = q_ref[:, j * LANE:(j + 1) * LANE]
        g = j // 2
        acc = None
        for half in range(2):
            c = 2 * g + half
            kk = kp_ref[:, c * LANE:(c + 1) * LANE]
            vv = vp_ref[:, c * LANE:(c + 1) * LANE]
            s = lax.dot_general(qj, kk, nt_dims, preferred_element_type=F32)
            m = jnp.max(s, axis=-1, keepdims=True)
            if has_cache:
                sc = lax.dot_general(qj, ckp[c], nt_dims, preferred_element_type=F32)
                m = jnp.maximum(m, jnp.max(sc, axis=-1, keepdims=True))
            p = jnp.exp2(s - m)
            d = jnp.sum(p, axis=-1, keepdims=True)
            o = jnp.dot(p.astype(BF16), vv, preferred_element_type=F32)
            if has_cache:
                pc = jnp.exp2(sc - m)
                d = d + jnp.sum(pc, axis=-1, keepdims=True)
                o = o + jnp.dot(pc.astype(BF16), cvp[c], preferred_element_type=F32)
            o = o * (1.0 / d)
            acc = o if acc is None else acc + o
        o_ref[:, j * LANE:(j + 1) * LANE] = acc.astype(BF16)


def _attention(q, kp, vp, cache_k, cache_v):
    att_ctx = pl.pallas_call(
        functools.partial(_att_kernel, has_cache=False),
        grid=(BATCH,),
        in_specs=[pl.BlockSpec((SEQ, ATT_WIDTH), lambda b: (b, 0)),
                  pl.BlockSpec((SEQ, 4 * LANE), lambda b: (b, 0)),
                  pl.BlockSpec((SEQ, 4 * LANE), lambda b: (b, 0))],
        out_specs=pl.BlockSpec((SEQ, ATT_WIDTH), lambda b: (b, 0)),
        out_shape=jax.ShapeDtypeStruct((N_CTX, ATT_WIDTH), BF16),
        compiler_params=_cparams(("arbitrary",)),
        name="attention_ctx",
    )(q, kp, vp)
    tq = 256
    off = N_CTX // DEC_SEQ
    att_lat = pl.pallas_call(
        functools.partial(_att_kernel, has_cache=True),
        grid=(DEC_BATCH, DEC_SEQ // tq),
        in_specs=[pl.BlockSpec((tq, ATT_WIDTH), lambda b, i: (N_CTX // tq + b * (DEC_SEQ // tq) + i, 0)),
                  pl.BlockSpec((DEC_SEQ, 4 * LANE), lambda b, i: (off + b, 0)),
                  pl.BlockSpec((DEC_SEQ, 4 * LANE), lambda b, i: (off + b, 0)),
                  pl.BlockSpec((None, PAST_LEN, KV_WIDTH), lambda b, i: (b, 0, 0)),
                  pl.BlockSpec((None, PAST_LEN, KV_WIDTH), lambda b, i: (b, 0, 0))],
        out_specs=pl.BlockSpec((tq, ATT_WIDTH), lambda b, i: (b * (DEC_SEQ // tq) + i, 0)),
        out_shape=jax.ShapeDtypeStruct((N_LAT, ATT_WIDTH), BF16),
        compiler_params=_cparams(("arbitrary", "arbitrary")),
        name="attention_lat",
    )(q, kp, vp, cache_k.reshape(DEC_BATCH, PAST_LEN, KV_WIDTH), cache_v.reshape(DEC_BATCH, PAST_LEN, KV_WIDTH))
    return att_ctx, att_lat


def _four_ctx_kernel(u_ref, c_ref, s_ref, o_ref):
    uc = u_ref[:, :FOURIER_WIDTH]
    us = u_ref[:, FOURIER_WIDTH:]
    o_ref[...] = (jnp.dot(c_ref[...], uc, preferred_element_type=F32)
                  - jnp.dot(s_ref[...], us, preferred_element_type=F32))


FCH = 8


def _four_lat_a_kernel(u_ref, w1_ref, w2_ref, tc_ref, ts_ref, o_ref):
    w1 = w1_ref[...]
    w2 = w2_ref[...]
    for j in range(FCH):
        uc = u_ref[:, j * 2 * FOURIER_WIDTH:j * 2 * FOURIER_WIDTH + FOURIER_WIDTH]
        us = u_ref[:, j * 2 * FOURIER_WIDTH + FOURIER_WIDTH:(j + 1) * 2 * FOURIER_WIDTH]
        z = jnp.dot(w1, uc, preferred_element_type=F32) + jnp.dot(w2, us, preferred_element_type=F32)
        zr = z[:GRID_W]
        zi = z[GRID_W:]
        tc = jnp.concatenate([tc_ref[j]] * (FOURIER_WIDTH // LANE), axis=1)
        ts = jnp.concatenate([ts_ref[j]] * (FOURIER_WIDTH // LANE), axis=1)
        o_ref[j, :, :FOURIER_WIDTH] = (zr * tc - zi * ts).astype(BF16)
        o_ref[j, :, FOURIER_WIDTH:] = (zr * ts + zi * tc).astype(BF16)


def _four_lat_b_kernel(b_ref, c_ref, s_ref, o_ref):
    c = c_ref[...]
    s = s_ref[...]
    for j in range(FCH):
        br = b_ref[:, j * 2 * FOURIER_WIDTH:j * 2 * FOURIER_WIDTH + FOURIER_WIDTH]
        bi = b_ref[:, j * 2 * FOURIER_WIDTH + FOURIER_WIDTH:(j + 1) * 2 * FOURIER_WIDTH]
        o_ref[:, j, :] = (jnp.dot(c, br, preferred_element_type=F32) - jnp.dot(s, bi, preferred_element_type=F32))


def _fourier(u_ctx, u_lat):
    c256, s256 = _dft_cos_sin(SEQ, SEQ ** -0.5)
    mixed_ctx = pl.pallas_call(
        _four_ctx_kernel,
        grid=(BATCH,),
        in_specs=[pl.BlockSpec((SEQ, 2 * FOURIER_WIDTH), lambda b: (b, 0)),
                  _const_spec((SEQ, SEQ)), _const_spec((SEQ, SEQ))],
        out_specs=pl.BlockSpec((SEQ, FOURIER_WIDTH), lambda b: (b, 0)),
        out_shape=jax.ShapeDtypeStruct((N_CTX, FOURIER_WIDTH), F32),
        compiler_params=_cparams(("arbitrary",)),
        name="fourier_ctx",
    )(u_ctx, jnp.asarray(c256, BF16), jnp.asarray(s256, BF16))

    g = GRID_W
    c64, s64 = _dft_cos_sin(g, g ** -0.5)
    w1 = jnp.asarray(np.concatenate([c64, s64], axis=0), BF16)
    w2 = jnp.asarray(np.concatenate([-s64, c64], axis=0), BF16)
    t2 = np.arange(g)[:, None]
    k1 = np.arange(g)[None, :]
    ang = 2.0 * np.pi * (t2 * k1) / (g * g)
    tw_c = jnp.asarray(np.broadcast_to(np.cos(ang)[:, :, None], (g, g, LANE)), F32)
    tw_s = jnp.asarray(np.broadcast_to(np.sin(ang)[:, :, None], (g, g, LANE)), F32)
    width = 2 * FOURIER_WIDTH
    u_lat = u_lat.reshape(DEC_BATCH, g, g * width)
    stage1 = pl.pallas_call(
        _four_lat_a_kernel,
        grid=(DEC_BATCH, g // FCH),
        in_specs=[pl.BlockSpec((None, g, FCH * width), lambda b, i: (b, 0, i)),
                  _const_spec((2 * g, g)), _const_spec((2 * g, g)),
                  pl.BlockSpec((FCH, g, LANE), lambda b, i: (i, 0, 0)),
                  pl.BlockSpec((FCH, g, LANE), lambda b, i: (i, 0, 0))],
        out_specs=pl.BlockSpec((None, FCH, g, width), lambda b, i: (b, i, 0, 0)),
        out_shape=jax.ShapeDtypeStruct((DEC_BATCH, g, g, width), BF16),
        compiler_params=_cparams(("arbitrary", "arbitrary")),
        name="fourier_lat_rows",
    )(u_lat, w1, w2, tw_c, tw_s)
    stage1 = stage1.reshape(DEC_BATCH, g, g * width)
    mixed_lat = pl.pallas_call(
        _four_lat_b_kernel,
        grid=(DEC_BATCH, g // FCH),
        in_specs=[pl.BlockSpec((None, g, FCH * width), lambda b, i: (b, 0, i)),
                  _const_spec((g, g)), _const_spec((g, g))],
        out_specs=pl.BlockSpec((None, g, FCH, FOURIER_WIDTH), lambda b, i: (b, 0, i, 0)),
        out_shape=jax.ShapeDtypeStruct((DEC_BATCH, g, g, FOURIER_WIDTH), F32),
        compiler_params=_cparams(("arbitrary", "arbitrary")),
        name="fourier_lat_cols",
    )(stage1, jnp.asarray(c64, BF16), jnp.asarray(s64, BF16))
    return mixed_ctx, mixed_lat.reshape(N_LAT, FOURIER_WIDTH)


def _pack_bf16_pairs(x):
    n = x.shape[1] // 2
    lo = pltpu.bitcast(x[:, :n].astype(BF16).astype(F32), U32)
    hi = pltpu.bitcast(x[:, n:].astype(BF16).astype(F32), U32)
    return (hi & jnp.uint32(0xFFFF0000)) | (lo >> 16)


def _unpack_bf16_pairs(w):
    lo = pltpu.bitcast(w << 16, F32)
    hi = pltpu.bitcast(w & jnp.uint32(0xFFFF0000), F32)
    return jnp.concatenate([lo, hi], axis=1)


def _route(logits, rb_ref, tri_ref, carry_ref):
    lt = logits.T
    score = [jax.nn.sigmoid(lt[e:e + 1, :]) for e in range(N_EXPERTS)]
    choice = [score[e] + rb_ref[e:e + 1, :] for e in range(N_EXPERTS)]
    best = jnp.zeros_like(score[0], dtype=jnp.int32)
    best_v = None
    for gi in range(N_EXPERT_GROUPS):
        c = choice[gi * EXPERTS_PER_GROUP:(gi + 1) * EXPERTS_PER_GROUP]
        top2 = None
        for a in range(EXPERTS_PER_GROUP):
            for b in range(a + 1, EXPERTS_PER_GROUP):
                pair = c[a] + c[b]
                top2 = pair if top2 is None else jnp.maximum(top2, pair)
        if best_v is None:
            best_v = top2
        else:
            better = top2 > best_v
            best = jnp.where(better, gi, best)
            best_v = jnp.where(better, top2, best_v)
    sel = []
    picked = []
    for e in range(N_EXPERTS):
        gi = e // EXPERTS_PER_GROUP
        rank = jnp.zeros_like(best)
        for o in range(gi * EXPERTS_PER_GROUP, (gi + 1) * EXPERTS_PER_GROUP):
            if o == e:
                continue
            ahead = (choice[o] > choice[e]) | ((choice[o] == choice[e]) & (o < e))
            rank = rank + ahead.astype(jnp.int32)
        chosen = (best == gi) & (rank < 2)
        sel.append(jnp.where(chosen, 1.0, 0.0))
        picked.append(jnp.where(chosen, score[e], 0.0))
    total = picked[0]
    for e in range(1, N_EXPERTS):
        total = total + picked[e]
    inv = 1.0 / total
    gate = [w * inv for w in picked]

    member = [sel[BUCKET_A[k]] * sel[BUCKET_B[k]] for k in range(N_BUCKETS)]
    bucket = member[1]
    wa = member[0] * gate[BUCKET_A[0]]
    wb = member[0] * gate[BUCKET_B[0]]
    for k in range(1, N_BUCKETS):
        if k > 1:
            bucket = bucket + float(k) * member[k]
        wa = wa + member[k] * gate[BUCKET_A[k]]
        wb = wb + member[k] * gate[BUCKET_B[k]]
    tm = bucket.shape[1]
    onehot = jnp.concatenate(member + [jnp.zeros((BUCKET_ROWS - N_BUCKETS, tm), F32)], axis=0)
    earlier = jnp.dot(onehot.astype(BF16), tri_ref[...], preferred_element_type=F32)
    carry = carry_ref[...]
    rank = jnp.sum(onehot * (earlier + carry[:, 0:1]), axis=0, keepdims=True)
    carry_ref[...] = carry + jnp.sum(onehot, axis=1, keepdims=True)
    return bucket.astype(I32), rank.astype(I32), wa, wb


SLAB = 32


def _row_slabs(n_rows, slab, body):
    for i in range(n_rows // slab):
        body(pl.ds(i * slab, slab))


def _residual_router(get_x, acc_ref, g1_ref, nf_ref, sc2_ref, sh2_ref, wr_ref, rb_ref, tri_ref,
                     x1_ref, rows_ref, meta_ref, count_ref, carry_ref, hhi_ref, hlo_ref):
    @pl.when(pl.program_id(0) == 0)
    def _():
        carry_ref[...] = jnp.zeros_like(carry_ref)

    g1 = g1_ref[...]
    gain = nf_ref[...] * (1.0 + sc2_ref[...])
    shift = sh2_ref[...]

    def slab(rows):
        x1 = get_x(rows) + g1 * acc_ref[rows, :]
        x1_ref[rows, :] = x1
        h2 = x1 * lax.rsqrt(jnp.mean(x1 * x1, axis=-1, keepdims=True) + EPS) * gain + shift
        hi = h2.astype(BF16)
        hi32 = hi.astype(F32)
        hhi_ref[rows, :] = hi
        hlo_ref[rows, :] = (h2 - hi32).astype(BF16)
        bits = pltpu.bitcast(hi32, U32)
        rows_ref[rows, :HALF] = (bits[:, HALF:] & jnp.uint32(0xFFFF0000)) | (bits[:, :HALF] >> 16)

    _row_slabs(acc_ref.shape[0], SLAB, slab)
    both = jnp.dot(hhi_ref[...], wr_ref[...], preferred_element_type=F32)
    logits = (both[:, :LANE] + both[:, LANE:]
              + jnp.dot(hlo_ref[...], wr_ref[:, :LANE], preferred_element_type=F32))
    bucket, rank, wa, wb = _route(logits, rb_ref, tri_ref, carry_ref)
    tm = logits.shape[0]
    gates_t = jnp.concatenate([wa, wb, jnp.zeros((LANE - 2, tm), F32)], axis=0)
    rows_ref[:, HALF:] = pltpu.bitcast(gates_t.T, U32)
    meta_ref[...] = jnp.concatenate([bucket, rank, jnp.zeros((6, tm), I32)], axis=0)
    count_ref[...] = carry_ref[...]


def _out0_kernel(attc_ref, attl_ref, mixc_ref, mixl_ref, wf_ref, woa_ref, wof_ref, xp_ref, xs_ref, g1_ref, nf_ref,
                 sc2_ref, sh2_ref, wr_ref, rb_ref, tri_ref, x1_ref, rows_ref, meta_ref, count_ref,
                 carry_ref, hhi_ref, hlo_ref, acc_ref):
    is_lat = pl.program_id(0) >= N_CTX_TILES
    att = jnp.where(is_lat, attl_ref[...], attc_ref[...])
    mix = jnp.where(is_lat, mixl_ref[...], mixc_ref[...])
    four = jnp.dot(mix.astype(BF16), wf_ref[...], preferred_element_type=F32)
    acc_ref[...] = (jnp.dot(att, woa_ref[...], preferred_element_type=F32)
                    + jnp.dot(four.astype(BF16), wof_ref[...], preferred_element_type=F32))

    def get_x(rows):
        return jnp.where(is_lat, xs_ref[rows, :], xp_ref[rows, :])

    _residual_router(get_x, acc_ref, g1_ref, nf_ref, sc2_ref, sh2_ref, wr_ref, rb_ref, tri_ref,
                     x1_ref, rows_ref, meta_ref, count_ref, carry_ref, hhi_ref, hlo_ref)


def _router_operands(w_router, router_bias):
    wr = jnp.zeros((D_MODEL, LANE), F32).at[:, :N_EXPERTS].set(w_router)
    wr_hi = wr.astype(BF16)
    wr_lo = (wr - wr_hi.astype(F32)).astype(BF16)
    rb = jnp.broadcast_to(router_bias.astype(F32)[:, None], (N_EXPERTS, TM))
    tri = jnp.asarray(np.triu(np.ones((TM, TM), np.float32), 1), BF16)
    return jnp.concatenate([wr_hi, wr_lo], axis=1), rb, tri


_EPILOGUE_OUT_SPECS = [_row_spec(D_MODEL), _row_spec(ROW_WORDS), pl.BlockSpec((8, TM), lambda i: (0, i)),
                       _const_spec((BUCKET_ROWS, LANE))]
_EPILOGUE_OUT_SHAPE = [jax.ShapeDtypeStruct((NT, D_MODEL), F32), jax.ShapeDtypeStruct((NT, ROW_WORDS), U32),
                       jax.ShapeDtypeStruct((8, NT), I32), jax.ShapeDtypeStruct((BUCKET_ROWS, LANE), F32)]
_EPILOGUE_SCRATCH = [pltpu.VMEM((BUCKET_ROWS, LANE), F32), pltpu.VMEM((TM, D_MODEL), BF16),
                     pltpu.VMEM((TM, D_MODEL), BF16), pltpu.VMEM((TM, D_MODEL), F32)]


def _epilogue_specs(layer):
    return [_mod_spec(layer, 2, TM), _const_spec((1, D_MODEL)), _mod_spec(layer, 4, TM),
            _mod_spec(layer, 3, TM), _const_spec((D_MODEL, 2 * LANE)), _const_spec((N_EXPERTS, TM)),
            _const_spec((TM, TM))]


def _out0(att, mixed, xp, xs, mods, w_fourier, w_out, norm_ffn0, router):
    wf = jnp.zeros((FOURIER_WIDTH, FOURIER_WIDTH), F32)
    for gi in range(N_FGROUPS):
        sl = slice(gi * FGROUP_DIM, (gi + 1) * FGROUP_DIM)
        wf = wf.at[sl, sl].set(w_fourier[gi])
    w_out = w_out.astype(BF16)
    wr, rb, tri = router
    return pl.pallas_call(
        _out0_kernel,
        grid=(NT // TM,),
        in_specs=_split_row_specs(ATT_WIDTH) + _split_row_specs(FOURIER_WIDTH)
        + [_const_spec((FOURIER_WIDTH, FOURIER_WIDTH)), _const_spec((ATT_WIDTH, D_MODEL)),
           _const_spec((FOURIER_WIDTH, D_MODEL))]
        + _split_row_specs(D_MODEL) + _epilogue_specs(0),
        out_specs=_EPILOGUE_OUT_SPECS,
        out_shape=_EPILOGUE_OUT_SHAPE,
        scratch_shapes=_EPILOGUE_SCRATCH,
        compiler_params=_cparams(("arbitrary",)),
        name="out_proj_att",
    )(att[0], att[1], mixed[0], mixed[1], wf.astype(BF16), w_out[:ATT_WIDTH], w_out[ATT_WIDTH:], xp, xs, mods,
      norm_ffn0.reshape(1, D_MODEL), mods, mods, wr, rb, tri)


def _dispatch_tables(meta, counts):
    bucket, rank = meta[0], meta[1]
    cnt = counts[:N_BUCKETS, 0].astype(I32)
    padded = (cnt + (TE - 1)) // TE * TE
    ends = jnp.cumsum(padded)
    starts = ends - padded
    kk = jnp.arange(N_BUCKETS, dtype=I32)
    pos = rank + jnp.sum(jnp.where(bucket[None, :] == kk[:, None], starts[:, None], 0), axis=0)
    tile0 = jnp.arange(N_ETILES, dtype=I32) * TE
    used = tile0 < ends[-1]
    tb = jnp.sum((tile0[:, None] >= ends[None, :]).astype(I32), axis=1)
    tb = jnp.where(used, tb, jnp.max(jnp.where(used, tb, 0)))
    pick = tb[:, None] == kk[None, :]

    def per_tile(table):
        return jnp.sum(jnp.where(pick, table[None, :], 0), axis=1)

    nrow = jnp.where(used, jnp.clip(per_tile(starts + cnt) - tile0, 0, TE), 0)
    pos3 = pos.reshape(SC_WORKERS, NT // (SC_WORKERS * SC_CHUNK), SC_CHUNK)
    return pos3, per_tile(jnp.asarray(BUCKET_A, I32)), per_tile(jnp.asarray(BUCKET_B, I32)), nrow


def _sc_permute(src, pos3, n_out, scatter):
    width = src.shape[1]
    _, n_chunks, chunk = pos3.shape
    rows_per_worker = n_chunks * chunk
    mesh = plsc.VectorSubcoreMesh(core_axis_name="c", subcore_axis_name="s")

    @functools.partial(pl.kernel, out_type=jax.ShapeDtypeStruct((n_out, width), src.dtype), mesh=mesh,
                       scratch_types=[pltpu.VMEM((n_chunks, chunk), I32), pltpu.VMEM((2, chunk, width), src.dtype),
                                      pltpu.SemaphoreType.DMA((2,))])
    def permute(src_hbm, pos_hbm, out_hbm, pos_v, buf, sem):
        worker = lax.axis_index("s") * SC_CORES + lax.axis_index("c")
        base = worker * rows_per_worker
        pltpu.sync_copy(pos_hbm.at[worker], pos_v)

        def own(j):
            return pl.ds(base + j * chunk, chunk)

        def load(j):
            rows = src_hbm.at[own(j)] if scatter else src_hbm.at[pos_v.at[j]]
            return pltpu.async_copy(rows, buf.at[j % 2], sem.at[j % 2])

        pending = load(0)
        for j in range(n_chunks):
            following = load(j + 1) if j + 1 < n_chunks else None
            pending.wait()
            pltpu.sync_copy(buf.at[j % 2], out_hbm.at[pos_v.at[j]] if scatter else out_hbm.at[own(j)])
            pending = following

    return permute(src, pos3)


def _experts_kernel(ea_ref, eb_ref, nrow_ref, rows_ref, wga_ref, wua_ref, wda_ref, wgb_ref, wub_ref, wdb_ref, y_ref,
                    cga, cua, cda, cgb, cub, cdb):
    j = pl.program_id(0)
    n = nrow_ref[j]

    @pl.when(n == 0)
    def _():
        y_ref[...] = jnp.zeros_like(y_ref)

    @pl.when(n > 0)
    def _():
        prev = jnp.maximum(j - 1, 0)

        @pl.when(jnp.logical_or(j == 0, ea_ref[j] != ea_ref[prev]))
        def _():
            cga[...] = wga_ref[...].astype(BF16)
            cua[...] = wua_ref[...].astype(BF16)
            cda[...] = wda_ref[...].astype(BF16)

        @pl.when(jnp.logical_or(j == 0, eb_ref[j] != eb_ref[prev]))
        def _():
            cgb[...] = wgb_ref[...].astype(BF16)
            cub[...] = wub_ref[...].astype(BF16)
            cdb[...] = wdb_ref[...].astype(BF16)

        valid = lax.broadcasted_iota(I32, (TE, 1), 0) < n
        h = jnp.where(valid, _unpack_bf16_pairs(rows_ref[:, :HALF]), 0.0).astype(BF16)
        gates = jnp.where(valid, pltpu.bitcast(rows_ref[:, HALF:], F32), 0.0)
        y = None
        for cg, cu, cd, col in ((cga, cua, cda, 0), (cgb, cub, cdb, 1)):
            a = jnp.dot(h, cg[...], preferred_element_type=F32)
            u = jnp.dot(h, cu[...], preferred_element_type=F32)
            hid = _silu(a) * u * gates[:, col:col + 1]
            o = jnp.dot(hid.astype(BF16), cd[...], preferred_element_type=F32)
            y = o if y is None else y + o
        y_ref[...] = _pack_bf16_pairs(y)


def _experts(rows_sorted, ea, eb, nrow, layer, w_gate, w_up, w_down):
    def w_spec(shape, which):
        return pl.BlockSpec((None, None) + shape, lambda j, ea, eb, nr: (layer, (ea, eb)[which][j], 0, 0))

    up_shape, down_shape = (D_MODEL, EXPERT_FF), (EXPERT_FF, D_MODEL)
    grid_spec = pltpu.PrefetchScalarGridSpec(
        num_scalar_prefetch=3,
        grid=(N_ETILES,),
        in_specs=[pl.BlockSpec((TE, ROW_WORDS), lambda j, *_: (j, 0)),
                  w_spec(up_shape, 0), w_spec(up_shape, 0), w_spec(down_shape, 0),
                  w_spec(up_shape, 1), w_spec(up_shape, 1), w_spec(down_shape, 1)],
        out_specs=pl.BlockSpec((TE, HALF), lambda j, *_: (j, 0)),
        scratch_shapes=[pltpu.VMEM(up_shape, BF16), pltpu.VMEM(up_shape, BF16), pltpu.VMEM(down_shape, BF16),
                        pltpu.VMEM(up_shape, BF16), pltpu.VMEM(up_shape, BF16), pltpu.VMEM(down_shape, BF16)])
    return pl.pallas_call(
        _experts_kernel,
        grid_spec=grid_spec,
        out_shape=jax.ShapeDtypeStruct((P_MAX, HALF), U32),
        compiler_params=_cparams(("arbitrary",)),
        name="experts_layer%d" % layer,
    )(ea, eb, nrow, rows_sorted, w_gate, w_up, w_down, w_gate, w_up, w_down)


def _moe(rows, meta, counts, layer, w_gate, w_up, w_down):
    pos3, ea, eb, nrow = _dispatch_tables(meta, counts)
    rows_sorted = _sc_permute(rows, pos3, P_MAX, scatter=True)
    y_sorted = _experts(rows_sorted, ea, eb, nrow, layer, w_gate, w_up, w_down)
    return _sc_permute(y_sorted, pos3, NT, scatter=False)


def _final_kernel(x_ref, y_ref, g2_ref, nfin_ref, yp_ref, ys_ref):
    i = pl.program_id(0)
    x = x_ref[...] + g2_ref[...] * _unpack_bf16_pairs(y_ref[...])
    out = x * lax.rsqrt(jnp.mean(x * x, axis=-1, keepdims=True) + EPS) * nfin_ref[...]

    @pl.when(i < N_CTX_TILES)
    def _():
        yp_ref[...] = out

    @pl.when(i >= N_CTX_TILES)
    def _():
        ys_ref[...] = out


def _final(x1, y_tok, mods, norm_final):
    return pl.pallas_call(
        _final_kernel,
        grid=(NT // TM,),
        in_specs=[_row_spec(D_MODEL), _row_spec(HALF), _mod_spec(DEPTH - 1, 5, TM), _const_spec((1, D_MODEL))],
        out_specs=_split_row_specs(D_MODEL),
        out_shape=[jax.ShapeDtypeStruct((N_CTX, D_MODEL), F32), jax.ShapeDtypeStruct((N_LAT, D_MODEL), F32)],
        compiler_params=_cparams(("arbitrary",)),
        name="final_norm",
    )(x1, y_tok, mods, norm_final.reshape(1, D_MODEL))


HALO = 16


def _in1_kernel(x_ref, xa_ref, xb_ref, y_ref, ya_ref, yb_ref, g2_ref, nm_ref, sc_ref, sh_ref, w_ref, cw_ref, cb_ref,
                x2_ref, z_ref, xc_ref, dt_ref, h_ref):
    i = pl.program_id(0)
    g2 = g2_ref[...]
    gain = nm_ref[...] * (1.0 + sc_ref[...])
    shift = sh_ref[...]

    def normed(x_rows, y_rows):
        x = x_rows + g2 * _unpack_bf16_pairs(y_rows)
        return x, (x * lax.rsqrt(jnp.mean(x * x, axis=-1, keepdims=True) + EPS) * gain + shift).astype(BF16)

    def norm_rows(rows):
        x, h = normed(x_ref[rows, :], y_ref[rows, :])
        x2_ref[rows, :] = x
        h_ref[pl.ds(rows.start + HALO, rows.size), :] = h

    _row_slabs(TM, SLAB, norm_rows)
    h_ref[0:HALO, :] = normed(xa_ref[...], ya_ref[...])[1]
    h_ref[HALO + TM:, :] = normed(xb_ref[...], yb_ref[...])[1]

    step = 512
    nt_dims = (((1,), (1,)), ((), ()))

    def proj(h, c0, width):
        return lax.dot_general(h, w_ref[c0:c0 + width, :], nt_dims, preferred_element_type=F32)

    h = h_ref[HALO:HALO + TM, :]
    for c0 in range(0, SSM_INNER, step):
        z_ref[:, c0:c0 + step] = proj(h, c0, step).astype(BF16)
    n_dt = 2 * SSM_HEADS
    dt_ref[:, :n_dt] = proj(h, SSM_INNER + SSM_CONV_CH, n_dt)
    dt_ref[:, n_dt:] = jnp.zeros((TM, LANE - n_dt), F32)

    row = lax.broadcasted_iota(jnp.int32, (TM, LANE), 0)
    is_ctx = i < N_CTX_TILES
    j = (i - N_CTX_TILES) % LAT_TILES_PER_SEQ
    in_seq = row & (SEQ - 1)
    lat_first = (j == 0).astype(I32)
    lat_last = (j == LAT_TILES_PER_SEQ - 1).astype(I32)
    starts = jnp.where(is_ctx, (in_seq == 0).astype(I32), (row == 0).astype(I32) * lat_first) != 0
    ends = jnp.where(is_ctx, (in_seq == SEQ - 1).astype(I32), (row == TM - 1).astype(I32) * lat_last) != 0
    h_ext = h_ref[...]
    n_ext = TM + 2 * HALO
    for c0 in range(0, SSM_CONV_CH, step):
        xe = proj(h_ext, SSM_INNER + c0, step)
        for t in range(step // LANE):
            cols = slice(c0 + t * LANE, c0 + (t + 1) * LANE)
            x = xe[:, t * LANE:(t + 1) * LANE]
            above = jnp.where(starts, 0.0, pltpu.roll(x, 1, 0)[HALO:HALO + TM])
            below = jnp.where(ends, 0.0, pltpu.roll(x, n_ext - 1, 0)[HALO:HALO + TM])
            y = (above * cw_ref[0:1, cols] + x[HALO:HALO + TM] * cw_ref[1:2, cols] + below * cw_ref[2:3, cols]
                 + cb_ref[:, cols])
            xc_ref[:, cols] = _silu(y).astype(BF16)


def _in1(x1, y_tok, mods, norm_mix1, w_in, conv_w, conv_b):
    w = jnp.swapaxes(w_in, 0, 1).astype(BF16)
    per_tile = TM // HALO
    last = NT // HALO - 1

    def above(i):
        return (jnp.maximum(i * per_tile - 1, 0), 0)

    def below(i):
        return (jnp.minimum((i + 1) * per_tile, last), 0)

    return pl.pallas_call(
        _in1_kernel,
        grid=(NT // TM,),
        in_specs=[_row_spec(D_MODEL), pl.BlockSpec((HALO, D_MODEL), above), pl.BlockSpec((HALO, D_MODEL), below),
                  _row_spec(HALF), pl.BlockSpec((HALO, HALF), above), pl.BlockSpec((HALO, HALF), below),
                  _mod_spec(0, 5, TM), _const_spec((1, D_MODEL)),
                  _mod_spec(1, 1, TM), _mod_spec(1, 0, TM), _const_spec((ODD_IN, D_MODEL)),
                  _const_spec((3, SSM_CONV_CH)), _const_spec((1, SSM_CONV_CH))],
        out_specs=[_row_spec(D_MODEL), _row_spec(SSM_INNER), _row_spec(SSM_CONV_CH), _row_spec(LANE)],
        out_shape=[jax.ShapeDtypeStruct((NT, D_MODEL), F32), jax.ShapeDtypeStruct((NT, SSM_INNER), BF16),
                   jax.ShapeDtypeStruct((NT, SSM_CONV_CH), BF16), jax.ShapeDtypeStruct((NT, LANE), F32)],
        scratch_shapes=[pltpu.VMEM((TM + 2 * HALO, D_MODEL), BF16)],
        compiler_params=_cparams(("arbitrary",)),
        name="in_proj_ssm",
    )(x1, x1, x1, y_tok, y_tok, y_tok, mods, norm_mix1.reshape(1, D_MODEL), mods, mods, w, conv_w,
      conv_b.reshape(1, SSM_CONV_CH))


HPG = SSM_HEADS // SSM_GROUPS
GW = HPG * SSM_HEAD_DIM


def _ssd_kernel(*refs, has_init, write_state):
    refs = list(refs)
    xcf_ref, xcb_ref, dtf_ref, dtb_ref, bias_ref, alog_ref = refs[:6]
    refs = refs[6:]
    h0f_ref, h0b_ref = (refs.pop(0), refs.pop(0)) if has_init else (None, None)
    yf_ref, yb_ref = refs.pop(0), refs.pop(0)
    houtf_ref, houtb_ref = (refs.pop(0), refs.pop(0)) if write_state else (None, None)
    hf_ref, hb_ref = refs
    c = pl.program_id(1)

    @pl.when(c == 0)
    def _():
        for h_ref, h0_ref in ((hf_ref, h0f_ref), (hb_ref, h0b_ref)):
            for g in range(SSM_GROUPS):
                if has_init:
                    h_ref[g] = h0_ref[g * GW:(g + 1) * GW, :].T
                else:
                    h_ref[g] = jnp.zeros((SSM_STATE, GW), F32)

    _ssd_chunk(xcf_ref, dtf_ref, bias_ref, alog_ref, yf_ref, hf_ref, False)
    _ssd_chunk(xcb_ref, dtb_ref, bias_ref, alog_ref, yb_ref, hb_ref, True)

    if write_state:
        @pl.when(c == pl.num_programs(1) - 1)
        def _():
            for h_ref, hout_ref in ((hf_ref, houtf_ref), (hb_ref, houtb_ref)):
                for g in range(SSM_GROUPS):
                    hout_ref[g * GW:(g + 1) * GW, :] = h_ref[g].T


def _ssd_chunk(xc_ref, dt_ref, bias_ref, alog_ref, y_ref, h_ref, reverse):
    col0 = SSM_HEADS if reverse else 0
    Q = SSM_CHUNK
    dt = jax.nn.softplus(dt_ref[...] + bias_ref[...])
    a = dt * -jnp.exp(alog_ref[...])
    row = lax.broadcasted_iota(jnp.int32, (Q, Q), 0)
    col = lax.broadcasted_iota(jnp.int32, (Q, Q), 1)
    keep = (col >= row) if reverse else (col <= row)
    tri = keep.astype(F32)
    acs = jnp.dot(tri, a, precision=HIGHEST, preferred_element_type=F32)
    edge = (0 if reverse else Q - 1)
    acs_end = acs[edge:edge + 1, :]
    log2e = math.log2(math.e)
    acs2 = acs * log2e
    src_t = (acs2 - jnp.log2(dt)).T
    out_t = (jnp.exp(acs_end - acs) * dt).T
    chunk_decay = jnp.exp(jnp.broadcast_to(acs_end, (8, LANE)))
    nt_dims = (((1,), (1,)), ((), ()))
    lane = lax.broadcasted_iota(jnp.int32, (Q, LANE), 1)
    low = lane < SSM_HEAD_DIM
    low8 = low[:8]

    def two_heads(v):
        zero = jnp.zeros_like(v)
        return jnp.concatenate([jnp.where(low, v, zero), jnp.where(low, zero, v)], axis=0)

    for g in range(SSM_GROUPS):
        bg = xc_ref[:, SSM_INNER + g * SSM_STATE:SSM_INNER + (g + 1) * SSM_STATE]
        cg = xc_ref[:, SSM_INNER + SSM_GROUPS * SSM_STATE + g * SSM_STATE:
                    SSM_INNER + SSM_GROUPS * SSM_STATE + (g + 1) * SSM_STATE]
        cb = lax.dot_general(cg, bg, nt_dims, preferred_element_type=F32).astype(BF16)
        bg_t = bg.astype(F32).T
        for j in range(HPG // 2):
            pair = slice(g * GW + j * LANE, g * GW + (j + 1) * LANE)
            mats, c_in, b_out, cdec = [], [], [], []
            for hh in (2 * j, 2 * j + 1):
                cidx = col0 + g * HPG + hh
                to_l = jnp.broadcast_to(acs2[:, cidx:cidx + 1], (Q, Q))
                lmat = jnp.where(keep, jnp.exp2(to_l - src_t[cidx:cidx + 1, :]), 0.0)
                mats.append(cb * lmat.astype(BF16))
                c_in.append(cg * jnp.exp2(to_l).astype(BF16))
                b_out.append((bg_t * out_t[cidx:cidx + 1, :]).astype(BF16))
                cdec.append(jnp.broadcast_to(chunk_decay[:, cidx:cidx + 1], (8, LANE)))
            hp = h_ref[g, :, j * LANE:(j + 1) * LANE]
            x2 = two_heads(xc_ref[:, pair])
            rhs = jnp.concatenate([x2, two_heads(hp.astype(BF16))], axis=0)
            y = jnp.dot(jnp.concatenate(mats + c_in, axis=1), rhs, preferred_element_type=F32)
            y_ref[:, pair] = y.astype(BF16)
            st = jnp.dot(jnp.concatenate(b_out, axis=1), x2, preferred_element_type=F32)
            h_ref[g, :, j * LANE:(j + 1) * LANE] = hp * jnp.where(low8, cdec[0], cdec[1])[0:1, :] + st


def _ssd(xc, dt_raw, bias128, alog128, h0_fwd, h0_bwd):
    outs = []
    for (row0, nseq, seqlen, has_init, write_state) in ((0, BATCH, SEQ, False, True),
                                                         (N_CTX, DEC_BATCH, DEC_SEQ, True, False)):
        nc = seqlen // SSM_CHUNK
        base = row0 // SSM_CHUNK

        def fwd_in(s, c, nc=nc, base=base):
            return (base + s * nc + c, 0)

        def bwd_in(s, c, nc=nc, base=base):
            return (base + s * nc + nc - 1 - c, 0)

        def fwd_out(s, c, nc=nc):
            return (s * nc + c, 0)

        def bwd_out(s, c, nc=nc):
            return (s * nc + nc - 1 - c, 0)

        state_spec = pl.BlockSpec((None, SSM_INNER, SSM_STATE), lambda s, c: (s, 0, 0))
        y_shape = jax.ShapeDtypeStruct((nseq * seqlen, SSM_INNER), BF16)
        state_shape = jax.ShapeDtypeStruct((nseq, SSM_INNER, SSM_STATE), F32)
        in_specs = [pl.BlockSpec((SSM_CHUNK, SSM_CONV_CH), fwd_in), pl.BlockSpec((SSM_CHUNK, SSM_CONV_CH), bwd_in),
                    pl.BlockSpec((SSM_CHUNK, LANE), fwd_in), pl.BlockSpec((SSM_CHUNK, LANE), bwd_in),
                    _const_spec((1, LANE)), _const_spec((1, LANE))]
        args = [xc, xc, dt_raw, dt_raw, bias128, alog128]
        if has_init:
            in_specs += [state_spec, state_spec]
            args += [h0_fwd, h0_bwd]
        out_specs = [pl.BlockSpec((SSM_CHUNK, SSM_INNER), fwd_out), pl.BlockSpec((SSM_CHUNK, SSM_INNER), bwd_out)]
        out_shape = [y_shape, y_shape]
        if write_state:
            out_specs += [state_spec, state_spec]
            out_shape += [state_shape, state_shape]
        state_scratch = pltpu.VMEM((SSM_GROUPS, SSM_STATE, GW), F32)
        outs.append(pl.pallas_call(
            functools.partial(_ssd_kernel, has_init=has_init, write_state=write_state),
            grid=(nseq, nc),
            in_specs=in_specs,
            out_specs=out_specs,
            out_shape=out_shape,
            scratch_shapes=[state_scratch, state_scratch],
            compiler_params=_cparams(("arbitrary", "arbitrary")),
            name="ssd_%s" % ("lat" if has_init else "ctx"),
        )(*args))
    ctx, lat = outs
    return (ctx[0], lat[0]), (ctx[1], lat[1]), ctx[2], ctx[3]


def _out1_kernel(yfc_ref, yfl_ref, ybc_ref, ybl_ref, xs_ref, z_ref, dskip_ref, ng_ref, wo_ref, x_ref, g1_ref, nf_ref,
                 sc2_ref, sh2_ref, wr_ref, rb_ref, tri_ref, x1_ref, rows_ref, meta_ref, count_ref,
                 carry_ref, hhi_ref, hlo_ref, acc_ref, y_ref):
    is_lat = pl.program_id(0) >= N_CTX_TILES
    dskip = dskip_ref[...]
    gain = ng_ref[...]

    def gate_norm(rows):
        yf = jnp.where(is_lat, yfl_ref[rows, :], yfc_ref[rows, :])
        yb = jnp.where(is_lat, ybl_ref[rows, :], ybc_ref[rows, :])
        y = (yf + yb).astype(F32) + dskip * xs_ref[rows, :].astype(F32)
        y = y * _silu(z_ref[rows, :].astype(F32))
        y_ref[rows, :] = (y * lax.rsqrt(jnp.mean(y * y, axis=-1, keepdims=True) + EPS) * gain).astype(BF16)

    _row_slabs(TM, SLAB // 2, gate_norm)
    acc_ref[...] = jnp.dot(y_ref[...], wo_ref[...], preferred_element_type=F32)
    _residual_router(lambda rows: x_ref[rows, :], acc_ref, g1_ref, nf_ref, sc2_ref, sh2_ref, wr_ref, rb_ref,
                     tri_ref, x1_ref, rows_ref, meta_ref, count_ref, carry_ref, hhi_ref, hlo_ref)


def _out1(yf, yb, xc, z, d_skip, ssm_norm, w_out, x, mods, norm_ffn1, router):
    wr, rb, tri = router
    return pl.pallas_call(
        _out1_kernel,
        grid=(NT // TM,),
        in_specs=_split_row_specs(SSM_INNER) + _split_row_specs(SSM_INNER)
        + [_row_spec(SSM_INNER), _row_spec(SSM_INNER),
           _const_spec((1, SSM_INNER)), _const_spec((1, SSM_INNER)), _const_spec((SSM_INNER, D_MODEL)),
           _row_spec(D_MODEL)] + _epilogue_specs(1),
        out_specs=_EPILOGUE_OUT_SPECS,
        out_shape=_EPILOGUE_OUT_SHAPE,
        scratch_shapes=_EPILOGUE_SCRATCH + [pltpu.VMEM((TM, SSM_INNER), BF16)],
        compiler_params=_cparams(("arbitrary",)),
        name="out_proj_ssm",
    )(yf[0], yf[1], yb[0], yb[1], xc, z, jnp.repeat(d_skip.astype(F32), SSM_HEAD_DIM).reshape(1, SSM_INNER),
      ssm_norm.reshape(1, SSM_INNER), w_out.astype(BF16), x, mods, norm_ffn1.reshape(1, D_MODEL), mods, mods,
      wr, rb, tri)


def _kv_from_lane_major(t):
    return t.reshape(BATCH, 1, N_KV_HEADS, HEAD_DIM, SEQ).transpose(0, 1, 4, 2, 3)


def kernel(x_prompt, x_sample, cache_k, cache_v, state_fwd, state_bwd, c, c_ctx, w_ada, b_ada, norm_mix, norm_ffn,
           w_in_att, q_gain, k_gain, w_fourier, w_out_att, w_in_ssm, conv_w, conv_b, dt_bias_f, dt_bias_b, a_log_f,
           a_log_b, d_skip, ssm_norm, w_out_ssm, w_router, router_bias, w_gate, w_up, w_down, norm_final):
    xp = x_prompt.reshape(N_CTX, D_MODEL)
    xs = x_sample.reshape(N_LAT, D_MODEL)
    cond8 = jnp.zeros((8, D_MODEL), F32).at[0].set(c_ctx).at[1:1 + DEC_BATCH].set(c)
    mods = _ada_mods(cond8, w_ada, b_ada)
    router = _router_operands(w_router, router_bias)
    cos2, sin2 = _rope_tables()

    q, kp, vp, u_ctx, u_lat, new_k, new_v = _in0(xp, xs, mods, norm_mix[0], w_in_att[0], q_gain[0], k_gain[0],
                                                 cos2, sin2)
    att = _attention(q, kp, vp, cache_k[:, 0], cache_v[:, 0])
    mixed = _fourier(u_ctx, u_lat)
    x1, rows, meta, counts = _out0(att, mixed, xp, xs, mods, w_fourier[0], w_out_att[0], norm_ffn[0], router)
    y_tok = _moe(rows, meta, counts, 0, w_gate, w_up, w_down)

    anchor = counts[0, 0] * 0.0
    w_in_1 = w_in_ssm[0] + anchor
    w_out_1 = w_out_ssm[0] + anchor

    x2, z, xc, dt_raw = _in1(x1, y_tok, mods, norm_mix[1], w_in_1, conv_w[0], conv_b[0])
    pad = jnp.zeros((LANE - 2 * SSM_HEADS,), F32)
    bias128 = jnp.concatenate([dt_bias_f[0], dt_bias_b[0], pad]).astype(F32).reshape(1, LANE)
    alog128 = jnp.concatenate([a_log_f[0], a_log_b[0], pad]).astype(F32).reshape(1, LANE)
    yf, yb, sf, sb = _ssd(xc, dt_raw, bias128, alog128, state_fwd.reshape(DEC_BATCH, SSM_INNER, SSM_STATE),
                          state_bwd.reshape(DEC_BATCH, SSM_INNER, SSM_STATE))
    x3, rows, meta, counts = _out1(yf, yb, xc, z, d_skip[0], ssm_norm[0], w_out_1, x2, mods, norm_ffn[1], router)
    y_tok = _moe(rows, meta, counts, 1, w_gate, w_up, w_down)
    y_prompt, y_sample = _final(x3, y_tok, mods, norm_final)

    state_shape = (BATCH, 1, SSM_HEADS, SSM_HEAD_DIM, SSM_STATE)
    return (y_prompt.reshape(BATCH, SEQ, D_MODEL), y_sample.reshape(DEC_BATCH, DEC_SEQ, D_MODEL),
            _kv_from_lane_major(new_k), _kv_from_lane_major(new_v),
            sf.reshape(state_shape), sb.reshape(state_shape))
```
